```python
import math, functools
import jax, jax.numpy as jnp
from jax import lax
import numpy as np

D_MODEL = 2048
BATCH = 2
SEQ = 4096
DEPTH = 2
DEC_BATCH = 128
DEC_SEQ = 1
PAST_LEN = 2048
PAGE_SIZE = 128

ATT_HEAD_DIM = 64
ATT_HEADS_PER_GROUP = 4
ATT_GROUPS = ((128, 1), (512, 4), (2048, 16))
N_ATT_GROUPS = len(ATT_GROUPS)
ATT_WIDTH = N_ATT_GROUPS * ATT_HEADS_PER_GROUP * ATT_HEAD_DIM
ATT_OUT_WIDTH = ATT_HEADS_PER_GROUP * ATT_HEAD_DIM
Q_BLOCK = 128
S5_GROUP_CH = 16
S5_STATE = 64
S5_WIDTH = 640
S5_GROUPS = S5_WIDTH // S5_GROUP_CH
DT_MIN = 1e-3
DT_MAX = 1e-1
LRU_WIDTH = 640
LRU_HEADS = 8
LRU_HEAD_DIM = LRU_WIDTH // LRU_HEADS
LRU_CONV = 4
LRU_C = 8.0
N_BRANCH = 3
OFF_Q = 0
OFF_K = OFF_Q + ATT_WIDTH
OFF_V = OFF_K + ATT_WIDTH
OFF_S5 = OFF_V + ATT_WIDTH
OFF_LRU_X = OFF_S5 + S5_WIDTH
OFF_LRU_G = OFF_LRU_X + LRU_WIDTH
OFF_GATE = OFF_LRU_G + LRU_WIDTH
N_IN = OFF_GATE + N_BRANCH * D_MODEL
D_FF = 5632
N_EXPERTS = 8
TOP_K = 2
D_FF_EXPERT = 2816
N_DENSE = (DEPTH + 1) // 2
N_MOE = DEPTH // 2
EPS = 1e-6
N_STATE = 9

kernel_name = 'hybrid_s5_dilated_attn_rglru_step'


def rms_norm(x, gain):
    x32 = x.astype(jnp.float32)
    y = x32 * lax.rsqrt(jnp.mean(x32 * x32, axis=-1, keepdims=True) + EPS)
    return (y * gain.astype(jnp.float32)).astype(x.dtype)


def _cmul(ar, ai, br, bi):
    return ar * br - ai * bi, ar * bi + ai * br


def dilated_attention(q, k_all, v_all, q_off, dilation, n_keys):
    bsz, tq, nh, hd = q.shape
    qb = Q_BLOCK if tq % Q_BLOCK == 0 else tq
    nb = tq // qb
    offsets = dilation * jnp.arange(n_keys, dtype=jnp.int32)
    q_blocks = q.reshape(bsz, nb, qb, nh, hd).transpose(1, 0, 2, 3, 4)
    starts = q_off + qb * jnp.arange(nb, dtype=jnp.int32)

    def one_block(args):
        qblk, start = args
        rows = start + jnp.arange(qb, dtype=jnp.int32)
        idx = rows[:, None] - offsets[None, :]
        valid = idx >= 0
        idx = jnp.maximum(idx, 0)
        kg = k_all[:, idx]
        vg = v_all[:, idx]
        s = jnp.einsum('bqhd,bqmhd->bqhm', qblk, kg).astype(jnp.float32)
        s = jnp.where(valid[None, :, None, :], s, -jnp.inf)
        lse = jax.nn.logsumexp(s, axis=-1)
        p = jnp.exp(s - lse[..., None]).astype(vg.dtype)
        o = jnp.einsum('bqhm,bqmhd->bqhd', p, vg)
        return o, lse

    o, lse = lax.map(one_block, (q_blocks, starts))
    o = o.transpose(1, 0, 2, 3, 4).reshape(bsz, tq, nh, hd)
    lse = lse.transpose(1, 0, 2, 3).reshape(bsz, tq, nh)
    return o, lse


def s5_ssm(u, lam_re, lam_im, log_dt, b_re, b_im, c_re, c_im, d_skip, h0):
    f32 = jnp.float32
    bsz, t, _ = u.shape
    u32 = u.reshape(bsz, t, S5_GROUPS, S5_GROUP_CH).astype(f32)
    dt = jnp.exp(log_dt.astype(f32))[:, None]
    lr, li = lam_re.astype(f32), lam_im.astype(f32)
    mag = jnp.exp(lr * dt)
    abar_r, abar_i = mag * jnp.cos(li * dt), mag * jnp.sin(li * dt)
    den = lr * lr + li * li
    fr = ((abar_r - 1.0) * lr + abar_i * li) / den
    fi = (abar_i * lr - (abar_r - 1.0) * li) / den
    bbr, bbi = _cmul(fr[..., None], fi[..., None], b_re.astype(f32), b_im.astype(f32))
    bu_r = jnp.einsum('btgc,gpc->btgp', u32, bbr)
    bu_i = jnp.einsum('btgc,gpc->btgp', u32, bbi)
    a_r = jnp.broadcast_to(abar_r, bu_r.shape)
    a_i = jnp.broadcast_to(abar_i, bu_i.shape)

    def combine(e1, e2):
        a1r, a1i, b1r, b1i = e1
        a2r, a2i, b2r, b2i = e2
        ar, ai = _cmul(a2r, a2i, a1r, a1i)
        br, bi = _cmul(a2r, a2i, b1r, b1i)
        return ar, ai, br + b2r, bi + b2i

    pr, pi, sr, si = lax.associative_scan(combine, (a_r, a_i, bu_r, bu_i), axis=1)
    if h0 is not None:
        h0r = h0[..., 0].astype(f32)[:, None]
        h0i = h0[..., 1].astype(f32)[:, None]
        cr, ci = _cmul(pr, pi, h0r, h0i)
        sr, si = sr + cr, si + ci
    y = (jnp.einsum('btgp,gcp->btgc', sr, c_re.astype(f32))
         - jnp.einsum('btgp,gcp->btgc', si, c_im.astype(f32)))
    y = y.reshape(bsz, t, S5_WIDTH).astype(u.dtype) + d_skip * u
    last = jnp.stack([sr[:, -1], si[:, -1]], axis=-1).astype(u.dtype)
    return y, last


def rg_lru(x, conv_w, conv_b, w_a, b_a, w_i, b_i, lam, conv_buf, h0):
    f32 = jnp.float32
    bsz, t, _ = x.shape
    if conv_buf is None:
        full = jnp.pad(x, ((0, 0), (LRU_CONV - 1, 0), (0, 0)))
    else:
        full = jnp.concatenate([conv_buf.astype(x.dtype), x], axis=1)
    xc = conv_b
    for tap in range(LRU_CONV):
        xc = xc + full[:, tap:tap + t] * conv_w[tap]
    new_buf = full[:, -(LRU_CONV - 1):]
    xh = xc.reshape(bsz, t, LRU_HEADS, LRU_HEAD_DIM)
    r = jax.nn.sigmoid(jnp.einsum('bthi,hij->bthj', xh, w_a).reshape(bsz, t, LRU_WIDTH) + b_a)
    ig = jax.nn.sigmoid(jnp.einsum('bthi,hij->bthj', xh, w_i).reshape(bsz, t, LRU_WIDTH) + b_i)
    log_a = (-LRU_C * r.astype(f32)) * jax.nn.softplus(-lam.astype(f32))
    a = jnp.exp(log_a)
    b = jnp.sqrt(-jnp.expm1(2.0 * log_a)) * (ig * xc).astype(f32)

    def combine(e1, e2):
        a1, b1 = e1
        a2, b2 = e2
        return a1 * a2, a2 * b1 + b2

    pa, h = lax.associative_scan(combine, (a, b), axis=1)
    if h0 is not None:
        h = h + pa * h0.astype(f32)[:, None]
    return h.astype(x.dtype), new_buf, h[:, -1].astype(x.dtype)


def token_mixers(xn, lp, past):
    bsz, t, _ = xn.shape
    proj = xn @ lp['w_in']
    hshape = (bsz, t, N_ATT_GROUPS, ATT_HEADS_PER_GROUP, ATT_HEAD_DIM)
    q = proj[..., OFF_Q:OFF_K].reshape(hshape)
    k = proj[..., OFF_K:OFF_V].reshape(hshape)
    v = proj[..., OFF_V:OFF_S5].reshape(hshape)
    q = rms_norm(q, lp['q_gain'][:, None, :]) * (ATT_HEAD_DIM ** -0.5)
    k = rms_norm(k, lp['k_gain'][:, None, :])
    new_state = []
    outs, lses = [], []
    for g, (window, dil) in enumerate(ATT_GROUPS):
        kg, vg = k[:, :, g], v[:, :, g]
        if past is None:
            k_all, v_all, q_off = kg, vg, 0
        else:
            k_all = jnp.concatenate([past[2 * g].astype(kg.dtype), kg], axis=1)
            v_all = jnp.concatenate([past[2 * g + 1].astype(vg.dtype), vg], axis=1)
            q_off = past[2 * g].shape[1]
        o, lse = dilated_attention(q[:, :, g], k_all, v_all, q_off, dil, window // dil + 1)
        outs.append(o)
        lses.append(lse)
        keep = min(window, k_all.shape[1])
        new_state += [k_all[:, -keep:], v_all[:, -keep:]]
    mix_w = jax.nn.softmax(jnp.stack(lses), axis=0)
    stacked = jnp.stack(outs)
    att = jnp.einsum('gbth,gbthd->bthd', mix_w.astype(stacked.dtype), stacked).reshape(bsz, t, ATT_OUT_WIDTH)

    u = proj[..., OFF_S5:OFF_LRU_X]
    s5_y, s5_last = s5_ssm(u, lp['s5_lam_re'], lp['s5_lam_im'], lp['s5_log_dt'], lp['s5_b_re'], lp['s5_b_im'],
                           lp['s5_c_re'], lp['s5_c_im'], lp['s5_d'], None if past is None else past[6])
    s5_y = jax.nn.gelu(s5_y)
    s5_y = s5_y * jax.nn.sigmoid(s5_y @ lp['s5_w_glu'] + lp['s5_b_glu'])

    h, conv_buf, h_last = rg_lru(proj[..., OFF_LRU_X:OFF_LRU_G], lp['lru_conv_w'], lp['lru_conv_b'],
                                 lp['lru_w_a'], lp['lru_b_a'], lp['lru_w_i'], lp['lru_b_i'], lp['lru_lam'],
                                 None if past is None else past[7], None if past is None else past[8])
    lru_y = h * jax.nn.gelu(proj[..., OFF_LRU_G:OFF_GATE])

    gates = jax.nn.sigmoid(proj[..., OFF_GATE:].reshape(bsz, t, N_BRANCH, D_MODEL))
    merged = (gates[:, :, 0] * (att @ lp['w_br_attn'])
              + gates[:, :, 1] * (s5_y @ lp['w_br_s5'])
              + gates[:, :, 2] * (lru_y @ lp['w_br_lru']))
    new_state += [s5_last, conv_buf, h_last]
    return merged @ lp['w_o'], new_state


def swiglu(x, wg, wu, wd):
    return (jax.nn.silu(x @ wg) * (x @ wu)) @ wd


def moe_ffn(x, router_w, router_b, wg, wu, wd):
    logits = (x @ router_w).astype(jnp.float32) + router_b.astype(jnp.float32)
    top_v, top_i = lax.top_k(logits, TOP_K)
    top_w = jax.nn.softmax(top_v, axis=-1)
    comb = jnp.sum(jax.nn.one_hot(top_i, N_EXPERTS, dtype=jnp.float32) * top_w[..., None], axis=-2).astype(x.dtype)
    y = jnp.zeros_like(x)
    for e in range(N_EXPERTS):
        y = y + comb[..., e:e + 1] * swiglu(x, wg[e], wu[e], wd[e])
    return y


def layer_forward(x, lp, past, ffn):
    mix, new_state = token_mixers(rms_norm(x, lp['norm_mix']), lp, past)
    x = x + mix
    x = x + ffn(rms_norm(x, lp['norm_ffn']))
    return x, new_state


def setup_inputs(seed: int = 0) -> dict:
    key = jax.random.key(seed)
    keys = jax.random.split(key, 64)
    counter = [0]

    def nk():
        k = keys[counter[0]]
        counter[0] += 1
        return k

    def nrm(shape, scale=1.0):
        return scale * jax.random.normal(nk(), shape, jnp.float32)

    def gain(shape):
        return 1.0 + nrm(shape, 0.02)

    def kv_shape(window):
        return (DEPTH, DEC_BATCH, min(window, PAST_LEN), ATT_HEADS_PER_GROUP, ATT_HEAD_DIM)

    w0, w1, w2 = ATT_GROUPS[0][0], ATT_GROUPS[1][0], ATT_GROUPS[2][0]
    x_prompt = nrm((BATCH, SEQ, D_MODEL))
    x_sample = nrm((DEC_BATCH, DEC_SEQ, D_MODEL))
    cache_k_w128 = nrm(kv_shape(w0))
    cache_v_w128 = nrm(kv_shape(w0))
    cache_k_w512 = nrm(kv_shape(w1))
    cache_v_w512 = nrm(kv_shape(w1))
    cache_k_w2048 = nrm(kv_shape(w2))
    cache_v_w2048 = nrm(kv_shape(w2))
    state_s5 = nrm((DEPTH, DEC_BATCH, S5_GROUPS, S5_STATE, 2), 0.5)
    state_conv = nrm((DEPTH, DEC_BATCH, LRU_CONV - 1, LRU_WIDTH))
    state_lru = nrm((DEPTH, DEC_BATCH, LRU_WIDTH), 0.5)
    norm_mix = gain((DEPTH, D_MODEL))
    w_in = nrm((DEPTH, D_MODEL, N_IN), D_MODEL ** -0.5)
    q_gain = gain((DEPTH, N_ATT_GROUPS, ATT_HEAD_DIM))
    k_gain = gain((DEPTH, N_ATT_GROUPS, ATT_HEAD_DIM))
    s5_lam_re = -0.5 + nrm((DEPTH, S5_GROUPS, S5_STATE), 0.01)
    s5_lam_im = jnp.pi * jnp.arange(S5_STATE, dtype=jnp.float32) + nrm((DEPTH, S5_GROUPS, S5_STATE), 0.01)
    s5_log_dt = jax.random.uniform(nk(), (DEPTH, S5_GROUPS), jnp.float32, math.log(DT_MIN), math.log(DT_MAX))
    s5_b_re = nrm((DEPTH, S5_GROUPS, S5_STATE, S5_GROUP_CH), (2.0 * S5_GROUP_CH) ** -0.5)
    s5_b_im = nrm((DEPTH, S5_GROUPS, S5_STATE, S5_GROUP_CH), (2.0 * S5_GROUP_CH) ** -0.5)
    s5_c_re = nrm((DEPTH, S5_GROUPS, S5_GROUP_CH, S5_STATE), (2.0 * S5_STATE) ** -0.5)
    s5_c_im = nrm((DEPTH, S5_GROUPS, S5_GROUP_CH, S5_STATE), (2.0 * S5_STATE) ** -0.5)
    s5_d = nrm((DEPTH, S5_WIDTH))
    s5_w_glu = nrm((DEPTH, S5_WIDTH, S5_WIDTH), S5_WIDTH ** -0.5)
    s5_b_glu = nrm((DEPTH, S5_WIDTH), 0.01)
    lru_conv_w = nrm((DEPTH, LRU_CONV, LRU_WIDTH), LRU_CONV ** -0.5)
    lru_conv_b = nrm((DEPTH, LRU_WIDTH), 0.01)
    lru_w_a = nrm((DEPTH, LRU_HEADS, LRU_HEAD_DIM, LRU_HEAD_DIM), LRU_HEAD_DIM ** -0.5)
    lru_b_a = nrm((DEPTH, LRU_WIDTH), 0.01)
    lru_w_i = nrm((DEPTH, LRU_HEADS, LRU_HEAD_DIM, LRU_HEAD_DIM), LRU_HEAD_DIM ** -0.5)
    lru_b_i = nrm((DEPTH, LRU_WIDTH), 0.01)
    a0 = jax.random.uniform(nk(), (DEPTH, LRU_WIDTH), jnp.float32, 0.9, 0.999)
    lru_lam = jnp.log(a0) - jnp.log1p(-a0)
    w_br_attn = nrm((DEPTH, ATT_OUT_WIDTH, D_MODEL), ATT_OUT_WIDTH ** -0.5)
    w_br_s5 = nrm((DEPTH, S5_WIDTH, D_MODEL), S5_WIDTH ** -0.5)
    w_br_lru = nrm((DEPTH, LRU_WIDTH, D_MODEL), LRU_WIDTH ** -0.5)
    w_o = nrm((DEPTH, D_MODEL, D_MODEL), D_MODEL ** -0.5)
    norm_ffn = gain((DEPTH, D_MODEL))
    ffn_w_gate = nrm((N_DENSE, D_MODEL, D_FF), D_MODEL ** -0.5)
    ffn_w_up = nrm((N_DENSE, D_MODEL, D_FF), D_MODEL ** -0.5)
    ffn_w_down = nrm((N_DENSE, D_FF, D_MODEL), D_FF ** -0.5)
    moe_router_w = nrm((N_MOE, D_MODEL, N_EXPERTS), D_MODEL ** -0.5)
    moe_router_b = nrm((N_MOE, N_EXPERTS), 0.01)
    moe_w_gate = nrm((N_MOE, N_EXPERTS, D_MODEL, D_FF_EXPERT), D_MODEL ** -0.5)
    moe_w_up = nrm((N_MOE, N_EXPERTS, D_MODEL, D_FF_EXPERT), D_MODEL ** -0.5)
    moe_w_down = nrm((N_MOE, N_EXPERTS, D_FF_EXPERT, D_MODEL), D_FF_EXPERT ** -0.5)
    return {'x_prompt': x_prompt, 'x_sample': x_sample,
            'cache_k_w128': cache_k_w128, 'cache_v_w128': cache_v_w128,
            'cache_k_w512': cache_k_w512, 'cache_v_w512': cache_v_w512,
            'cache_k_w2048': cache_k_w2048, 'cache_v_w2048': cache_v_w2048,
            'state_s5': state_s5, 'state_conv': state_conv, 'state_lru': state_lru,
            'norm_mix': norm_mix, 'w_in': w_in, 'q_gain': q_gain, 'k_gain': k_gain,
            's5_lam_re': s5_lam_re, 's5_lam_im': s5_lam_im, 's5_log_dt': s5_log_dt,
            's5_b_re': s5_b_re, 's5_b_im': s5_b_im, 's5_c_re': s5_c_re, 's5_c_im': s5_c_im,
            's5_d': s5_d, 's5_w_glu': s5_w_glu, 's5_b_glu': s5_b_glu,
            'lru_conv_w': lru_conv_w, 'lru_conv_b': lru_conv_b, 'lru_w_a': lru_w_a, 'lru_b_a': lru_b_a,
            'lru_w_i': lru_w_i, 'lru_b_i': lru_b_i, 'lru_lam': lru_lam,
            'w_br_attn': w_br_attn, 'w_br_s5': w_br_s5, 'w_br_lru': w_br_lru, 'w_o': w_o,
            'norm_ffn': norm_ffn, 'ffn_w_gate': ffn_w_gate, 'ffn_w_up': ffn_w_up, 'ffn_w_down': ffn_w_down,
            'moe_router_w': moe_router_w, 'moe_router_b': moe_router_b,
            'moe_w_gate': moe_w_gate, 'moe_w_up': moe_w_up, 'moe_w_down': moe_w_down}


def reference(x_prompt, x_sample, cache_k_w128, cache_v_w128, cache_k_w512, cache_v_w512,
              cache_k_w2048, cache_v_w2048, state_s5, state_conv, state_lru,
              norm_mix, w_in, q_gain, k_gain,
              s5_lam_re, s5_lam_im, s5_log_dt, s5_b_re, s5_b_im, s5_c_re, s5_c_im,
              s5_d, s5_w_glu, s5_b_glu,
              lru_conv_w, lru_conv_b, lru_w_a, lru_b_a, lru_w_i, lru_b_i, lru_lam,
              w_br_attn, w_br_s5, w_br_lru, w_o,
              norm_ffn, ffn_w_gate, ffn_w_up, ffn_w_down,
              moe_router_w, moe_router_b, moe_w_gate, moe_w_up, moe_w_down):
    hp, hs = x_prompt, x_sample
    new_p = [[] for _ in range(N_STATE)]
    new_s = [[] for _ in range(N_STATE)]
    for l in range(DEPTH):
        lp = dict(norm_mix=norm_mix[l], w_in=w_in[l], q_gain=q_gain[l], k_gain=k_gain[l],
                  s5_lam_re=s5_lam_re[l], s5_lam_im=s5_lam_im[l], s5_log_dt=s5_log_dt[l],
                  s5_b_re=s5_b_re[l], s5_b_im=s5_b_im[l], s5_c_re=s5_c_re[l], s5_c_im=s5_c_im[l],
                  s5_d=s5_d[l], s5_w_glu=s5_w_glu[l], s5_b_glu=s5_b_glu[l],
                  lru_conv_w=lru_conv_w[l], lru_conv_b=lru_conv_b[l], lru_w_a=lru_w_a[l], lru_b_a=lru_b_a[l],
                  lru_w_i=lru_w_i[l], lru_b_i=lru_b_i[l], lru_lam=lru_lam[l],
                  w_br_attn=w_br_attn[l], w_br_s5=w_br_s5[l], w_br_lru=w_br_lru[l], w_o=w_o[l],
                  norm_ffn=norm_ffn[l])
        j = l // 2
        if l % 2 == 0:
            ffn = functools.partial(swiglu, wg=ffn_w_gate[j], wu=ffn_w_up[j], wd=ffn_w_down[j])
        else:
            ffn = functools.partial(moe_ffn, router_w=moe_router_w[j], router_b=moe_router_b[j],
                                    wg=moe_w_gate[j], wu=moe_w_up[j], wd=moe_w_down[j])
        past = [cache_k_w128[l], cache_v_w128[l], cache_k_w512[l], cache_v_w512[l],
                cache_k_w2048[l], cache_v_w2048[l], state_s5[l], state_conv[l], state_lru[l]]
        hp, st_p = layer_forward(hp, lp, None, ffn)
        hs, st_s = layer_forward(hs, lp, past, ffn)
        for i in range(N_STATE):
            new_p[i].append(st_p[i])
            new_s[i].append(st_s[i])
    (pk128, pv128, pk512, pv512, pk2048, pv2048, ps5, pconv, plru) = [jnp.stack(a) for a in new_p]
    (sk128, sv128, sk512, sv512, sk2048, sv2048, ss5, sconv, slru) = [jnp.stack(a) for a in new_s]
    return (hp, hs, pk128, pv128, pk512, pv512, pk2048, pv2048, ps5, pconv, plru,
            sk128, sv128, sk512, sv512, sk2048, sv2048, ss5, sconv, slru)
```

```python
import functools
import math

import jax
import jax.numpy as jnp
from jax import lax
from jax.experimental import pallas as pl
from jax.experimental.pallas import tpu as pltpu

F32 = jnp.float32
BF16 = jnp.bfloat16

D_MODEL = 2048
BATCH = 2
SEQ = 4096
DEPTH = 2
DEC_BATCH = 128
HEAD_DIM = 64
HEADS = 4
GROUP_W = HEADS * HEAD_DIM
ATT_GROUPS = ((128, 1), (512, 4), (2048, 16))
N_GROUPS = 3
ATT_WIDTH = N_GROUPS * GROUP_W
N_KEYS = 128
S5_WIDTH = 640
S5_GROUPS = 40
S5_CH = 16
S5_STATE = 64
LRU_WIDTH = 640
LRU_HEADS = 8
LRU_C = 8.0
OFF_MID = 3 * ATT_WIDTH
MID_WIDTH = S5_WIDTH + 2 * LRU_WIDTH
OFF_GATE = OFF_MID + MID_WIDTH
N_EXPERTS = 8
D_FF_EXPERT = 2816
EPS = 1e-6
LANES = 128

M_PROMPT = BATCH * SEQ

S5_CHUNK = 16
LRU_SEG = 32
NEG_BIG = -1e30

VMEM_LIMIT_BYTES = 56 * 1024 * 1024


def _call(body, *, grid, in_specs, out_specs, out_shape, name, scratch_shapes=(), num_scalar_prefetch=0,
          input_output_aliases=None):
    params = pltpu.CompilerParams(dimension_semantics=("arbitrary",) * len(grid),
                                  vmem_limit_bytes=VMEM_LIMIT_BYTES)
    kwargs = {}
    if input_output_aliases:
        kwargs["input_output_aliases"] = input_output_aliases
    if num_scalar_prefetch:
        grid_spec = pltpu.PrefetchScalarGridSpec(
            num_scalar_prefetch=num_scalar_prefetch, grid=grid, in_specs=in_specs,
            out_specs=out_specs, scratch_shapes=scratch_shapes)
        return pl.pallas_call(body, grid_spec=grid_spec, out_shape=out_shape,
                              compiler_params=params, name=name, **kwargs)
    return pl.pallas_call(body, grid=grid, in_specs=in_specs, out_specs=out_specs,
                          out_shape=out_shape, scratch_shapes=scratch_shapes,
                          compiler_params=params, name=name, **kwargs)


def _dot(a, b):
    return jnp.dot(a, b, preferred_element_type=F32)


def _split2(x):
    hi = x.astype(BF16)
    lo = (x - hi.astype(F32)).astype(BF16)
    return hi, lo


def _dot_sel(x, sel, parts=2):
    acc = None
    r = x
    for _ in range(parts):
        piece = r.astype(BF16)
        r = r - piece.astype(F32)
        d = _dot(piece, sel)
        acc = d if acc is None else acc + d
    return acc


def _dot3(a, b):
    ah, al = _split2(a)
    bh, bl = _split2(b)
    return _dot(ah, bh) + (_dot(al, bh) + _dot(ah, bl))


def _sigmoid(x):
    return 1.0 / (1.0 + jnp.exp(-x))


def _gelu(x):
    c = math.sqrt(2.0 / math.pi)
    return 0.5 * x * (1.0 + jnp.tanh(c * (x + 0.044715 * (x * x * x))))


def _rmsnorm_body(x_ref, g_ref, o_ref):
    x = x_ref[...]
    y = x * lax.rsqrt(jnp.mean(x * x, axis=-1, keepdims=True) + EPS)
    o_ref[...] = (y * g_ref[...]).astype(o_ref.dtype)


def _rmsnorm_router_body(x_ref, g_ref, rw_ref, rb_ref, o_ref, comb_ref, idx_ref):
    x = x_ref[...]
    y = x * lax.rsqrt(jnp.mean(x * x, axis=-1, keepdims=True) + EPS)
    xn = y * g_ref[...]
    o_ref[...] = xn.astype(o_ref.dtype)
    logits = _dot(xn.astype(BF16), rw_ref[...].astype(BF16)) + rb_ref[...]
    lane = lax.broadcasted_iota(jnp.int32, logits.shape, 1).astype(F32)
    m1 = jnp.max(logits, axis=1, keepdims=True)
    i1 = jnp.min(jnp.where(logits == m1, lane, float(LANES)), axis=1, keepdims=True)
    rest = jnp.where(lane == i1, -jnp.inf, logits)
    m2 = jnp.max(rest, axis=1, keepdims=True)
    i2 = jnp.min(jnp.where(rest == m2, lane, float(LANES)), axis=1, keepdims=True)
    e = jnp.exp(m2 - m1)
    w1 = 1.0 / (1.0 + e)
    w2 = e / (1.0 + e)
    comb_ref[...] = jnp.where(lane == i1, w1, 0.0) + jnp.where(lane == i2, w2, 0.0)
    idx_ref[...] = jnp.where(lane == 0.0, i1, jnp.where(lane == 1.0, i2, 0.0)).astype(jnp.int32)


def rmsnorm(x, gain, tm, out_dtype):
    m, d = x.shape
    return _call(
        _rmsnorm_body, grid=(m // tm,),
        in_specs=[pl.BlockSpec((tm, d), lambda i: (i, 0)), pl.BlockSpec((1, d), lambda i: (0, 0))],
        out_specs=pl.BlockSpec((tm, d), lambda i: (i, 0)),
        out_shape=jax.ShapeDtypeStruct((m, d), out_dtype), name="rmsnorm")(x, gain.reshape(1, d))


def rmsnorm_router(x, gain, router_w, router_b, tm):
    m, d = x.shape
    ne = router_w.shape[1]
    rw = jnp.zeros((d, LANES), F32).at[:, :ne].set(router_w)
    rb = jnp.full((1, LANES), NEG_BIG, F32).at[0, :ne].set(router_b)
    row = lambda i: (i, 0)
    fixed = lambda i: (0, 0)
    return _call(
        _rmsnorm_router_body, grid=(m // tm,),
        in_specs=[pl.BlockSpec((tm, d), row), pl.BlockSpec((1, d), fixed),
                  pl.BlockSpec((d, LANES), fixed), pl.BlockSpec((1, LANES), fixed)],
        out_specs=[pl.BlockSpec((tm, d), row), pl.BlockSpec((tm, LANES), row), pl.BlockSpec((tm, LANES), row)],
        out_shape=[jax.ShapeDtypeStruct((m, d), BF16), jax.ShapeDtypeStruct((m, LANES), F32),
                   jax.ShapeDtypeStruct((m, LANES), jnp.int32)],
        name="rmsnorm_router")(x, gain.reshape(1, d), rw, rb)


def _mm_body(*refs, n_x, w_of_x, n_te, n_re, epilogue, n_ptiles):
    n_w = len(w_of_x)
    pos = 0
    xp_refs = refs[pos:pos + n_x]; pos += n_x
    xs_refs = refs[pos:pos + n_x]; pos += n_x
    w_refs = refs[pos:pos + n_w]; pos += n_w
    tep_refs = refs[pos:pos + n_te]; pos += n_te
    tes_refs = refs[pos:pos + n_te]; pos += n_te
    re_refs = refs[pos:pos + n_re]; pos += n_re
    op_ref, os_ref = refs[pos], refs[pos + 1]; pos += 2
    wb_refs = refs[pos:pos + n_w]
    i = pl.program_id(1)

    @pl.when(i == 0)
    def _():
        for w_ref, wb_ref in zip(w_refs, wb_refs):
            wb_ref[...] = w_ref[...].astype(BF16)

    def rows(x_refs, te_refs, o_ref):
        accs = [_dot(x_refs[xi][...].astype(BF16), wb_ref[...]) for xi, wb_ref in zip(w_of_x, wb_refs)]
        extras = [e[...] for e in te_refs] + [e[...] for e in re_refs]
        o_ref[...] = epilogue(accs, extras).astype(o_ref.dtype)

    @pl.when(i < n_ptiles)
    def _():
        rows(xp_refs, tep_refs, op_ref)

    @pl.when(i == n_ptiles)
    def _():
        rows(xs_refs, tes_refs, os_ref)


def matmul_ws(xps, xss, ws, *, w_of_x, tm, tn, out_dtype, epilogue, tile_extras=(), row_extras=(), name,
              out_dtype_s=None):
    mp = xps[0].shape[0]
    ms = xss[0].shape[0]
    n = ws[0][0].shape[-1]
    n_ptiles = mp // tm
    last = n_ptiles - 1
    grid = (n // tn, n_ptiles + 1)
    in_specs = [pl.BlockSpec((tm, x.shape[1]), lambda j, i: (jnp.minimum(i, last), 0)) for x in xps]
    in_specs += [pl.BlockSpec((ms, x.shape[1]), lambda j, i: (0, 0)) for x in xss]
    scratch = []
    for w, lead in ws:
        k = w.shape[-2]
        block = (None,) * len(lead) + (k, tn)
        in_specs.append(pl.BlockSpec(block, lambda j, i, lead=lead: lead + (0, j)))
        scratch.append(pltpu.VMEM((k, tn), BF16))
    for _, _, off in tile_extras:
        in_specs.append(pl.BlockSpec((tm, tn), lambda j, i, off=off: (jnp.minimum(i, last), off + j)))
    for _, _, off in tile_extras:
        in_specs.append(pl.BlockSpec((ms, tn), lambda j, i, off=off: (0, off + j)))
    for _ in row_extras:
        in_specs.append(pl.BlockSpec((1, tn), lambda j, i: (0, j)))
    body = functools.partial(_mm_body, n_x=len(xps), w_of_x=tuple(w_of_x), n_te=len(tile_extras),
                             n_re=len(row_extras), epilogue=epilogue, n_ptiles=n_ptiles)
    return _call(
        body, grid=grid, in_specs=in_specs,
        out_specs=[pl.BlockSpec((tm, tn), lambda j, i: (jnp.minimum(i, last), j)),
                   pl.BlockSpec((ms, tn), lambda j, i: (0, j))],
        out_shape=[jax.ShapeDtypeStruct((mp, n), out_dtype),
                   jax.ShapeDtypeStruct((ms, n), out_dtype_s or out_dtype)],
        scratch_shapes=scratch, name=name,
    )(*xps, *xss, *[w for w, _ in ws], *[a for a, _, _ in tile_extras], *[a for _, a, _ in tile_extras],
      *row_extras)


def _ep_plain(accs, extras):
    return accs[0]


def _ep_sigmoid(accs, extras):
    return _sigmoid(accs[0])


def _ep_residual(accs, extras):
    return extras[0] + accs[0]


def _ep_swiglu(accs, extras):
    g, u = accs
    return (g * _sigmoid(g)) * u


def _ep_merge(accs, extras):
    return (extras[0].astype(F32) * accs[0] + extras[1].astype(F32) * accs[1]
            + extras[2].astype(F32) * accs[2])


def _gmm_body(te_ref, nu_ref, *refs, n_te, n_ce, epilogue):
    x_ref, w_ref = refs[0], refs[1]
    extra_refs = refs[2:2 + n_te + n_ce]
    o_ref = refs[2 + n_te + n_ce]
    wb_ref = refs[3 + n_te + n_ce]
    i = pl.program_id(1)
    new_expert = jnp.logical_or(i == 0, te_ref[i] != te_ref[jnp.maximum(i - 1, 0)])

    @pl.when(new_expert)
    def _():
        wb_ref[...] = w_ref[...].astype(BF16)

    @pl.when(i < nu_ref[0])
    def _():
        acc = _dot(x_ref[...], wb_ref[...])
        o_ref[...] = epilogue([acc], [e[...] for e in extra_refs]).astype(o_ref.dtype)

    @pl.when(i >= nu_ref[0])
    def _():
        o_ref[...] = jnp.zeros(o_ref.shape, o_ref.dtype)


def grouped_matmul_ws(tile_expert, n_used, x, w, lead, *, tm, tn, out_dtype, epilogue, tile_extras=(),
                      col_extras=(), name):
    m, k = x.shape
    n = w.shape[-1]
    grid = (n // tn, m // tm)
    block = (None,) * (len(lead) + 1) + (k, tn)
    in_specs = [pl.BlockSpec((tm, k), lambda j, i, te, nu: (i, 0)),
                pl.BlockSpec(block, lambda j, i, te, nu: lead + (te[i], 0, j))]
    in_specs += [pl.BlockSpec((tm, tn), lambda j, i, te, nu: (i, j)) for _ in tile_extras]
    in_specs += [pl.BlockSpec((tm, 1), lambda j, i, te, nu: (i, 0)) for _ in col_extras]
    body = functools.partial(_gmm_body, n_te=len(tile_extras), n_ce=len(col_extras), epilogue=epilogue)
    return _call(
        body, grid=grid, in_specs=in_specs,
        out_specs=pl.BlockSpec((tm, tn), lambda j, i, te, nu: (i, j)),
        out_shape=jax.ShapeDtypeStruct((m, n), out_dtype), scratch_shapes=[pltpu.VMEM((k, tn), BF16)],
        num_scalar_prefetch=2, name=name,
    )(tile_expert, n_used, x, w, *tile_extras, *col_extras)


def _ep_silu_times(accs, extras):
    g = extras[0].astype(F32)
    return (g * _sigmoid(g)) * accs[0]


def _ep_scale_rows(accs, extras):
    return extras[0] * accs[0]


def _qknorm_body(q_ref, k_ref, qg_ref, kg_ref, seg_ref, qo_ref, ko_ref):
    seg = seg_ref[...]
    for src, gain, dst, scale in ((q_ref, qg_ref, qo_ref, HEAD_DIM ** -0.5), (k_ref, kg_ref, ko_ref, 1.0)):
        for g in range(N_GROUPS):
            cols = slice(GROUP_W * g, GROUP_W * (g + 1))
            x = src[:, cols]
            ms = _dot_sel(x * x, seg, 3)
            y = x * lax.rsqrt(ms + EPS) * gain[:, cols]
            dst[:, cols] = (y * scale).astype(dst.dtype)


def qk_norm(qkv, q_gain, k_gain, tm, q_dtype):
    m = qkv.shape[0]
    head = jnp.arange(GROUP_W) // HEAD_DIM
    seg = jnp.where(head[:, None] == head[None, :], 1.0 / HEAD_DIM, 0.0).astype(BF16)
    qg = jnp.broadcast_to(q_gain[:, None, :], (N_GROUPS, HEADS, HEAD_DIM)).reshape(1, ATT_WIDTH)
    kg = jnp.broadcast_to(k_gain[:, None, :], (N_GROUPS, HEADS, HEAD_DIM)).reshape(1, ATT_WIDTH)
    fixed = lambda i: (0, 0)
    return _call(
        _qknorm_body, grid=(m // tm,),
        in_specs=[pl.BlockSpec((tm, ATT_WIDTH), lambda i: (i, 0)), pl.BlockSpec((tm, ATT_WIDTH), lambda i: (i, 1)),
                  pl.BlockSpec((1, ATT_WIDTH), fixed), pl.BlockSpec((1, ATT_WIDTH), fixed),
                  pl.BlockSpec((GROUP_W, GROUP_W), fixed)],
        out_specs=[pl.BlockSpec((tm, ATT_WIDTH), lambda i: (i, 0)), pl.BlockSpec((tm, ATT_WIDTH), lambda i: (i, 0))],
        out_shape=[jax.ShapeDtypeStruct((m, ATT_WIDTH), q_dtype), jax.ShapeDtypeStruct((m, ATT_WIDTH), F32)],
        name="qk_norm")(qkv, qkv, qg, kg, seg)


def _attn_body(q_ref, kp_ref, kc_ref, vp_ref, vc_ref, o_ref, lse_ref):
    a = pl.program_id(1)
    tq = q_ref.shape[1]
    q = q_ref[0]
    k2 = jnp.concatenate([kp_ref[0], kc_ref[0]], axis=0).astype(BF16)
    v2 = jnp.concatenate([vp_ref[0], vc_ref[0]], axis=0).astype(BF16)
    r = lax.broadcasted_iota(jnp.int32, (tq, 2 * tq), 0)
    c = lax.broadcasted_iota(jnp.int32, (tq, 2 * tq), 1)
    valid = (c >= r) & (c <= r + N_KEYS) & ((a > 0) | (c >= tq))
    head = lax.broadcasted_iota(jnp.int32, (1, GROUP_W), 1) // HEAD_DIM
    o = jnp.zeros((tq, GROUP_W), F32)
    lse = jnp.zeros((tq, GROUP_W), F32)
    for h in range(HEADS):
        qh = jnp.where(head == h, q, jnp.zeros_like(q))
        s = lax.dot_general(qh, k2, (((1,), (1,)), ((), ())), preferred_element_type=F32)
        s = jnp.where(valid, s, NEG_BIG)
        m = jnp.max(s, axis=1, keepdims=True)
        p = jnp.exp(s - m)
        l = jnp.sum(p, axis=1, keepdims=True)
        oh = _dot(p.astype(BF16), v2)
        o = jnp.where(head == h, oh / l, o)
        lse = jnp.where(head == h, m + jnp.log(l), lse)
    o_ref[0] = o
    lse_ref[0] = lse


def prompt_attention(qd, kd, vd):
    s, length, _ = qd.shape
    tq = N_KEYS
    cur = lambda b, a: (b, a, 0)
    prev = lambda b, a: (b, jnp.maximum(a - 1, 0), 0)
    blk = (1, tq, GROUP_W)
    return _call(
        _attn_body, grid=(s, length // tq),
        in_specs=[pl.BlockSpec(blk, cur), pl.BlockSpec(blk, prev), pl.BlockSpec(blk, cur),
                  pl.BlockSpec(blk, prev), pl.BlockSpec(blk, cur)],
        out_specs=[pl.BlockSpec(blk, cur), pl.BlockSpec(blk, cur)],
        out_shape=[jax.ShapeDtypeStruct((s, length, GROUP_W), F32)] * 2,
        name="prompt_attention")(qd, kd, kd, vd, vd)


def _dec_attn_body(*refs, dil, aliased):
    if aliased:
        qkv_ref, ck_ref, cv_ref, _, _, o_ref, lse_ref, nk_ref, nv_ref = refs
    else:
        qkv_ref, ck_ref, cv_ref, o_ref, lse_ref, nk_ref, nv_ref = refs
    bs = ck_ref.shape[1]
    n_chunk = ck_ref.shape[4] // LANES
    lane = lax.broadcasted_iota(jnp.int32, (1, LANES), 1)
    key_lane = (lane % dil) == 0
    not_last = lax.broadcasted_iota(jnp.int32, (HEAD_DIM, LANES), 1) < LANES - 1
    for s in range(bs):
        for h in range(HEADS):
            rows = slice(HEAD_DIM * h, HEAD_DIM * (h + 1))
            qb = qkv_ref[s, rows, :]
            kn = qkv_ref[s, GROUP_W + HEAD_DIM * h:GROUP_W + HEAD_DIM * (h + 1), :]
            vn = qkv_ref[s, 2 * GROUP_W + HEAD_DIM * h:2 * GROUP_W + HEAD_DIM * (h + 1), :]
            chunk = lambda ref, c: ref[0, s, h, :, c * LANES:(c + 1) * LANES]
            scores = [jnp.where(key_lane, jnp.sum(chunk(ck_ref, c) * qb, axis=0, keepdims=True), NEG_BIG)
                      for c in range(n_chunk)]
            s_self = jnp.sum(kn * qb, axis=0, keepdims=True)
            m = s_self
            for sc in scores:
                m = jnp.maximum(m, jnp.max(sc, axis=1, keepdims=True))
            p_self = jnp.exp(s_self - m)
            l = p_self
            acc = None
            for c in range(n_chunk):
                p = jnp.exp(scores[c] - m)
                l = l + jnp.sum(p, axis=1, keepdims=True)
                pv = p * chunk(cv_ref, c)
                acc = pv if acc is None else acc + pv
            o = (jnp.sum(acc, axis=1, keepdims=True) + p_self[:, 0:1] * vn[:, 0:1]) / l[:, 0:1]
            o_ref[s, rows, :] = o
            lse_ref[s, rows, :] = jnp.broadcast_to(m[:, 0:1] + jnp.log(l[:, 0:1]), (HEAD_DIM, 1))
            for src, dst, new in ((ck_ref, nk_ref, kn), (cv_ref, nv_ref, vn)):
                rolled = [pltpu.roll(chunk(src, c), LANES - 1, axis=1) for c in range(n_chunk)] + [new]
                for c in range(n_chunk):
                    dst[0, s, h, :, c * LANES:(c + 1) * LANES] = jnp.where(not_last, rolled[c], rolled[c + 1])


def decode_attention(layer, g, qkv_cols, cache_k, cache_v, prev_k, prev_v, bs):
    window, dil = ATT_GROUPS[g]
    cblk = (1, bs, HEADS, HEAD_DIM, window)
    cmap = lambda i: (layer, i, 0, 0, 0)
    col = pl.BlockSpec((bs, GROUP_W, 1), lambda i: (i, 0, 0))
    in_specs = [pl.BlockSpec((bs, 3 * GROUP_W, LANES), lambda i: (i, 0, 0)),
                pl.BlockSpec(cblk, cmap), pl.BlockSpec(cblk, cmap)]
    args = [qkv_cols, cache_k, cache_v]
    aliases = None
    if prev_k is not None:
        in_specs += [pl.BlockSpec(memory_space=pl.ANY), pl.BlockSpec(memory_space=pl.ANY)]
        args += [prev_k, prev_v]
        aliases = {3: 2, 4: 3}
    return _call(
        functools.partial(_dec_attn_body, dil=dil, aliased=prev_k is not None), grid=(DEC_BATCH // bs,),
        in_specs=in_specs,
        out_specs=[col, col, pl.BlockSpec(cblk, cmap), pl.BlockSpec(cblk, cmap)],
        out_shape=[jax.ShapeDtypeStruct((DEC_BATCH, GROUP_W, 1), F32)] * 2
        + [jax.ShapeDtypeStruct(cache_k.shape, F32)] * 2,
        input_output_aliases=aliases, name="decode_attention")(*args)


def _mix_body(o0_ref, o1_ref, o2_ref, l0_ref, l1_ref, l2_ref, att_ref):
    l0, l1, l2 = l0_ref[...], l1_ref[...], l2_ref[...]
    m = jnp.maximum(jnp.maximum(l0, l1), l2)
    e0, e1, e2 = jnp.exp(l0 - m), jnp.exp(l1 - m), jnp.exp(l2 - m)
    num = e0 * o0_ref[...] + e1 * o1_ref[...] + e2 * o2_ref[...]
    att_ref[...] = (num / (e0 + e1 + e2)).astype(att_ref.dtype)


def mix_groups(outs, lses, tm, out_dtype):
    m = outs[0].shape[0]
    spec = pl.BlockSpec((tm, GROUP_W), lambda i: (i, 0))
    return _call(_mix_body, grid=(m // tm,), in_specs=[spec] * 6, out_specs=spec,
                 out_shape=jax.ShapeDtypeStruct((m, GROUP_W), out_dtype), name="mix_groups")(*outs, *lses)


def _s5_local_body(u_ref, wst_ref, s_ref):
    s_ref[0] = _dot3(u_ref[0], wst_ref[0])


def _s5_out_body(u_ref, hp_ref, toep_ref, wout_ref, y_ref):
    y_ref[0] = _dot3(u_ref[0], toep_ref[0]) + _dot3(hp_ref[0], wout_ref[0])


def _s5_carry_body(s_ref, h0_ref, a1_ref, a2_ref, hp_ref, hl_ref):
    bb, nc, ng, w = s_ref.shape
    a1 = jnp.broadcast_to(a1_ref[...][None], (bb, ng, w)).reshape(bb * ng, w)
    a2 = jnp.broadcast_to(a2_ref[...][None], (bb, ng, w)).reshape(bb * ng, w)

    def step(c, h):
        hp_ref[:, pl.ds(c, 1)] = h.reshape(bb, 1, ng, w)
        s = s_ref[:, pl.ds(c, 1)].reshape(bb * ng, w)
        return a1 * h + a2 * pltpu.roll(h, w // 2, axis=1) + s

    h = lax.fori_loop(0, nc, step, h0_ref[...].reshape(bb * ng, w))
    hl_ref[...] = h.reshape(bb, ng, w)


def s5_weights(lam_re, lam_im, log_dt, b_re, b_im, c_re, c_im, chunk, pad):
    hp = lax.Precision.HIGHEST
    dt = jnp.exp(log_dt)[:, None]
    tau = jnp.arange(chunk + 1, dtype=F32)[:, None, None]
    mag = jnp.exp(lam_re * dt * tau)
    pw_r = mag * jnp.cos(lam_im * dt * tau)
    pw_i = mag * jnp.sin(lam_im * dt * tau)
    abar_r, abar_i = pw_r[1], pw_i[1]
    den = lam_re * lam_re + lam_im * lam_im
    fr = ((abar_r - 1.0) * lam_re + abar_i * lam_im) / den
    fi = (abar_i * lam_re - (abar_r - 1.0) * lam_im) / den
    bbr = fr[..., None] * b_re - fi[..., None] * b_im
    bbi = fr[..., None] * b_im + fi[..., None] * b_re
    ab_r = pw_r[..., None] * bbr - pw_i[..., None] * bbi
    ab_i = pw_r[..., None] * bbi + pw_i[..., None] * bbr
    kern = (jnp.einsum("gop,tgpc->tgoc", c_re, ab_r[:chunk], precision=hp)
            - jnp.einsum("gop,tgpc->tgoc", c_im, ab_i[:chunk], precision=hp))
    s_idx = jnp.arange(chunk)[:, None]
    t_idx = jnp.arange(chunk)[None, :]
    lag = t_idx - s_idx
    k_st = kern[jnp.maximum(lag, 0)]
    k_st = jnp.where((lag >= 0)[:, :, None, None, None], k_st, 0.0)
    toep = k_st.transpose(2, 0, 4, 1, 3).reshape(S5_GROUPS, chunk * S5_CH, chunk * S5_CH)
    rev = chunk - 1 - jnp.arange(chunk)
    wst = jnp.concatenate([ab_r[rev], ab_i[rev]], axis=2)
    wst = wst.transpose(1, 0, 3, 2).reshape(S5_GROUPS, chunk * S5_CH, 2 * S5_STATE)
    ar, ai = pw_r[1:, :, None, :], pw_i[1:, :, None, :]
    co_r = c_re[None] * ar - c_im[None] * ai
    co_i = -c_re[None] * ai - c_im[None] * ar
    wout = jnp.concatenate([co_r, co_i], axis=3)
    wout = wout.transpose(1, 3, 0, 2).reshape(S5_GROUPS, 2 * S5_STATE, chunk * S5_CH)
    width = chunk * S5_CH
    if width < pad:
        extra = pad - width
        toep = jnp.pad(toep, ((0, 0), (0, extra), (0, extra)))
        wst = jnp.pad(wst, ((0, 0), (0, extra), (0, 0)))
        wout = jnp.pad(wout, ((0, 0), (0, 0), (0, extra)))
    a1 = jnp.concatenate([pw_r[chunk], pw_r[chunk]], axis=1)
    a2 = jnp.concatenate([-pw_i[chunk], pw_i[chunk]], axis=1)
    return toep, wst, wout, a1, a2


def s5_scan(u_g, h0, weights, bb):
    toep, wst, wout, a1, a2 = weights
    ng, rows, width = u_g.shape
    nseq = h0.shape[0]
    nc = rows // nseq
    w2 = 2 * S5_STATE
    gspec = lambda shape: pl.BlockSpec((1,) + shape, lambda g: (g, 0, 0))
    s_loc = _call(
        _s5_local_body, grid=(ng,), in_specs=[gspec((rows, width)), gspec((width, w2))],
        out_specs=gspec((rows, w2)), out_shape=jax.ShapeDtypeStruct((ng, rows, w2), F32),
        name="s5_local")(u_g, wst)
    s_seq = s_loc.reshape(ng, nseq, nc, w2).transpose(1, 2, 0, 3)
    seq_spec = pl.BlockSpec((bb, nc, ng, w2), lambda b: (b, 0, 0, 0))
    st_spec = pl.BlockSpec((bb, ng, w2), lambda b: (b, 0, 0))
    a_spec = pl.BlockSpec((ng, w2), lambda b: (0, 0))
    h_prev, h_last = _call(
        _s5_carry_body, grid=(nseq // bb,), in_specs=[seq_spec, st_spec, a_spec, a_spec],
        out_specs=[seq_spec, st_spec],
        out_shape=[jax.ShapeDtypeStruct((nseq, nc, ng, w2), F32), jax.ShapeDtypeStruct((nseq, ng, w2), F32)],
        name="s5_carry")(s_seq, h0, a1, a2)
    hp_g = h_prev.transpose(2, 0, 1, 3).reshape(ng, rows, w2)
    y_g = _call(
        _s5_out_body, grid=(ng,),
        in_specs=[gspec((rows, width)), gspec((rows, w2)), gspec((width, width)), gspec((w2, width))],
        out_specs=gspec((rows, width)), out_shape=jax.ShapeDtypeStruct((ng, rows, width), F32),
        name="s5_out")(u_g, hp_g, toep, wout)
    return y_g, h_last


def _s5_post_body(y_ref, u_ref, d_ref, w_ref, b_ref, o_ref):
    z = _gelu(y_ref[...] + d_ref[...] * u_ref[...])
    lin = _dot(z.astype(BF16), w_ref[...].astype(BF16))
    o_ref[...] = (z * _sigmoid(lin + b_ref[...])).astype(o_ref.dtype)


def s5_post(y, mid, d_skip, w_glu, b_glu, tm):
    m = y.shape[0]
    row = lambda i: (i, 0)
    fixed = lambda i: (0, 0)
    return _call(
        _s5_post_body, grid=(m // tm,),
        in_specs=[pl.BlockSpec((tm, S5_WIDTH), row), pl.BlockSpec((tm, S5_WIDTH), row),
                  pl.BlockSpec((1, S5_WIDTH), fixed), pl.BlockSpec((S5_WIDTH, S5_WIDTH), fixed),
                  pl.BlockSpec((1, S5_WIDTH), fixed)],
        out_specs=pl.BlockSpec((tm, S5_WIDTH), row),
        out_shape=jax.ShapeDtypeStruct((m, S5_WIDTH), BF16),
        name="s5_post")(y, mid, d_skip.reshape(1, -1), w_glu, b_glu.reshape(1, -1))


def _s5_decode_body(u_ref, hr_ref, hi_ref, ar_ref, ai_ref, bb_ref, cc_ref, y_ref, sr_ref, si_ref):
    n = hr_ref.shape[1]
    bu = _dot(u_ref[...].astype(BF16), bb_ref[...])
    ar, ai = ar_ref[...], ai_ref[...]
    hr, hi = hr_ref[...], hi_ref[...]
    sr = bu[:, :n] + (ar * hr - ai * hi)
    si = bu[:, n:] + (ar * hi + ai * hr)
    sr_ref[...] = sr
    si_ref[...] = si
    y_ref[...] = _dot(jnp.concatenate([sr, si], axis=1).astype(BF16), cc_ref[...])


def s5_decode(mid, state, lam_re, lam_im, log_dt, b_re, b_im, c_re, c_im):
    dt = jnp.exp(log_dt)[:, None]
    mag = jnp.exp(lam_re * dt)
    abar_r, abar_i = mag * jnp.cos(lam_im * dt), mag * jnp.sin(lam_im * dt)
    den = lam_re * lam_re + lam_im * lam_im
    fr = ((abar_r - 1.0) * lam_re + abar_i * lam_im) / den
    fi = (abar_i * lam_re - (abar_r - 1.0) * lam_im) / den
    bbr = fr[..., None] * b_re - fi[..., None] * b_im
    bbi = fr[..., None] * b_im + fi[..., None] * b_re
    eye = jnp.eye(S5_GROUPS, dtype=F32)
    n_state = S5_GROUPS * S5_STATE
    in_map = lambda w: jnp.einsum("gpc,gk->gckp", w, eye).reshape(S5_WIDTH, n_state)
    out_map = lambda w: jnp.einsum("gop,gk->gpko", w, eye).reshape(n_state, S5_WIDTH)
    bb = jnp.concatenate([in_map(bbr), in_map(bbi)], axis=1).astype(BF16)
    cc = jnp.concatenate([out_map(c_re), -out_map(c_im)], axis=0).astype(BF16)
    nb = DEC_BATCH
    fixed = lambda i: (0, 0)
    st = pl.BlockSpec((nb, n_state), fixed)
    vec = pl.BlockSpec((1, n_state), fixed)
    y, sr, si = _call(
        _s5_decode_body, grid=(1,),
        in_specs=[pl.BlockSpec((nb, S5_WIDTH), fixed), st, st, vec, vec,
                  pl.BlockSpec((S5_WIDTH, 2 * n_state), fixed), pl.BlockSpec((2 * n_state, S5_WIDTH), fixed)],
        out_specs=[pl.BlockSpec((nb, S5_WIDTH), fixed), st, st],
        out_shape=[jax.ShapeDtypeStruct((nb, S5_WIDTH), F32), jax.ShapeDtypeStruct((nb, n_state), F32),
                   jax.ShapeDtypeStruct((nb, n_state), F32)],
        name="s5_decode")(mid, state[..., 0].reshape(nb, n_state), state[..., 1].reshape(nb, n_state),
                          abar_r.reshape(1, n_state), abar_i.reshape(1, n_state), bb, cc)
    new_state = jnp.stack([sr.reshape(nb, S5_GROUPS, S5_STATE), si.reshape(nb, S5_GROUPS, S5_STATE)], axis=-1)
    return y, new_state


def _lru_gates(xc, wa_ref, ba_ref, wi_ref, bi_ref, c8_ref):
    xb = xc.astype(BF16)
    r = _sigmoid(_dot(xb, wa_ref[...]) + ba_ref[...])
    ig = _sigmoid(_dot(xb, wi_ref[...]) + bi_ref[...])
    log_a = c8_ref[...] * r
    a = jnp.exp(log_a)
    b = jnp.sqrt(-jnp.tanh(log_a) * (a * a + 1.0)) * (ig * xc)
    return a, b


def _lru_pre_body(x_ref, p_ref, cw_ref, cb_ref, wa_ref, ba_ref, wi_ref, bi_ref, c8_ref,
                  a_ref, b_ref, ext_ref, *, tiles_per_seq):
    tm = x_ref.shape[0]
    first = pl.program_id(0) % tiles_per_seq == 0
    ext_ref[0:8, :] = jnp.where(first, 0.0, p_ref[...])
    ext_ref[8:, :] = x_ref[...]
    xc = cb_ref[...] + cw_ref[3:4, :] * x_ref[...]
    for k in (1, 2, 3):
        xc = xc + cw_ref[3 - k:4 - k, :] * ext_ref[8 - k:8 - k + tm, :]
    a, b = _lru_gates(xc, wa_ref, ba_ref, wi_ref, bi_ref, c8_ref)
    a_ref[...] = a
    b_ref[...] = b


def _lru_scan_body(a_ref, b_ref, g_ref, y_ref, hl_ref, h_ref, p_ref):
    t_len = a_ref.shape[0]
    seg_len = t_len // LRU_SEG
    n_q = LRU_SEG // 8

    def step(t, carry):
        out = []
        for qi in range(n_q):
            h, p = carry[2 * qi], carry[2 * qi + 1]
            rows = pl.ds(qi * 8 * seg_len + t, 8, stride=seg_len)
            at = a_ref[rows, :]
            h = at * h + b_ref[rows, :]
            p = at * p
            h_ref[rows, :] = h
            p_ref[rows, :] = p
            out += [h, p]
        return tuple(out)

    init = (jnp.zeros((8, LANES), F32), jnp.ones((8, LANES), F32)) * n_q
    lax.fori_loop(0, seg_len, step, init)
    carry = jnp.zeros((1, LANES), F32)
    for s in range(LRU_SEG):
        rows = slice(s * seg_len, (s + 1) * seg_len)
        h = h_ref[rows, :] + p_ref[rows, :] * carry
        y_ref[rows, :] = (h * _gelu(g_ref[rows, :])).astype(y_ref.dtype)
        carry = h[seg_len - 1:seg_len, :]
    hl_ref[0] = carry


def lru_weights(conv_w, conv_b, w_a, b_a, w_i, b_i, lam):
    eye = jnp.eye(LRU_HEADS, dtype=F32)
    bd = lambda w: jnp.einsum("hij,hk->hikj", w, eye).reshape(LRU_WIDTH, LRU_WIDTH).astype(BF16)
    c8 = (-LRU_C) * jax.nn.softplus(-lam)
    return (conv_w, conv_b.reshape(1, -1), bd(w_a), b_a.reshape(1, -1), bd(w_i), b_i.reshape(1, -1),
            c8.reshape(1, -1))


def lru_prompt(mid, lw, tm):
    conv_w, conv_b, wa, ba, wi, bi, c8 = lw
    fixed = lambda i: (0, 0)
    vec = pl.BlockSpec((1, LRU_WIDTH), fixed)
    mat = pl.BlockSpec((LRU_WIDTH, LRU_WIDTH), fixed)
    a, b = _call(
        functools.partial(_lru_pre_body, tiles_per_seq=SEQ // tm), grid=(M_PROMPT // tm,),
        in_specs=[pl.BlockSpec((tm, LRU_WIDTH), lambda i: (i, 1)),
                  pl.BlockSpec((8, LRU_WIDTH), lambda i: (jnp.maximum(i * (tm // 8) - 1, 0), 1)),
                  pl.BlockSpec((4, LRU_WIDTH), fixed), vec, mat, vec, mat, vec, vec],
        out_specs=[pl.BlockSpec((tm, LRU_WIDTH), lambda i: (i, 0))] * 2,
        out_shape=[jax.ShapeDtypeStruct((M_PROMPT, LRU_WIDTH), F32)] * 2,
        scratch_shapes=[pltpu.VMEM((tm + 8, LRU_WIDTH), F32)],
        name="lru_pre")(mid, mid, conv_w, conv_b, wa, ba, wi, bi, c8)
    n_lane = LRU_WIDTH // LANES
    g_col0 = (S5_WIDTH + LRU_WIDTH) // LANES
    blk = lambda off: pl.BlockSpec((SEQ, LANES), lambda bi_, c: (bi_, off + c))
    y, h_last = _call(
        _lru_scan_body, grid=(BATCH, n_lane),
        in_specs=[blk(0), blk(0), blk(g_col0)],
        out_specs=[blk(0), pl.BlockSpec((1, 1, LANES), lambda bi_, c: (bi_, 0, c))],
        out_shape=[jax.ShapeDtypeStruct((M_PROMPT, LRU_WIDTH), BF16),
                   jax.ShapeDtypeStruct((BATCH, 1, LRU_WIDTH), F32)],
        scratch_shapes=[pltpu.VMEM((SEQ, LANES), F32), pltpu.VMEM((SEQ, LANES), F32)],
        name="lru_scan")(a, b, mid)
    return y, h_last.reshape(BATCH, LRU_WIDTH)


def _lru_decode_body(x_ref, g_ref, c0_ref, c1_ref, c2_ref, h0_ref, cw_ref, cb_ref, wa_ref, ba_ref,
                     wi_ref, bi_ref, c8_ref, y_ref, h_ref):
    xc = (cb_ref[...] + cw_ref[0:1, :] * c0_ref[...] + cw_ref[1:2, :] * c1_ref[...]
          + cw_ref[2:3, :] * c2_ref[...] + cw_ref[3:4, :] * x_ref[...])
    a, b = _lru_gates(xc, wa_ref, ba_ref, wi_ref, bi_ref, c8_ref)
    h = a * h0_ref[...] + b
    h_ref[...] = h
    y_ref[...] = (h * _gelu(g_ref[...])).astype(y_ref.dtype)


def lru_decode(mid, conv_buf, h0, lw):
    conv_w, conv_b, wa, ba, wi, bi, c8 = lw
    nb = DEC_BATCH
    fixed = lambda i: (0, 0)
    vec = pl.BlockSpec((1, LRU_WIDTH), fixed)
    mat = pl.BlockSpec((LRU_WIDTH, LRU_WIDTH), fixed)
    st = pl.BlockSpec((nb, LRU_WIDTH), fixed)
    return _call(
        _lru_decode_body, grid=(1,),
        in_specs=[pl.BlockSpec((nb, LRU_WIDTH), lambda i: (0, 1)), pl.BlockSpec((nb, LRU_WIDTH), lambda i: (0, 2)),
                  st, st, st, st, pl.BlockSpec((4, LRU_WIDTH), fixed), vec, mat, vec, mat, vec, vec],
        out_specs=[st, st],
        out_shape=[jax.ShapeDtypeStruct((nb, LRU_WIDTH), BF16), jax.ShapeDtypeStruct((nb, LRU_WIDTH), F32)],
        name="lru_decode")(mid, mid, conv_buf[:, 0], conv_buf[:, 1], conv_buf[:, 2], h0,
                           conv_w, conv_b, wa, ba, wi, bi, c8)


TM = 1024
TM_SMALL = 512
TM_EXPERT = 512
M_ALL = M_PROMPT + DEC_BATCH
N_EXPERT_TILES = (2 * M_ALL) // TM_EXPERT + N_EXPERTS
DEC_ATT_BS = (16, 8, 2)


def _deinterleave(x, dil):
    return (x.reshape(BATCH, SEQ // dil, dil, GROUP_W).transpose(0, 2, 1, 3)
            .reshape(BATCH * dil, SEQ // dil, GROUP_W))


def _interleave(x, dil):
    return (x.reshape(BATCH, dil, SEQ // dil, GROUP_W).transpose(0, 2, 1, 3)
            .reshape(BATCH * SEQ, GROUP_W))


def _token_mixers(layer, xp, xs, p, caches_t, win_prev, state_s5, state_conv, state_lru):
    n_p = rmsnorm(xp, p["norm_mix"][layer], TM_SMALL, BF16)
    n_s = rmsnorm(xs, p["norm_mix"][layer], DEC_BATCH, BF16)
    w_in = p["w_in"][layer]
    proj = functools.partial(matmul_ws, [n_p], [n_s], w_of_x=[0], tm=TM)
    qkv_p, qkv_s = proj([(w_in[:, :OFF_MID], ())], tn=768, out_dtype=F32, epilogue=_ep_plain, name="proj_qkv")
    mid_p, mid_s = proj([(w_in[:, OFF_MID:OFF_GATE], ())], tn=640, out_dtype=F32, epilogue=_ep_plain,
                        name="proj_mid")
    gates_p, gates_s = proj([(w_in[:, OFF_GATE:], ())], tn=1024, out_dtype=BF16, out_dtype_s=F32,
                            epilogue=_ep_sigmoid, name="proj_gates")

    qn_p, kn_p = qk_norm(qkv_p, p["q_gain"][layer], p["k_gain"][layer], TM_SMALL, BF16)
    qn_s, kn_s = qk_norm(qkv_s, p["q_gain"][layer], p["k_gain"][layer], DEC_BATCH, F32)
    outs, lses, outs_s, lses_s = [], [], [], []
    new_p, win = [], []
    for g, (window, dil) in enumerate(ATT_GROUPS):
        cols = slice(GROUP_W * g, GROUP_W * (g + 1))
        v_cols = slice(2 * ATT_WIDTH + GROUP_W * g, 2 * ATT_WIDTH + GROUP_W * (g + 1))
        kg, vg = kn_p[:, cols], qkv_p[:, v_cols]
        o, lse = prompt_attention(_deinterleave(qn_p[:, cols], dil), _deinterleave(kg, dil),
                                  _deinterleave(vg, dil))
        outs.append(_interleave(o, dil))
        lses.append(_interleave(lse, dil))
        keep = min(window, SEQ)
        new_p.append(kg.reshape(BATCH, SEQ, HEADS, HEAD_DIM)[:, SEQ - keep:])
        new_p.append(vg.reshape(BATCH, SEQ, HEADS, HEAD_DIM)[:, SEQ - keep:])
        cols_s = jnp.concatenate([qn_s[:, cols], kn_s[:, cols], qkv_s[:, v_cols]], axis=1)
        cols_s = jnp.broadcast_to(cols_s[:, :, None], (DEC_BATCH, 3 * GROUP_W, LANES))
        prev_k, prev_v = (None, None) if win_prev is None else (win_prev[2 * g], win_prev[2 * g + 1])
        o_s, lse_s, wk, wv = decode_attention(layer, g, cols_s, caches_t[2 * g], caches_t[2 * g + 1],
                                              prev_k, prev_v, DEC_ATT_BS[g])
        outs_s.append(o_s.reshape(DEC_BATCH, GROUP_W))
        lses_s.append(lse_s.reshape(DEC_BATCH, GROUP_W))
        win += [wk, wv]
    att_p = mix_groups(outs, lses, TM_SMALL, BF16)
    att_s = mix_groups(outs_s, lses_s, DEC_BATCH, BF16)

    s5p = (p["s5_lam_re"][layer], p["s5_lam_im"][layer], p["s5_log_dt"][layer], p["s5_b_re"][layer],
           p["s5_b_im"][layer], p["s5_c_re"][layer], p["s5_c_im"][layer])
    n_chunk = SEQ // S5_CHUNK
    u_p = mid_p[:, :S5_WIDTH].reshape(BATCH * n_chunk, S5_CHUNK, S5_GROUPS, S5_CH)
    u_p = u_p.transpose(2, 0, 1, 3).reshape(S5_GROUPS, BATCH * n_chunk, S5_CHUNK * S5_CH)
    y_p, h_p = s5_scan(u_p, jnp.zeros((BATCH, S5_GROUPS, 2 * S5_STATE), F32),
                       s5_weights(*s5p, chunk=S5_CHUNK, pad=0), bb=1)
    y_p = (y_p.reshape(S5_GROUPS, BATCH * n_chunk, S5_CHUNK, S5_CH).transpose(1, 2, 0, 3)
           .reshape(M_PROMPT, S5_WIDTH))
    y_s, h_s = s5_decode(mid_s, state_s5[layer], *s5p)
    glu = (p["s5_d"][layer], p["s5_w_glu"][layer], p["s5_b_glu"][layer])
    s5_p = s5_post(y_p, mid_p, *glu, TM_SMALL)
    s5_s = s5_post(y_s, mid_s, *glu, DEC_BATCH)
    new_p.append(jnp.stack([h_p[..., :S5_STATE], h_p[..., S5_STATE:]], axis=-1))
    new_s = [h_s]

    lw = lru_weights(p["lru_conv_w"][layer], p["lru_conv_b"][layer], p["lru_w_a"][layer], p["lru_b_a"][layer],
                     p["lru_w_i"][layer], p["lru_b_i"][layer], p["lru_lam"][layer])
    lru_p, h_lp = lru_prompt(mid_p, lw, 512)
    lru_s, h_ls = lru_decode(mid_s, state_conv[layer], state_lru[layer], lw)
    x_lru = slice(S5_WIDTH, S5_WIDTH + LRU_WIDTH)
    new_p.append(mid_p[:, x_lru].reshape(BATCH, SEQ, LRU_WIDTH)[:, SEQ - 3:])
    new_p.append(h_lp)
    new_s.append(jnp.concatenate([state_conv[layer, :, 1:], mid_s[:, None, x_lru]], axis=1))
    new_s.append(h_ls)

    tn = 512
    gate_tiles = D_MODEL // tn
    merged_p, merged_s = matmul_ws(
        [att_p, s5_p, lru_p], [att_s, s5_s, lru_s],
        [(p["w_br_attn"], (layer,)), (p["w_br_s5"], (layer,)), (p["w_br_lru"], (layer,))],
        w_of_x=[0, 1, 2], tm=TM, tn=tn, out_dtype=BF16, epilogue=_ep_merge,
        tile_extras=[(gates_p, gates_s, 0), (gates_p, gates_s, gate_tiles), (gates_p, gates_s, 2 * gate_tiles)],
        name="merge")
    xp, xs = matmul_ws([merged_p], [merged_s], [(p["w_o"], (layer,))], w_of_x=[0], tm=TM, tn=512,
                       out_dtype=F32, epilogue=_ep_residual, tile_extras=[(xp, xs, 0)], name="out_proj")
    return xp, xs, new_p, new_s, win


def _dense_ffn(xp, xs, p, layer, j):
    n_p = rmsnorm(xp, p["norm_ffn"][layer], TM_SMALL, BF16)
    n_s = rmsnorm(xs, p["norm_ffn"][layer], DEC_BATCH, BF16)
    h_p, h_s = matmul_ws([n_p], [n_s], [(p["ffn_w_gate"], (j,)), (p["ffn_w_up"], (j,))], w_of_x=[0, 0],
                         tm=TM, tn=512, out_dtype=BF16, epilogue=_ep_swiglu, name="ffn_up")
    return matmul_ws([h_p], [h_s], [(p["ffn_w_down"], (j,))], w_of_x=[0], tm=512, tn=256, out_dtype=F32,
                     epilogue=_ep_residual, tile_extras=[(xp, xs, 0)], name="ffn_down")


def _moe_ffn(xp, xs, p, layer, j):
    router = (p["norm_ffn"][layer], p["moe_router_w"][j], p["moe_router_b"][j])
    n_p, comb_p, idx_p = rmsnorm_router(xp, *router, TM_SMALL)
    n_s, comb_s, idx_s = rmsnorm_router(xs, *router, DEC_BATCH)
    n = jnp.concatenate([n_p, n_s], axis=0)
    comb = jnp.concatenate([comb_p, comb_s], axis=0)
    idx = jnp.concatenate([idx_p, idx_s], axis=0)[:, :2]
    m = M_ALL
    top_e = idx.reshape(-1)
    top_w = jnp.take_along_axis(comb, idx, axis=1).reshape(-1)
    counts = jnp.sum(top_e[:, None] == jnp.arange(N_EXPERTS)[None, :], axis=0).astype(jnp.int32)
    padded = ((counts + TM_EXPERT - 1) // TM_EXPERT) * TM_EXPERT
    pad_end = jnp.cumsum(padded)
    pad_start = pad_end - padded
    start = jnp.cumsum(counts) - counts
    order = jnp.argsort(top_e, stable=True)
    rank = jnp.zeros((2 * m,), jnp.int32).at[order].set(jnp.arange(2 * m, dtype=jnp.int32))
    dest = pad_start[top_e] + rank - start[top_e]
    n_rows = N_EXPERT_TILES * TM_EXPERT
    src_token = jnp.zeros((n_rows,), jnp.int32).at[dest].set(jnp.arange(2 * m, dtype=jnp.int32) // 2)
    row_w = jnp.zeros((n_rows,), F32).at[dest].set(top_w)
    tile_start = jnp.arange(N_EXPERT_TILES, dtype=jnp.int32) * TM_EXPERT
    tile_expert = jnp.minimum(jnp.sum(tile_start[:, None] >= pad_end[None, :], axis=1), N_EXPERTS - 1)
    tile_expert = tile_expert.astype(jnp.int32)
    n_used = (pad_end[-1] // TM_EXPERT).reshape(1).astype(jnp.int32)
    x_sorted = jnp.take(n, src_token, axis=0)
    gmm = functools.partial(grouped_matmul_ws, tile_expert, n_used, lead=(j,), tm=TM_EXPERT)
    gate = gmm(x_sorted, p["moe_w_gate"], tn=1408, out_dtype=BF16, epilogue=_ep_plain, name="moe_gate")
    h = gmm(x_sorted, p["moe_w_up"], tn=1408, out_dtype=BF16, epilogue=_ep_silu_times, tile_extras=[gate],
            name="moe_up")
    ys = gmm(h, p["moe_w_down"], tn=1024, out_dtype=F32, epilogue=_ep_scale_rows,
             col_extras=[row_w.reshape(-1, 1)], name="moe_down")
    dest2 = dest.reshape(m, 2)
    y = jnp.take(ys, dest2[:, 0], axis=0) + jnp.take(ys, dest2[:, 1], axis=0)
    return xp + y[:M_PROMPT], xs + y[M_PROMPT:]


def kernel(x_prompt, x_sample, cache_k_w128, cache_v_w128, cache_k_w512, cache_v_w512, cache_k_w2048, cache_v_w2048, state_s5, state_conv, state_lru, norm_mix, w_in, q_gain, k_gain, s5_lam_re, s5_lam_im, s5_log_dt, s5_b_re, s5_b_im, s5_c_re, s5_c_im, s5_d, s5_w_glu, s5_b_glu, lru_conv_w, lru_conv_b, lru_w_a, lru_b_a, lru_w_i, lru_b_i, lru_lam, w_br_attn, w_br_s5, w_br_lru, w_o, norm_ffn, ffn_w_gate, ffn_w_up, ffn_w_down, moe_router_w, moe_router_b, moe_w_gate, moe_w_up, moe_w_down):
    p = dict(norm_mix=norm_mix, w_in=w_in, q_gain=q_gain, k_gain=k_gain, s5_lam_re=s5_lam_re,
             s5_lam_im=s5_lam_im, s5_log_dt=s5_log_dt, s5_b_re=s5_b_re, s5_b_im=s5_b_im, s5_c_re=s5_c_re,
             s5_c_im=s5_c_im, s5_d=s5_d, s5_w_glu=s5_w_glu, s5_b_glu=s5_b_glu, lru_conv_w=lru_conv_w,
             lru_conv_b=lru_conv_b, lru_w_a=lru_w_a, lru_b_a=lru_b_a, lru_w_i=lru_w_i, lru_b_i=lru_b_i,
             lru_lam=lru_lam, w_br_attn=w_br_attn, w_br_s5=w_br_s5, w_br_lru=w_br_lru, w_o=w_o,
             norm_ffn=norm_ffn, ffn_w_gate=ffn_w_gate, ffn_w_up=ffn_w_up, ffn_w_down=ffn_w_down,
             moe_router_w=moe_router_w, moe_router_b=moe_router_b, moe_w_gate=moe_w_gate,
             moe_w_up=moe_w_up, moe_w_down=moe_w_down)
    caches_t = [jnp.transpose(c, (0, 1, 3, 4, 2)) for c in
                (cache_k_w128, cache_v_w128, cache_k_w512, cache_v_w512, cache_k_w2048, cache_v_w2048)]
    xp = x_prompt.reshape(M_PROMPT, D_MODEL)
    xs = x_sample.reshape(DEC_BATCH, D_MODEL)
    states_p = [[] for _ in range(9)]
    states_s = [[] for _ in range(3)]
    win = None
    for layer in range(DEPTH):
        xp, xs, new_p, new_s, win = _token_mixers(layer, xp, xs, p, caches_t, win, state_s5, state_conv,
                                                  state_lru)
        if layer % 2 == 0:
            xp, xs = _dense_ffn(xp, xs, p, layer, layer // 2)
        else:
            xp, xs = _moe_ffn(xp, xs, p, layer, layer // 2)
        for i in range(9):
            states_p[i].append(new_p[i])
        for i in range(3):
            states_s[i].append(new_s[i])
    y_prompt = xp.reshape(BATCH, SEQ, D_MODEL)
    y_sample = xs.reshape(DEC_BATCH, 1, D_MODEL)
    windows = [jnp.transpose(w, (0, 1, 4, 2, 3)) for w in win]
    return (y_prompt, y_sample, *[jnp.stack(a) for a in states_p], *windows,
            *[jnp.stack(a) for a in states_s])
```

```python
import functools
import math

import jax
import jax.numpy as jnp
from jax import lax
from jax.experimental import pallas as pl
from jax.experimental.pallas import tpu as pltpu

F32 = jnp.float32
BF16 = jnp.bfloat16

D_MODEL = 2048
BATCH = 2
SEQ = 4096
DEPTH = 2
DEC_BATCH = 128
HEAD_DIM = 64
HEADS = 4
GROUP_W = HEADS * HEAD_DIM
ATT_GROUPS = ((128, 1), (512, 4), (2048, 16))
N_GROUPS = 3
ATT_WIDTH = N_GROUPS * GROUP_W
N_KEYS = 128
S5_WIDTH = 640
S5_GROUPS = 40
S5_CH = 16
S5_STATE = 64
LRU_WIDTH = 640
LRU_HEADS = 8
LRU_C = 8.0
OFF_MID = 3 * ATT_WIDTH
MID_WIDTH = S5_WIDTH + 2 * LRU_WIDTH
OFF_GATE = OFF_MID + MID_WIDTH
N_EXPERTS = 8
D_FF_EXPERT = 2816
EPS = 1e-6
LANES = 128

M_PROMPT = BATCH * SEQ

S5_CHUNK = 16
LRU_SEG = 32
NEG_BIG = -1e30

VMEM_LIMIT_BYTES = 56 * 1024 * 1024


def _call(body, *, grid, in_specs, out_specs, out_shape, name, scratch_shapes=(), num_scalar_prefetch=0,
          input_output_aliases=None):
    params = pltpu.CompilerParams(dimension_semantics=("arbitrary",) * len(grid),
                                  vmem_limit_bytes=VMEM_LIMIT_BYTES)
    kwargs = {}
    if input_output_aliases:
        kwargs["input_output_aliases"] = input_output_aliases
    if num_scalar_prefetch:
        grid_spec = pltpu.PrefetchScalarGridSpec(
            num_scalar_prefetch=num_scalar_prefetch, grid=grid, in_specs=in_specs,
            out_specs=out_specs, scratch_shapes=scratch_shapes)
        return pl.pallas_call(body, grid_spec=grid_spec, out_shape=out_shape,
                              compiler_params=params, name=name, **kwargs)
    return pl.pallas_call(body, grid=grid, in_specs=in_specs, out_specs=out_specs,
                          out_shape=out_shape, scratch_shapes=scratch_shapes,
                          compiler_params=params, name=name, **kwargs)


def _dot(a, b):
    return jnp.dot(a, b, preferred_element_type=F32)


def _split2(x):
    hi = x.astype(BF16)
    lo = (x - hi.astype(F32)).astype(BF16)
    return hi, lo


def _dot_sel(x, sel, parts=2):
    acc = None
    r = x
    for _ in range(parts):
        piece = r.astype(BF16)
        r = r - piece.astype(F32)
        d = _dot(piece, sel)
        acc = d if acc is None else acc + d
    return acc


def _dot3(a, b):
    ah, al = _split2(a)
    bh, bl = _split2(b)
    return _dot(ah, bh) + (_dot(al, bh) + _dot(ah, bl))


def _sigmoid(x):
    return 1.0 / (1.0 + jnp.exp(-x))


def _gelu(x):
    c = math.sqrt(2.0 / math.pi)
    return 0.5 * x * (1.0 + jnp.tanh(c * (x + 0.044715 * (x * x * x))))


def _rmsnorm_body(x_ref, g_ref, o_ref):
    x = x_ref[...]
    y = x * lax.rsqrt(jnp.mean(x * x, axis=-1, keepdims=True) + EPS)
    o_ref[...] = (y * g_ref[...]).astype(o_ref.dtype)


def _rmsnorm_router_body(x_ref, g_ref, rw_ref, rb_ref, o_ref, comb_ref, idx_ref):
    x = x_ref[...]
    y = x * lax.rsqrt(jnp.mean(x * x, axis=-1, keepdims=True) + EPS)
    xn = y * g_ref[...]
    o_ref[...] = xn.astype(o_ref.dtype)
    logits = _dot(xn.astype(BF16), rw_ref[...].astype(BF16)) + rb_ref[...]
    lane = lax.broadcasted_iota(jnp.int32, logits.shape, 1).astype(F32)
    m1 = jnp.max(logits, axis=1, keepdims=True)
    i1 = jnp.min(jnp.where(logits == m1, lane, float(LANES)), axis=1, keepdims=True)
    rest = jnp.where(lane == i1, -jnp.inf, logits)
    m2 = jnp.max(rest, axis=1, keepdims=True)
    i2 = jnp.min(jnp.where(rest == m2, lane, float(LANES)), axis=1, keepdims=True)
    e = jnp.exp(m2 - m1)
    w1 = 1.0 / (1.0 + e)
    w2 = e / (1.0 + e)
    comb_ref[...] = jnp.where(lane == i1, w1, 0.0) + jnp.where(lane == i2, w2, 0.0)
    idx_ref[...] = jnp.where(lane == 0.0, i1, jnp.where(lane == 1.0, i2, 0.0)).astype(jnp.int32)


def rmsnorm(x, gain, tm, out_dtype):
    m, d = x.shape
    return _call(
        _rmsnorm_body, grid=(m // tm,),
        in_specs=[pl.BlockSpec((tm, d), lambda i: (i, 0)), pl.BlockSpec((1, d), lambda i: (0, 0))],
        out_specs=pl.BlockSpec((tm, d), lambda i: (i, 0)),
        out_shape=jax.ShapeDtypeStruct((m, d), out_dtype), name="rmsnorm")(x, gain.reshape(1, d))


def rmsnorm_router(x, gain, router_w, router_b, tm):
    m, d = x.shape
    ne = router_w.shape[1]
    rw = jnp.zeros((d, LANES), F32).at[:, :ne].set(router_w)
    rb = jnp.full((1, LANES), NEG_BIG, F32).at[0, :ne].set(router_b)
    row = lambda i: (i, 0)
    fixed = lambda i: (0, 0)
    return _call(
        _rmsnorm_router_body, grid=(m // tm,),
        in_specs=[pl.BlockSpec((tm, d), row), pl.BlockSpec((1, d), fixed),
                  pl.BlockSpec((d, LANES), fixed), pl.BlockSpec((1, LANES), fixed)],
        out_specs=[pl.BlockSpec((tm, d), row), pl.BlockSpec((tm, LANES), row), pl.BlockSpec((tm, LANES), row)],
        out_shape=[jax.ShapeDtypeStruct((m, d), F32), jax.ShapeDtypeStruct((m, LANES), F32),
                   jax.ShapeDtypeStruct((m, LANES), jnp.int32)],
        name="rmsnorm_router")(x, gain.reshape(1, d), rw, rb)


def _mm_body(*refs, n_x, w_of_x, n_te, n_re, epilogue, n_ptiles):
    n_w = len(w_of_x)
    pos = 0
    xp_refs = refs[pos:pos + n_x]; pos += n_x
    xs_refs = refs[pos:pos + n_x]; pos += n_x
    w_refs = refs[pos:pos + n_w]; pos += n_w
    tep_refs = refs[pos:pos + n_te]; pos += n_te
    tes_refs = refs[pos:pos + n_te]; pos += n_te
    re_refs = refs[pos:pos + n_re]; pos += n_re
    op_ref, os_ref = refs[pos], refs[pos + 1]; pos += 2
    wb_refs = refs[pos:pos + n_w]
    i = pl.program_id(1)

    @pl.when(i == 0)
    def _():
        for w_ref, wb_ref in zip(w_refs, wb_refs):
            wb_ref[...] = w_ref[...].astype(BF16)

    def rows(x_refs, te_refs, o_ref):
        accs = [_dot(x_refs[xi][...].astype(BF16), wb_ref[...]) for xi, wb_ref in zip(w_of_x, wb_refs)]
        extras = [e[...] for e in te_refs] + [e[...] for e in re_refs]
        o_ref[...] = epilogue(accs, extras).astype(o_ref.dtype)

    @pl.when(i < n_ptiles)
    def _():
        rows(xp_refs, tep_refs, op_ref)

    @pl.when(i == n_ptiles)
    def _():
        rows(xs_refs, tes_refs, os_ref)


def matmul_ws(xps, xss, ws, *, w_of_x, tm, tn, out_dtype, epilogue, tile_extras=(), row_extras=(), name,
              out_dtype_s=None):
    mp = xps[0].shape[0]
    ms = xss[0].shape[0]
    n = ws[0][0].shape[-1]
    n_ptiles = mp // tm
    last = n_ptiles - 1
    grid = (n // tn, n_ptiles + 1)
    in_specs = [pl.BlockSpec((tm, x.shape[1]), lambda j, i: (jnp.minimum(i, last), 0)) for x in xps]
    in_specs += [pl.BlockSpec((ms, x.shape[1]), lambda j, i: (0, 0)) for x in xss]
    scratch = []
    for w, lead in ws:
        k = w.shape[-2]
        block = (None,) * len(lead) + (k, tn)
        in_specs.append(pl.BlockSpec(block, lambda j, i, lead=lead: lead + (0, j)))
        scratch.append(pltpu.VMEM((k, tn), BF16))
    for _, _, off in tile_extras:
        in_specs.append(pl.BlockSpec((tm, tn), lambda j, i, off=off: (jnp.minimum(i, last), off + j)))
    for _, _, off in tile_extras:
        in_specs.append(pl.BlockSpec((ms, tn), lambda j, i, off=off: (0, off + j)))
    for _ in row_extras:
        in_specs.append(pl.BlockSpec((1, tn), lambda j, i: (0, j)))
    body = functools.partial(_mm_body, n_x=len(xps), w_of_x=tuple(w_of_x), n_te=len(tile_extras),
                             n_re=len(row_extras), epilogue=epilogue, n_ptiles=n_ptiles)
    return _call(
        body, grid=grid, in_specs=in_specs,
        out_specs=[pl.BlockSpec((tm, tn), lambda j, i: (jnp.minimum(i, last), j)),
                   pl.BlockSpec((ms, tn), lambda j, i: (0, j))],
        out_shape=[jax.ShapeDtypeStruct((mp, n), out_dtype),
                   jax.ShapeDtypeStruct((ms, n), out_dtype_s or out_dtype)],
        scratch_shapes=scratch, name=name,
    )(*xps, *xss, *[w for w, _ in ws], *[a for a, _, _ in tile_extras], *[a for _, a, _ in tile_extras],
      *row_extras)


def _ep_plain(accs, extras):
    return accs[0]


def _ep_sigmoid(accs, extras):
    return _sigmoid(accs[0])


def _ep_residual(accs, extras):
    return extras[0] + accs[0]


def _ep_swiglu(accs, extras):
    g, u = accs
    return (g * _sigmoid(g)) * u


def _ep_merge(accs, extras):
    return (extras[0].astype(F32) * accs[0] + extras[1].astype(F32) * accs[1]
            + extras[2].astype(F32) * accs[2])


def _gmm_body(te_ref, nu_ref, *refs, n_te, n_ce, epilogue):
    x_ref, w_ref = refs[0], refs[1]
    extra_refs = refs[2:2 + n_te + n_ce]
    o_ref = refs[2 + n_te + n_ce]
    wb_ref = refs[3 + n_te + n_ce]
    i = pl.program_id(1)
    new_expert = jnp.logical_or(i == 0, te_ref[i] != te_ref[jnp.maximum(i - 1, 0)])

    @pl.when(new_expert)
    def _():
        wb_ref[...] = w_ref[...].astype(BF16)

    @pl.when(i < nu_ref[0])
    def _():
        acc = _dot(x_ref[...].astype(BF16), wb_ref[...])
        o_ref[...] = epilogue([acc], [e[...] for e in extra_refs]).astype(o_ref.dtype)

    @pl.when(i >= nu_ref[0])
    def _():
        o_ref[...] = jnp.zeros(o_ref.shape, o_ref.dtype)


def grouped_matmul_ws(tile_expert, n_used, x, w, lead, *, tm, tn, out_dtype, epilogue, tile_extras=(),
                      col_extras=(), name):
    m, k = x.shape
    n = w.shape[-1]
    grid = (n // tn, m // tm)
    block = (None,) * (len(lead) + 1) + (k, tn)
    in_specs = [pl.BlockSpec((tm, k), lambda j, i, te, nu: (i, 0)),
                pl.BlockSpec(block, lambda j, i, te, nu: lead + (te[i], 0, j))]
    in_specs += [pl.BlockSpec((tm, tn), lambda j, i, te, nu: (i, j)) for _ in tile_extras]
    in_specs += [pl.BlockSpec((tm, 1), lambda j, i, te, nu: (i, 0)) for _ in col_extras]
    body = functools.partial(_gmm_body, n_te=len(tile_extras), n_ce=len(col_extras), epilogue=epilogue)
    return _call(
        body, grid=grid, in_specs=in_specs,
        out_specs=pl.BlockSpec((tm, tn), lambda j, i, te, nu: (i, j)),
        out_shape=jax.ShapeDtypeStruct((m, n), out_dtype), scratch_shapes=[pltpu.VMEM((k, tn), BF16)],
        num_scalar_prefetch=2, name=name,
    )(tile_expert, n_used, x, w, *tile_extras, *col_extras)


def _ep_silu_times(accs, extras):
    g = extras[0].astype(F32)
    return (g * _sigmoid(g)) * accs[0]


def _ep_scale_rows(accs, extras):
    return extras[0] * accs[0]


def _qknorm_body(q_ref, k_ref, qg_ref, kg_ref, seg_ref, qo_ref, ko_ref):
    seg = seg_ref[...]
    for src, gain, dst, scale in ((q_ref, qg_ref, qo_ref, HEAD_DIM ** -0.5), (k_ref, kg_ref, ko_ref, 1.0)):
        for g in range(N_GROUPS):
            cols = slice(GROUP_W * g, GROUP_W * (g + 1))
            x = src[:, cols]
            ms = _dot_sel(x * x, seg, 3)
            y = x * lax.rsqrt(ms + EPS) * gain[:, cols]
            dst[:, cols] = (y * scale).astype(dst.dtype)


def qk_norm(qkv, q_gain, k_gain, tm, q_dtype):
    m = qkv.shape[0]
    head = jnp.arange(GROUP_W) // HEAD_DIM
    seg = jnp.where(head[:, None] == head[None, :], 1.0 / HEAD_DIM, 0.0).astype(BF16)
    qg = jnp.broadcast_to(q_gain[:, None, :], (N_GROUPS, HEADS, HEAD_DIM)).reshape(1, ATT_WIDTH)
    kg = jnp.broadcast_to(k_gain[:, None, :], (N_GROUPS, HEADS, HEAD_DIM)).reshape(1, ATT_WIDTH)
    fixed = lambda i: (0, 0)
    return _call(
        _qknorm_body, grid=(m // tm,),
        in_specs=[pl.BlockSpec((tm, ATT_WIDTH), lambda i: (i, 0)), pl.BlockSpec((tm, ATT_WIDTH), lambda i: (i, 1)),
                  pl.BlockSpec((1, ATT_WIDTH), fixed), pl.BlockSpec((1, ATT_WIDTH), fixed),
                  pl.BlockSpec((GROUP_W, GROUP_W), fixed)],
        out_specs=[pl.BlockSpec((tm, ATT_WIDTH), lambda i: (i, 0)), pl.BlockSpec((tm, ATT_WIDTH), lambda i: (i, 0))],
        out_shape=[jax.ShapeDtypeStruct((m, ATT_WIDTH), q_dtype), jax.ShapeDtypeStruct((m, ATT_WIDTH), F32)],
        name="qk_norm")(qkv, qkv, qg, kg, seg)


def _attn_body(*refs, dil):
    q_refs, kp_refs, kc_refs, vp_refs, vc_refs = (refs[2 * n:2 * n + 2] for n in range(5))
    o_ref, lse_ref = refs[10], refs[11]
    o_stage, lse_stage = refs[12:14], refs[14:16]
    a = pl.program_id(1)
    tq = N_KEYS
    r = lax.broadcasted_iota(jnp.int32, (tq, 2 * tq), 0)
    c = lax.broadcasted_iota(jnp.int32, (tq, 2 * tq), 1)
    valid = (c >= r) & (c <= r + N_KEYS) & ((a > 0) | (c >= tq))
    head = lax.broadcasted_iota(jnp.int32, (1, GROUP_W), 1) // HEAD_DIM

    def residue(res, carry):
        rows = pl.ds(res, tq, stride=dil) if dil > 1 else slice(None)
        load = lambda halves: jnp.concatenate([halves[0][rows, :], halves[1][rows, :]], axis=1)
        q = load(q_refs)
        k2 = jnp.concatenate([load(kp_refs), load(kc_refs)], axis=0).astype(BF16)
        v2 = jnp.concatenate([load(vp_refs), load(vc_refs)], axis=0).astype(BF16)
        o = jnp.zeros((tq, GROUP_W), F32)
        lse = jnp.zeros((tq, GROUP_W), F32)
        for h in range(HEADS):
            qh = jnp.where(head == h, q, 0.0).astype(BF16)
            s = lax.dot_general(qh, k2, (((1,), (1,)), ((), ())), preferred_element_type=F32)
            s = jnp.where(valid, s, NEG_BIG)
            m = jnp.max(s, axis=1, keepdims=True)
            p = jnp.exp(s - m)
            l = jnp.sum(p, axis=1, keepdims=True)
            oh = _dot(p.astype(BF16), v2)
            o = jnp.where(head == h, oh / l, o)
            lse = jnp.where(head == h, m + jnp.log(l), lse)
        for half in range(2):
            lanes = slice(LANES * half, LANES * (half + 1))
            o_stage[half][rows, :] = o[:, lanes]
            lse_stage[half][rows, :] = lse[:, lanes]
        return carry

    if dil == 1:
        residue(0, 0)
    else:
        lax.fori_loop(0, dil, residue, 0)
    for half in range(2):
        lanes = slice(LANES * half, LANES * (half + 1))
        o_ref[:, lanes] = o_stage[half][...]
        lse_ref[:, lanes] = lse_stage[half][...]


def prompt_attention(g, qn, kn, qkv):
    _, dil = ATT_GROUPS[g]
    span = N_KEYS * dil
    n_span = SEQ // span
    halves_per_group = GROUP_W // LANES
    cur = lambda col: (lambda b, a: (b * n_span + a, col))
    prev = lambda col: (lambda b, a: (b * n_span + jnp.maximum(a - 1, 0), col))
    blk = (span, LANES)
    qk_cols = [halves_per_group * g + half for half in range(halves_per_group)]
    v_cols = [halves_per_group * (2 * N_GROUPS + g) + half for half in range(halves_per_group)]
    in_specs = ([pl.BlockSpec(blk, cur(c)) for c in qk_cols] + [pl.BlockSpec(blk, prev(c)) for c in qk_cols]
                + [pl.BlockSpec(blk, cur(c)) for c in qk_cols] + [pl.BlockSpec(blk, prev(c)) for c in v_cols]
                + [pl.BlockSpec(blk, cur(c)) for c in v_cols])
    out = pl.BlockSpec((span, GROUP_W), lambda b, a: (b * n_span + a, 0))
    return _call(
        functools.partial(_attn_body, dil=dil), grid=(BATCH, n_span),
        in_specs=in_specs, out_specs=[out, out],
        out_shape=[jax.ShapeDtypeStruct((M_PROMPT, GROUP_W), F32)] * 2,
        scratch_shapes=[pltpu.VMEM((span, LANES), F32)] * 4,
        name="prompt_attention")(qn, qn, kn, kn, kn, kn, qkv, qkv, qkv, qkv)


def _dec_attn_body(*refs, dil, aliased):
    if aliased:
        qkv_ref, ck_ref, cv_ref, _, _, o_ref, lse_ref, nk_ref, nv_ref = refs
    else:
        qkv_ref, ck_ref, cv_ref, o_ref, lse_ref, nk_ref, nv_ref = refs
    bs = ck_ref.shape[1]
    n_chunk = ck_ref.shape[4] // LANES
    lane = lax.broadcasted_iota(jnp.int32, (1, LANES), 1)
    key_lane = (lane % dil) == 0
    not_last = lax.broadcasted_iota(jnp.int32, (HEAD_DIM, LANES), 1) < LANES - 1
    sample_row = lax.broadcasted_iota(jnp.int32, (qkv_ref.shape[1], LANES), 0)
    qkv_t = qkv_ref[...]
    for s in range(bs):
        pick = (sample_row == pl.program_id(0) * bs + s).astype(BF16)
        cols = _dot_sel(qkv_t, pick, 3)
        for h in range(HEADS):
            rows = slice(HEAD_DIM * h, HEAD_DIM * (h + 1))
            qb = cols[rows, :]
            kn = cols[GROUP_W + HEAD_DIM * h:GROUP_W + HEAD_DIM * (h + 1), :]
            vn = cols[2 * GROUP_W + HEAD_DIM * h:2 * GROUP_W + HEAD_DIM * (h + 1), :]
            chunk = lambda ref, c: ref[0, s, h, :, c * LANES:(c + 1) * LANES]
            scores = [jnp.where(key_lane, jnp.sum(chunk(ck_ref, c) * qb, axis=0, keepdims=True), NEG_BIG)
                      for c in range(n_chunk)]
            s_self = jnp.sum(kn * qb, axis=0, keepdims=True)
            m = s_self
            for sc in scores:
                m = jnp.maximum(m, jnp.max(sc, axis=1, keepdims=True))
            p_self = jnp.exp(s_self - m)
            l = p_self
            acc = None
            for c in range(n_chunk):
                p = jnp.exp(scores[c] - m)
                l = l + jnp.sum(p, axis=1, keepdims=True)
                pv = p * chunk(cv_ref, c)
                acc = pv if acc is None else acc + pv
            o = (jnp.sum(acc, axis=1, keepdims=True) + p_self[:, 0:1] * vn[:, 0:1]) / l[:, 0:1]
            o_ref[s, rows, :] = o
            lse_ref[s, rows, :] = jnp.broadcast_to(m[:, 0:1] + jnp.log(l[:, 0:1]), (HEAD_DIM, 1))
            for src, dst, new in ((ck_ref, nk_ref, kn), (cv_ref, nv_ref, vn)):
                rolled = [pltpu.roll(chunk(src, c), LANES - 1, axis=1) for c in range(n_chunk)] + [new]
                for c in range(n_chunk):
                    dst[0, s, h, :, c * LANES:(c + 1) * LANES] = jnp.where(not_last, rolled[c], rolled[c + 1])


def decode_attention(layer, g, qkv_cols, cache_k, cache_v, prev_k, prev_v, bs):
    window, dil = ATT_GROUPS[g]
    cblk = (1, bs, HEADS, HEAD_DIM, window)
    cmap = lambda i: (layer, i, 0, 0, 0)
    col = pl.BlockSpec((bs, GROUP_W, 1), lambda i: (i, 0, 0))
    in_specs = [pl.BlockSpec((3 * GROUP_W, DEC_BATCH), lambda i: (0, 0)),
                pl.BlockSpec(cblk, cmap), pl.BlockSpec(cblk, cmap)]
    args = [qkv_cols, cache_k, cache_v]
    aliases = None
    if prev_k is not None:
        in_specs += [pl.BlockSpec(memory_space=pl.ANY), pl.BlockSpec(memory_space=pl.ANY)]
        args += [prev_k, prev_v]
        aliases = {3: 2, 4: 3}
    return _call(
        functools.partial(_dec_attn_body, dil=dil, aliased=prev_k is not None), grid=(DEC_BATCH // bs,),
        in_specs=in_specs,
        out_specs=[col, col, pl.BlockSpec(cblk, cmap), pl.BlockSpec(cblk, cmap)],
        out_shape=[jax.ShapeDtypeStruct((DEC_BATCH, GROUP_W, 1), F32)] * 2
        + [jax.ShapeDtypeStruct(cache_k.shape, F32)] * 2,
        input_output_aliases=aliases, name="decode_attention")(*args)


def _mix_body(o0_ref, o1_ref, o2_ref, l0_ref, l1_ref, l2_ref, att_ref):
    l0, l1, l2 = l0_ref[...], l1_ref[...], l2_ref[...]
    m = jnp.maximum(jnp.maximum(l0, l1), l2)
    e0, e1, e2 = jnp.exp(l0 - m), jnp.exp(l1 - m), jnp.exp(l2 - m)
    num = e0 * o0_ref[...] + e1 * o1_ref[...] + e2 * o2_ref[...]
    att_ref[...] = (num / (e0 + e1 + e2)).astype(att_ref.dtype)


def mix_groups(outs, lses, tm, out_dtype):
    m = outs[0].shape[0]
    spec = pl.BlockSpec((tm, GROUP_W), lambda i: (i, 0))
    return _call(_mix_body, grid=(m // tm,), in_specs=[spec] * 6, out_specs=spec,
                 out_shape=jax.ShapeDtypeStruct((m, GROUP_W), out_dtype), name="mix_groups")(*outs, *lses)


def _s5_local_body(u_ref, wst_ref, s_ref):
    s_ref[0] = _dot3(u_ref[0], wst_ref[0])


def _s5_out_body(u_ref, hp_ref, toep_ref, wout_ref, y_ref):
    y_ref[0] = _dot3(u_ref[0], toep_ref[0]) + _dot3(hp_ref[0], wout_ref[0])


def _s5_carry_body(s_ref, h0_ref, a1_ref, a2_ref, hp_ref, hl_ref):
    bb, nc, ng, w = s_ref.shape
    a1 = jnp.broadcast_to(a1_ref[...][None], (bb, ng, w)).reshape(bb * ng, w)
    a2 = jnp.broadcast_to(a2_ref[...][None], (bb, ng, w)).reshape(bb * ng, w)

    def step(c, h):
        hp_ref[:, pl.ds(c, 1)] = h.reshape(bb, 1, ng, w)
        s = s_ref[:, pl.ds(c, 1)].reshape(bb * ng, w)
        return a1 * h + a2 * pltpu.roll(h, w // 2, axis=1) + s

    h = lax.fori_loop(0, nc, step, h0_ref[...].reshape(bb * ng, w))
    hl_ref[...] = h.reshape(bb, ng, w)


def s5_weights(lam_re, lam_im, log_dt, b_re, b_im, c_re, c_im, chunk, pad):
    hp = lax.Precision.HIGHEST
    dt = jnp.exp(log_dt)[:, None]
    tau = jnp.arange(chunk + 1, dtype=F32)[:, None, None]
    mag = jnp.exp(lam_re * dt * tau)
    pw_r = mag * jnp.cos(lam_im * dt * tau)
    pw_i = mag * jnp.sin(lam_im * dt * tau)
    abar_r, abar_i = pw_r[1], pw_i[1]
    den = lam_re * lam_re + lam_im * lam_im
    fr = ((abar_r - 1.0) * lam_re + abar_i * lam_im) / den
    fi = (abar_i * lam_re - (abar_r - 1.0) * lam_im) / den
    bbr = fr[..., None] * b_re - fi[..., None] * b_im
    bbi = fr[..., None] * b_im + fi[..., None] * b_re
    ab_r = pw_r[..., None] * bbr - pw_i[..., None] * bbi
    ab_i = pw_r[..., None] * bbi + pw_i[..., None] * bbr
    kern = (jnp.einsum("gop,tgpc->tgoc", c_re, ab_r[:chunk], precision=hp)
            - jnp.einsum("gop,tgpc->tgoc", c_im, ab_i[:chunk], precision=hp))
    s_idx = jnp.arange(chunk)[:, None]
    t_idx = jnp.arange(chunk)[None, :]
    lag = t_idx - s_idx
    k_st = kern[jnp.maximum(lag, 0)]
    k_st = jnp.where((lag >= 0)[:, :, None, None, None], k_st, 0.0)
    toep = k_st.transpose(2, 0, 4, 1, 3).reshape(S5_GROUPS, chunk * S5_CH, chunk * S5_CH)
    rev = chunk - 1 - jnp.arange(chunk)
    wst = jnp.concatenate([ab_r[rev], ab_i[rev]], axis=2)
    wst = wst.transpose(1, 0, 3, 2).reshape(S5_GROUPS, chunk * S5_CH, 2 * S5_STATE)
    ar, ai = pw_r[1:, :, None, :], pw_i[1:, :, None, :]
    co_r = c_re[None] * ar - c_im[None] * ai
    co_i = -c_re[None] * ai - c_im[None] * ar
    wout = jnp.concatenate([co_r, co_i], axis=3)
    wout = wout.transpose(1, 3, 0, 2).reshape(S5_GROUPS, 2 * S5_STATE, chunk * S5_CH)
    width = chunk * S5_CH
    if width < pad:
        extra = pad - width
        toep = jnp.pad(toep, ((0, 0), (0, extra), (0, extra)))
        wst = jnp.pad(wst, ((0, 0), (0, extra), (0, 0)))
        wout = jnp.pad(wout, ((0, 0), (0, 0), (0, extra)))
    a1 = jnp.concatenate([pw_r[chunk], pw_r[chunk]], axis=1)
    a2 = jnp.concatenate([-pw_i[chunk], pw_i[chunk]], axis=1)
    return toep, wst, wout, a1, a2


def s5_scan(u_g, h0, weights, bb):
    toep, wst, wout, a1, a2 = weights
    ng, rows, width = u_g.shape
    nseq = h0.shape[0]
    nc = rows // nseq
    w2 = 2 * S5_STATE
    gspec = lambda shape: pl.BlockSpec((1,) + shape, lambda g: (g, 0, 0))
    s_loc = _call(
        _s5_local_body, grid=(ng,), in_specs=[gspec((rows, width)), gspec((width, w2))],
        out_specs=gspec((rows, w2)), out_shape=jax.ShapeDtypeStruct((ng, rows, w2), F32),
        name="s5_local")(u_g, wst)
    s_seq = s_loc.reshape(ng, nseq, nc, w2).transpose(1, 2, 0, 3)
    seq_spec = pl.BlockSpec((bb, nc, ng, w2), lambda b: (b, 0, 0, 0))
    st_spec = pl.BlockSpec((bb, ng, w2), lambda b: (b, 0, 0))
    a_spec = pl.BlockSpec((ng, w2), lambda b: (0, 0))
    h_prev, h_last = _call(
        _s5_carry_body, grid=(nseq // bb,), in_specs=[seq_spec, st_spec, a_spec, a_spec],
        out_specs=[seq_spec, st_spec],
        out_shape=[jax.ShapeDtypeStruct((nseq, nc, ng, w2), F32), jax.ShapeDtypeStruct((nseq, ng, w2), F32)],
        name="s5_carry")(s_seq, h0, a1, a2)
    hp_g = h_prev.transpose(2, 0, 1, 3).reshape(ng, rows, w2)
    y_g = _call(
        _s5_out_body, grid=(ng,),
        in_specs=[gspec((rows, width)), gspec((rows, w2)), gspec((width, width)), gspec((w2, width))],
        out_specs=gspec((rows, width)), out_shape=jax.ShapeDtypeStruct((ng, rows, width), F32),
        name="s5_out")(u_g, hp_g, toep, wout)
    return y_g, h_last


def _s5_post_body(y_ref, u_ref, d_ref, w_ref, b_ref, o_ref):
    z = _gelu(y_ref[...] + d_ref[...] * u_ref[...])
    lin = _dot(z.astype(BF16), w_ref[...].astype(BF16))
    o_ref[...] = (z * _sigmoid(lin + b_ref[...])).astype(o_ref.dtype)


def s5_post(y, mid, d_skip, w_glu, b_glu, tm):
    m = y.shape[0]
    row = lambda i: (i, 0)
    fixed = lambda i: (0, 0)
    return _call(
        _s5_post_body, grid=(m // tm,),
        in_specs=[pl.BlockSpec((tm, S5_WIDTH), row), pl.BlockSpec((tm, S5_WIDTH), row),
                  pl.BlockSpec((1, S5_WIDTH), fixed), pl.BlockSpec((S5_WIDTH, S5_WIDTH), fixed),
                  pl.BlockSpec((1, S5_WIDTH), fixed)],
        out_specs=pl.BlockSpec((tm, S5_WIDTH), row),
        out_shape=jax.ShapeDtypeStruct((m, S5_WIDTH), BF16),
        name="s5_post")(y, mid, d_skip.reshape(1, -1), w_glu, b_glu.reshape(1, -1))


def _s5_decode_body(u_ref, hr_ref, hi_ref, ar_ref, ai_ref, bb_ref, cc_ref, y_ref, sr_ref, si_ref):
    n = hr_ref.shape[1]
    bu = _dot(u_ref[...].astype(BF16), bb_ref[...])
    ar, ai = ar_ref[...], ai_ref[...]
    hr, hi = hr_ref[...], hi_ref[...]
    sr = bu[:, :n] + (ar * hr - ai * hi)
    si = bu[:, n:] + (ar * hi + ai * hr)
    sr_ref[...] = sr
    si_ref[...] = si
    y_ref[...] = _dot(jnp.concatenate([sr, si], axis=1).astype(BF16), cc_ref[...])


def s5_decode(mid, state, lam_re, lam_im, log_dt, b_re, b_im, c_re, c_im):
    dt = jnp.exp(log_dt)[:, None]
    mag = jnp.exp(lam_re * dt)
    abar_r, abar_i = mag * jnp.cos(lam_im * dt), mag * jnp.sin(lam_im * dt)
    den = lam_re * lam_re + lam_im * lam_im
    fr = ((abar_r - 1.0) * lam_re + abar_i * lam_im) / den
    fi = (abar_i * lam_re - (abar_r - 1.0) * lam_im) / den
    bbr = fr[..., None] * b_re - fi[..., None] * b_im
    bbi = fr[..., None] * b_im + fi[..., None] * b_re
    eye = jnp.eye(S5_GROUPS, dtype=F32)
    n_state = S5_GROUPS * S5_STATE
    in_map = lambda w: jnp.einsum("gpc,gk->gckp", w, eye).reshape(S5_WIDTH, n_state)
    out_map = lambda w: jnp.einsum("gop,gk->gpko", w, eye).reshape(n_state, S5_WIDTH)
    bb = jnp.concatenate([in_map(bbr), in_map(bbi)], axis=1).astype(BF16)
    cc = jnp.concatenate([out_map(c_re), -out_map(c_im)], axis=0).astype(BF16)
    nb = DEC_BATCH
    fixed = lambda i: (0, 0)
    st = pl.BlockSpec((nb, n_state), fixed)
    vec = pl.BlockSpec((1, n_state), fixed)
    y, sr, si = _call(
        _s5_decode_body, grid=(1,),
        in_specs=[pl.BlockSpec((nb, S5_WIDTH), fixed), st, st, vec, vec,
                  pl.BlockSpec((S5_WIDTH, 2 * n_state), fixed), pl.BlockSpec((2 * n_state, S5_WIDTH), fixed)],
        out_specs=[pl.BlockSpec((nb, S5_WIDTH), fixed), st, st],
        out_shape=[jax.ShapeDtypeStruct((nb, S5_WIDTH), F32), jax.ShapeDtypeStruct((nb, n_state), F32),
                   jax.ShapeDtypeStruct((nb, n_state), F32)],
        name="s5_decode")(mid, state[..., 0].reshape(nb, n_state), state[..., 1].reshape(nb, n_state),
                          abar_r.reshape(1, n_state), abar_i.reshape(1, n_state), bb, cc)
    new_state = jnp.stack([sr.reshape(nb, S5_GROUPS, S5_STATE), si.reshape(nb, S5_GROUPS, S5_STATE)], axis=-1)
    return y, new_state


def _lru_gates(xc, wa_ref, ba_ref, wi_ref, bi_ref, c8_ref):
    xb = xc.astype(BF16)
    r = _sigmoid(_dot(xb, wa_ref[...]) + ba_ref[...])
    ig = _sigmoid(_dot(xb, wi_ref[...]) + bi_ref[...])
    log_a = c8_ref[...] * r
    a = jnp.exp(log_a)
    b = jnp.sqrt(-jnp.tanh(log_a) * (a * a + 1.0)) * (ig * xc)
    return a, b


def _lru_pre_body(x_ref, p_ref, cw_ref, cb_ref, wa_ref, ba_ref, wi_ref, bi_ref, c8_ref,
                  a_ref, b_ref, ext_ref, *, tiles_per_seq):
    tm = x_ref.shape[0]
    first = pl.program_id(0) % tiles_per_seq == 0
    ext_ref[0:8, :] = jnp.where(first, 0.0, p_ref[...])
    ext_ref[8:, :] = x_ref[...]
    xc = cb_ref[...] + cw_ref[3:4, :] * x_ref[...]
    for k in (1, 2, 3):
        xc = xc + cw_ref[3 - k:4 - k, :] * ext_ref[8 - k:8 - k + tm, :]
    a, b = _lru_gates(xc, wa_ref, ba_ref, wi_ref, bi_ref, c8_ref)
    a_ref[...] = a
    b_ref[...] = b


def _lru_scan_body(a_ref, b_ref, g_ref, y_ref, hl_ref, h_ref, p_ref):
    t_len = a_ref.shape[0]
    seg_len = t_len // LRU_SEG
    n_q = LRU_SEG // 8

    def step(t, carry):
        out = []
        for qi in range(n_q):
            h, p = carry[2 * qi], carry[2 * qi + 1]
            rows = pl.ds(qi * 8 * seg_len + t, 8, stride=seg_len)
            at = a_ref[rows, :]
            h = at * h + b_ref[rows, :]
            p = at * p
            h_ref[rows, :] = h
            p_ref[rows, :] = p
            out += [h, p]
        return tuple(out)

    init = (jnp.zeros((8, LANES), F32), jnp.ones((8, LANES), F32)) * n_q
    lax.fori_loop(0, seg_len, step, init)
    carry = jnp.zeros((1, LANES), F32)
    for s in range(LRU_SEG):
        rows = slice(s * seg_len, (s + 1) * seg_len)
        h = h_ref[rows, :] + p_ref[rows, :] * carry
        y_ref[rows, :] = (h * _gelu(g_ref[rows, :])).astype(y_ref.dtype)
        carry = h[seg_len - 1:seg_len, :]
    hl_ref[0] = carry


def lru_weights(conv_w, conv_b, w_a, b_a, w_i, b_i, lam):
    eye = jnp.eye(LRU_HEADS, dtype=F32)
    bd = lambda w: jnp.einsum("hij,hk->hikj", w, eye).reshape(LRU_WIDTH, LRU_WIDTH).astype(BF16)
    c8 = (-LRU_C) * jax.nn.softplus(-lam)
    return (conv_w, conv_b.reshape(1, -1), bd(w_a), b_a.reshape(1, -1), bd(w_i), b_i.reshape(1, -1),
            c8.reshape(1, -1))


def lru_prompt(mid, lw, tm):
    conv_w, conv_b, wa, ba, wi, bi, c8 = lw
    fixed = lambda i: (0, 0)
    vec = pl.BlockSpec((1, LRU_WIDTH), fixed)
    mat = pl.BlockSpec((LRU_WIDTH, LRU_WIDTH), fixed)
    a, b = _call(
        functools.partial(_lru_pre_body, tiles_per_seq=SEQ // tm), grid=(M_PROMPT // tm,),
        in_specs=[pl.BlockSpec((tm, LRU_WIDTH), lambda i: (i, 1)),
                  pl.BlockSpec((8, LRU_WIDTH), lambda i: (jnp.maximum(i * (tm // 8) - 1, 0), 1)),
                  pl.BlockSpec((4, LRU_WIDTH), fixed), vec, mat, vec, mat, vec, vec],
        out_specs=[pl.BlockSpec((tm, LRU_WIDTH), lambda i: (i, 0))] * 2,
        out_shape=[jax.ShapeDtypeStruct((M_PROMPT, LRU_WIDTH), F32)] * 2,
        scratch_shapes=[pltpu.VMEM((tm + 8, LRU_WIDTH), F32)],
        name="lru_pre")(mid, mid, conv_w, conv_b, wa, ba, wi, bi, c8)
    n_lane = LRU_WIDTH // LANES
    g_col0 = (S5_WIDTH + LRU_WIDTH) // LANES
    blk = lambda off: pl.BlockSpec((SEQ, LANES), lambda bi_, c: (bi_, off + c))
    y, h_last = _call(
        _lru_scan_body, grid=(BATCH, n_lane),
        in_specs=[blk(0), blk(0), blk(g_col0)],
        out_specs=[blk(0), pl.BlockSpec((1, 1, LANES), lambda bi_, c: (bi_, 0, c))],
        out_shape=[jax.ShapeDtypeStruct((M_PROMPT, LRU_WIDTH), BF16),
                   jax.ShapeDtypeStruct((BATCH, 1, LRU_WIDTH), F32)],
        scratch_shapes=[pltpu.VMEM((SEQ, LANES), F32), pltpu.VMEM((SEQ, LANES), F32)],
        name="lru_scan")(a, b, mid)
    return y, h_last.reshape(BATCH, LRU_WIDTH)


def _lru_decode_body(x_ref, g_ref, c0_ref, c1_ref, c2_ref, h0_ref, cw_ref, cb_ref, wa_ref, ba_ref,
                     wi_ref, bi_ref, c8_ref, y_ref, h_ref):
    xc = (cb_ref[...] + cw_ref[0:1, :] * c0_ref[...] + cw_ref[1:2, :] * c1_ref[...]
          + cw_ref[2:3, :] * c2_ref[...] + cw_ref[3:4, :] * x_ref[...])
    a, b = _lru_gates(xc, wa_ref, ba_ref, wi_ref, bi_ref, c8_ref)
    h = a * h0_ref[...] + b
    h_ref[...] = h
    y_ref[...] = (h * _gelu(g_ref[...])).astype(y_ref.dtype)


def lru_decode(mid, conv_buf, h0, lw):
    conv_w, conv_b, wa, ba, wi, bi, c8 = lw
    nb = DEC_BATCH
    fixed = lambda i: (0, 0)
    vec = pl.BlockSpec((1, LRU_WIDTH), fixed)
    mat = pl.BlockSpec((LRU_WIDTH, LRU_WIDTH), fixed)
    st = pl.BlockSpec((nb, LRU_WIDTH), fixed)
    return _call(
        _lru_decode_body, grid=(1,),
        in_specs=[pl.BlockSpec((nb, LRU_WIDTH), lambda i: (0, 1)), pl.BlockSpec((nb, LRU_WIDTH), lambda i: (0, 2)),
                  st, st, st, st, pl.BlockSpec((4, LRU_WIDTH), fixed), vec, mat, vec, mat, vec, vec],
        out_specs=[st, st],
        out_shape=[jax.ShapeDtypeStruct((nb, LRU_WIDTH), BF16), jax.ShapeDtypeStruct((nb, LRU_WIDTH), F32)],
        name="lru_decode")(mid, mid, conv_buf[:, 0], conv_buf[:, 1], conv_buf[:, 2], h0,
                           conv_w, conv_b, wa, ba, wi, bi, c8)


TM = 1024
TM_SMALL = 512
TM_EXPERT = 512
M_ALL = M_PROMPT + DEC_BATCH
N_EXPERT_TILES = (2 * M_ALL) // TM_EXPERT + N_EXPERTS
DEC_ATT_BS = (16, 8, 2)


def _token_mixers(layer, xp, xs, p, caches_t, win_prev, state_s5, state_conv, state_lru):
    n_p = rmsnorm(xp, p["norm_mix"][layer], TM_SMALL, BF16)
    n_s = rmsnorm(xs, p["norm_mix"][layer], DEC_BATCH, BF16)
    w_in = p["w_in"][layer]
    proj = functools.partial(matmul_ws, [n_p], [n_s], w_of_x=[0], tm=TM)
    qkv_p, qkv_s = proj([(w_in[:, :OFF_MID], ())], tn=768, out_dtype=F32, epilogue=_ep_plain, name="proj_qkv")
    mid_p, mid_s = proj([(w_in[:, OFF_MID:OFF_GATE], ())], tn=640, out_dtype=F32, epilogue=_ep_plain,
                        name="proj_mid")
    gates_p, gates_s = proj([(w_in[:, OFF_GATE:], ())], tn=1024, out_dtype=BF16, out_dtype_s=F32,
                            epilogue=_ep_sigmoid, name="proj_gates")

    qn_p, kn_p = qk_norm(qkv_p, p["q_gain"][layer], p["k_gain"][layer], TM_SMALL, F32)
    qn_s, kn_s = qk_norm(qkv_s, p["q_gain"][layer], p["k_gain"][layer], DEC_BATCH, F32)
    outs, lses, outs_s, lses_s = [], [], [], []
    new_p, win = [], []
    for g, (window, dil) in enumerate(ATT_GROUPS):
        cols = slice(GROUP_W * g, GROUP_W * (g + 1))
        v_cols = slice(2 * ATT_WIDTH + GROUP_W * g, 2 * ATT_WIDTH + GROUP_W * (g + 1))
        o, lse = prompt_attention(g, qn_p, kn_p, qkv_p)
        outs.append(o)
        lses.append(lse)
        keep = min(window, SEQ)
        new_p.append(kn_p[:, cols].reshape(BATCH, SEQ, HEADS, HEAD_DIM)[:, SEQ - keep:])
        new_p.append(qkv_p[:, v_cols].reshape(BATCH, SEQ, HEADS, HEAD_DIM)[:, SEQ - keep:])
        cols_s = jnp.concatenate([qn_s[:, cols], kn_s[:, cols], qkv_s[:, v_cols]], axis=1).T
        prev_k, prev_v = (None, None) if win_prev is None else (win_prev[2 * g], win_prev[2 * g + 1])
        o_s, lse_s, wk, wv = decode_attention(layer, g, cols_s, caches_t[2 * g], caches_t[2 * g + 1],
                                              prev_k, prev_v, DEC_ATT_BS[g])
        outs_s.append(o_s.reshape(DEC_BATCH, GROUP_W))
        lses_s.append(lse_s.reshape(DEC_BATCH, GROUP_W))
        win += [wk, wv]
    att_p = mix_groups(outs, lses, TM_SMALL, BF16)
    att_s = mix_groups(outs_s, lses_s, DEC_BATCH, BF16)

    s5p = (p["s5_lam_re"][layer], p["s5_lam_im"][layer], p["s5_log_dt"][layer], p["s5_b_re"][layer],
           p["s5_b_im"][layer], p["s5_c_re"][layer], p["s5_c_im"][layer])
    n_chunk = SEQ // S5_CHUNK
    u_p = mid_p[:, :S5_WIDTH].reshape(BATCH * n_chunk, S5_CHUNK, S5_GROUPS, S5_CH)
    u_p = u_p.transpose(2, 0, 1, 3).reshape(S5_GROUPS, BATCH * n_chunk, S5_CHUNK * S5_CH)
    y_p, h_p = s5_scan(u_p, jnp.zeros((BATCH, S5_GROUPS, 2 * S5_STATE), F32),
                       s5_weights(*s5p, chunk=S5_CHUNK, pad=0), bb=1)
    y_p = (y_p.reshape(S5_GROUPS, BATCH * n_chunk, S5_CHUNK, S5_CH).transpose(1, 2, 0, 3)
           .reshape(M_PROMPT, S5_WIDTH))
    y_s, h_s = s5_decode(mid_s, state_s5[layer], *s5p)
    glu = (p["s5_d"][layer], p["s5_w_glu"][layer], p["s5_b_glu"][layer])
    s5_p = s5_post(y_p, mid_p, *glu, TM_SMALL)
    s5_s = s5_post(y_s, mid_s, *glu, DEC_BATCH)
    new_p.append(jnp.stack([h_p[..., :S5_STATE], h_p[..., S5_STATE:]], axis=-1))
    new_s = [h_s]

    lw = lru_weights(p["lru_conv_w"][layer], p["lru_conv_b"][layer], p["lru_w_a"][layer], p["lru_b_a"][layer],
                     p["lru_w_i"][layer], p["lru_b_i"][layer], p["lru_lam"][layer])
    lru_p, h_lp = lru_prompt(mid_p, lw, 512)
    lru_s, h_ls = lru_decode(mid_s, state_conv[layer], state_lru[layer], lw)
    x_lru = slice(S5_WIDTH, S5_WIDTH + LRU_WIDTH)
    new_p.append(mid_p[:, x_lru].reshape(BATCH, SEQ, LRU_WIDTH)[:, SEQ - 3:])
    new_p.append(h_lp)
    new_s.append(jnp.concatenate([state_conv[layer, :, 1:], mid_s[:, None, x_lru]], axis=1))
    new_s.append(h_ls)

    tn = 512
    gate_tiles = D_MODEL // tn
    merged_p, merged_s = matmul_ws(
        [att_p, s5_p, lru_p], [att_s, s5_s, lru_s],
        [(p["w_br_attn"], (layer,)), (p["w_br_s5"], (layer,)), (p["w_br_lru"], (layer,))],
        w_of_x=[0, 1, 2], tm=TM, tn=tn, out_dtype=BF16, epilogue=_ep_merge,
        tile_extras=[(gates_p, gates_s, 0), (gates_p, gates_s, gate_tiles), (gates_p, gates_s, 2 * gate_tiles)],
        name="merge")
    xp, xs = matmul_ws([merged_p], [merged_s], [(p["w_o"], (layer,))], w_of_x=[0], tm=TM, tn=512,
                       out_dtype=F32, epilogue=_ep_residual, tile_extras=[(xp, xs, 0)], name="out_proj")
    return xp, xs, new_p, new_s, win


def _dense_ffn(xp, xs, p, layer, j):
    n_p = rmsnorm(xp, p["norm_ffn"][layer], TM_SMALL, BF16)
    n_s = rmsnorm(xs, p["norm_ffn"][layer], DEC_BATCH, BF16)
    h_p, h_s = matmul_ws([n_p], [n_s], [(p["ffn_w_gate"], (j,)), (p["ffn_w_up"], (j,))], w_of_x=[0, 0],
                         tm=TM, tn=512, out_dtype=BF16, epilogue=_ep_swiglu, name="ffn_up")
    return matmul_ws([h_p], [h_s], [(p["ffn_w_down"], (j,))], w_of_x=[0], tm=512, tn=512, out_dtype=F32,
                     epilogue=_ep_residual, tile_extras=[(xp, xs, 0)], name="ffn_down")


def _moe_ffn(xp, xs, p, layer, j):
    router = (p["norm_ffn"][layer], p["moe_router_w"][j], p["moe_router_b"][j])
    n_p, comb_p, idx_p = rmsnorm_router(xp, *router, TM_SMALL)
    n_s, comb_s, idx_s = rmsnorm_router(xs, *router, DEC_BATCH)
    n = jnp.concatenate([n_p, n_s], axis=0)
    comb = jnp.concatenate([comb_p, comb_s], axis=0)
    idx = jnp.concatenate([idx_p, idx_s], axis=0)[:, :2]
    m = M_ALL
    top_e = idx.reshape(-1)
    top_w = jnp.take_along_axis(comb, idx, axis=1).reshape(-1)
    counts = jnp.sum(top_e[:, None] == jnp.arange(N_EXPERTS)[None, :], axis=0).astype(jnp.int32)
    padded = ((counts + TM_EXPERT - 1) // TM_EXPERT) * TM_EXPERT
    pad_end = jnp.cumsum(padded)
    pad_start = pad_end - padded
    start = jnp.cumsum(counts) - counts
    order = jnp.argsort(top_e, stable=True).astype(jnp.int32)
    rank = jnp.argsort(order).astype(jnp.int32)
    dest = pad_start[top_e] + rank - start[top_e]
    n_rows = N_EXPERT_TILES * TM_EXPERT
    tile_start = jnp.arange(N_EXPERT_TILES, dtype=jnp.int32) * TM_EXPERT
    tile_expert = jnp.minimum(jnp.sum(tile_start[:, None] >= pad_end[None, :], axis=1), N_EXPERTS - 1)
    tile_expert = tile_expert.astype(jnp.int32)
    n_used = (pad_end[-1] // TM_EXPERT).reshape(1).astype(jnp.int32)
    row_e = jnp.repeat(tile_expert, TM_EXPERT)
    within = jnp.arange(n_rows, dtype=jnp.int32) - pad_start[row_e]
    row_valid = within < counts[row_e]
    row_pair = order[jnp.clip(start[row_e] + within, 0, 2 * m - 1)]
    src_token = jnp.where(row_valid, row_pair // 2, 0)
    row_w = jnp.where(row_valid, top_w[row_pair], 0.0)
    x_sorted = jnp.take(n, src_token, axis=0, mode="clip")
    gmm = functools.partial(grouped_matmul_ws, tile_expert, n_used, lead=(j,), tm=TM_EXPERT)
    gate = gmm(x_sorted, p["moe_w_gate"], tn=1408, out_dtype=BF16, epilogue=_ep_plain, name="moe_gate")
    h = gmm(x_sorted, p["moe_w_up"], tn=1408, out_dtype=BF16, epilogue=_ep_silu_times, tile_extras=[gate],
            name="moe_up")
    ys = gmm(h, p["moe_w_down"], tn=1024, out_dtype=F32, epilogue=_ep_scale_rows,
             col_extras=[row_w.reshape(-1, 1)], name="moe_down")
    dest2 = dest.reshape(m, 2)
    y = jnp.take(ys, dest2[:, 0], axis=0, mode="clip") + jnp.take(ys, dest2[:, 1], axis=0, mode="clip")
    return xp + y[:M_PROMPT], xs + y[M_PROMPT:]


def kernel(x_prompt, x_sample, cache_k_w128, cache_v_w128, cache_k_w512, cache_v_w512, cache_k_w2048, cache_v_w2048, state_s5, state_conv, state_lru, norm_mix, w_in, q_gain, k_gain, s5_lam_re, s5_lam_im, s5_log_dt, s5_b_re, s5_b_im, s5_c_re, s5_c_im, s5_d, s5_w_glu, s5_b_glu, lru_conv_w, lru_conv_b, lru_w_a, lru_b_a, lru_w_i, lru_b_i, lru_lam, w_br_attn, w_br_s5, w_br_lru, w_o, norm_ffn, ffn_w_gate, ffn_w_up, ffn_w_down, moe_router_w, moe_router_b, moe_w_gate, moe_w_up, moe_w_down):
    p = dict(norm_mix=norm_mix, w_in=w_in, q_gain=q_gain, k_gain=k_gain, s5_lam_re=s5_lam_re,
             s5_lam_im=s5_lam_im, s5_log_dt=s5_log_dt, s5_b_re=s5_b_re, s5_b_im=s5_b_im, s5_c_re=s5_c_re,
             s5_c_im=s5_c_im, s5_d=s5_d, s5_w_glu=s5_w_glu, s5_b_glu=s5_b_glu, lru_conv_w=lru_conv_w,
             lru_conv_b=lru_conv_b, lru_w_a=lru_w_a, lru_b_a=lru_b_a, lru_w_i=lru_w_i, lru_b_i=lru_b_i,
             lru_lam=lru_lam, w_br_attn=w_br_attn, w_br_s5=w_br_s5, w_br_lru=w_br_lru, w_o=w_o,
             norm_ffn=norm_ffn, ffn_w_gate=ffn_w_gate, ffn_w_up=ffn_w_up, ffn_w_down=ffn_w_down,
             moe_router_w=moe_router_w, moe_router_b=moe_router_b, moe_w_gate=moe_w_gate,
             moe_w_up=moe_w_up, moe_w_down=moe_w_down)
    caches_t = [jnp.transpose(c, (0, 1, 3, 4, 2)) for c in
                (cache_k_w128, cache_v_w128, cache_k_w512, cache_v_w512, cache_k_w2048, cache_v_w2048)]
    xp = x_prompt.reshape(M_PROMPT, D_MODEL)
    xs = x_sample.reshape(DEC_BATCH, D_MODEL)
    states_p = [[] for _ in range(9)]
    states_s = [[] for _ in range(3)]
    win = None
    for layer in range(DEPTH):
        xp, xs, new_p, new_s, win = _token_mixers(layer, xp, xs, p, caches_t, win, state_s5, state_conv,
                                                  state_lru)
        if layer % 2 == 0:
            xp, xs = _dense_ffn(xp, xs, p, layer, layer // 2)
        else:
            xp, xs = _moe_ffn(xp, xs, p, layer, layer // 2)
        for i in range(9):
            states_p[i].append(new_p[i])
        for i in range(3):
            states_s[i].append(new_s[i])
    y_prompt = xp.reshape(BATCH, SEQ, D_MODEL)
    y_sample = xs.reshape(DEC_BATCH, 1, D_MODEL)
    windows = [jnp.transpose(w, (0, 1, 4, 2, 3)) for w in win]
    return (y_prompt, y_sample, *[jnp.stack(a) for a in states_p], *windows,
            *[jnp.stack(a) for a in states_s])
```

```python
import functools
import math

import jax
import jax.numpy as jnp
from jax import lax
from jax.experimental import pallas as pl
from jax.experimental.pallas import tpu as pltpu

F32 = jnp.float32
BF16 = jnp.bfloat16

D_MODEL = 2048
BATCH = 2
SEQ = 4096
DEPTH = 2
DEC_BATCH = 128
HEAD_DIM = 64
HEADS = 4
GROUP_W = HEADS * HEAD_DIM
ATT_GROUPS = ((128, 1), (512, 4), (2048, 16))
N_GROUPS = 3
ATT_WIDTH = N_GROUPS * GROUP_W
N_KEYS = 128
S5_WIDTH = 640
S5_GROUPS = 40
S5_CH = 16
S5_STATE = 64
LRU_WIDTH = 640
LRU_HEADS = 8
LRU_C = 8.0
OFF_MID = 3 * ATT_WIDTH
MID_WIDTH = S5_WIDTH + 2 * LRU_WIDTH
OFF_GATE = OFF_MID + MID_WIDTH
N_EXPERTS = 8
D_FF_EXPERT = 2816
EPS = 1e-6
LANES = 128

M_PROMPT = BATCH * SEQ

S5_CHUNK = 8
S5_PACK = LANES // S5_CH
N_SUPER = S5_GROUPS // S5_PACK
LRU_SEG = 32
NEG_BIG = -1e30

VMEM_LIMIT_BYTES = 56 * 1024 * 1024


def _call(body, *, grid, in_specs, out_specs, out_shape, name, scratch_shapes=(), num_scalar_prefetch=0,
          input_output_aliases=None):
    params = pltpu.CompilerParams(dimension_semantics=("arbitrary",) * len(grid),
                                  vmem_limit_bytes=VMEM_LIMIT_BYTES)
    kwargs = {}
    if input_output_aliases:
        kwargs["input_output_aliases"] = input_output_aliases
    if num_scalar_prefetch:
        grid_spec = pltpu.PrefetchScalarGridSpec(
            num_scalar_prefetch=num_scalar_prefetch, grid=grid, in_specs=in_specs,
            out_specs=out_specs, scratch_shapes=scratch_shapes)
        return pl.pallas_call(body, grid_spec=grid_spec, out_shape=out_shape,
                              compiler_params=params, name=name, **kwargs)
    return pl.pallas_call(body, grid=grid, in_specs=in_specs, out_specs=out_specs,
                          out_shape=out_shape, scratch_shapes=scratch_shapes,
                          compiler_params=params, name=name, **kwargs)


def _dot(a, b):
    return jnp.dot(a, b, preferred_element_type=F32)


def _dot_sel(x, sel, parts=2):
    acc = None
    r = x
    for _ in range(parts):
        piece = r.astype(BF16)
        r = r - piece.astype(F32)
        d = _dot(piece, sel)
        acc = d if acc is None else acc + d
    return acc


def _sigmoid(x):
    return 1.0 / (1.0 + jnp.exp(-x))


def _gelu(x):
    c = math.sqrt(2.0 / math.pi)
    return 0.5 * x * (1.0 + jnp.tanh(c * (x + 0.044715 * (x * x * x))))


def _rmsnorm_body(x_ref, g_ref, o_ref):
    x = x_ref[...]
    y = x * lax.rsqrt(jnp.mean(x * x, axis=-1, keepdims=True) + EPS)
    o_ref[...] = (y * g_ref[...]).astype(o_ref.dtype)


def _rmsnorm_router_body(x_ref, g_ref, rw_ref, rb_ref, o_ref, comb_ref, idx_ref):
    x = x_ref[...]
    y = x * lax.rsqrt(jnp.mean(x * x, axis=-1, keepdims=True) + EPS)
    xn = y * g_ref[...]
    o_ref[...] = xn.astype(o_ref.dtype)
    logits = _dot(xn.astype(BF16), rw_ref[...].astype(BF16)) + rb_ref[...]
    lane = lax.broadcasted_iota(jnp.int32, logits.shape, 1).astype(F32)
    m1 = jnp.max(logits, axis=1, keepdims=True)
    i1 = jnp.min(jnp.where(logits == m1, lane, float(LANES)), axis=1, keepdims=True)
    rest = jnp.where(lane == i1, -jnp.inf, logits)
    m2 = jnp.max(rest, axis=1, keepdims=True)
    i2 = jnp.min(jnp.where(rest == m2, lane, float(LANES)), axis=1, keepdims=True)
    e = jnp.exp(m2 - m1)
    w1 = 1.0 / (1.0 + e)
    w2 = e / (1.0 + e)
    comb_ref[...] = jnp.where(lane == i1, w1, 0.0) + jnp.where(lane == i2, w2, 0.0)
    idx_ref[...] = jnp.where(lane == 0.0, i1, jnp.where(lane == 1.0, i2, 0.0)).astype(jnp.int32)


def rmsnorm(x, gain, tm, out_dtype):
    m, d = x.shape
    return _call(
        _rmsnorm_body, grid=(m // tm,),
        in_specs=[pl.BlockSpec((tm, d), lambda i: (i, 0)), pl.BlockSpec((1, d), lambda i: (0, 0))],
        out_specs=pl.BlockSpec((tm, d), lambda i: (i, 0)),
        out_shape=jax.ShapeDtypeStruct((m, d), out_dtype), name="rmsnorm")(x, gain.reshape(1, d))


def rmsnorm_router(x, gain, router_w, router_b, tm):
    m, d = x.shape
    ne = router_w.shape[1]
    rw = jnp.zeros((d, LANES), F32).at[:, :ne].set(router_w)
    rb = jnp.full((1, LANES), NEG_BIG, F32).at[0, :ne].set(router_b)
    row = lambda i: (i, 0)
    fixed = lambda i: (0, 0)
    return _call(
        _rmsnorm_router_body, grid=(m // tm,),
        in_specs=[pl.BlockSpec((tm, d), row), pl.BlockSpec((1, d), fixed),
                  pl.BlockSpec((d, LANES), fixed), pl.BlockSpec((1, LANES), fixed)],
        out_specs=[pl.BlockSpec((tm, d), row), pl.BlockSpec((tm, LANES), row), pl.BlockSpec((tm, LANES), row)],
        out_shape=[jax.ShapeDtypeStruct((m, d), F32), jax.ShapeDtypeStruct((m, LANES), F32),
                   jax.ShapeDtypeStruct((m, LANES), jnp.int32)],
        name="rmsnorm_router")(x, gain.reshape(1, d), rw, rb)


def _mm_body(*refs, n_x, w_of_x, n_te, n_re, epilogue, n_ptiles):
    n_w = len(w_of_x)
    pos = 0
    xp_refs = refs[pos:pos + n_x]; pos += n_x
    xs_refs = refs[pos:pos + n_x]; pos += n_x
    w_refs = refs[pos:pos + n_w]; pos += n_w
    tep_refs = refs[pos:pos + n_te]; pos += n_te
    tes_refs = refs[pos:pos + n_te]; pos += n_te
    re_refs = refs[pos:pos + n_re]; pos += n_re
    op_ref, os_ref = refs[pos], refs[pos + 1]; pos += 2
    wb_refs = refs[pos:pos + n_w]
    i = pl.program_id(1)

    @pl.when(i == 0)
    def _():
        for w_ref, wb_ref in zip(w_refs, wb_refs):
            wb_ref[...] = w_ref[...].astype(BF16)

    def rows(x_refs, te_refs, o_ref):
        accs = [_dot(x_refs[xi][...].astype(BF16), wb_ref[...]) for xi, wb_ref in zip(w_of_x, wb_refs)]
        extras = [e[...] for e in te_refs] + [e[...] for e in re_refs]
        o_ref[...] = epilogue(accs, extras).astype(o_ref.dtype)

    @pl.when(i < n_ptiles)
    def _():
        rows(xp_refs, tep_refs, op_ref)

    @pl.when(i == n_ptiles)
    def _():
        rows(xs_refs, tes_refs, os_ref)


def matmul_ws(xps, xss, ws, *, w_of_x, tm, tn, out_dtype, epilogue, tile_extras=(), row_extras=(), name,
              out_dtype_s=None, n_cols=None):
    mp = xps[0].shape[0]
    ms = xss[0].shape[0]
    n = n_cols or ws[0][0].shape[-1]
    n_ptiles = mp // tm
    last = n_ptiles - 1
    grid = (n // tn, n_ptiles + 1)
    in_specs = [pl.BlockSpec((tm, x.shape[1]), lambda j, i: (jnp.minimum(i, last), 0)) for x in xps]
    in_specs += [pl.BlockSpec((ms, x.shape[1]), lambda j, i: (0, 0)) for x in xss]
    scratch = []
    for w, lead in ws:
        k = w.shape[-2]
        block = (None,) * len(lead) + (k, tn)
        in_specs.append(pl.BlockSpec(block, lambda j, i, lead=lead: lead + (0, j)))
        scratch.append(pltpu.VMEM((k, tn), BF16))
    for _, _, off in tile_extras:
        in_specs.append(pl.BlockSpec((tm, tn), lambda j, i, off=off: (jnp.minimum(i, last), off + j)))
    for _, _, off in tile_extras:
        in_specs.append(pl.BlockSpec((ms, tn), lambda j, i, off=off: (0, off + j)))
    for _ in row_extras:
        in_specs.append(pl.BlockSpec((1, tn), lambda j, i: (0, j)))
    body = functools.partial(_mm_body, n_x=len(xps), w_of_x=tuple(w_of_x), n_te=len(tile_extras),
                             n_re=len(row_extras), epilogue=epilogue, n_ptiles=n_ptiles)
    return _call(
        body, grid=grid, in_specs=in_specs,
        out_specs=[pl.BlockSpec((tm, tn), lambda j, i: (jnp.minimum(i, last), j)),
                   pl.BlockSpec((ms, tn), lambda j, i: (0, j))],
        out_shape=[jax.ShapeDtypeStruct((mp, n), out_dtype),
                   jax.ShapeDtypeStruct((ms, n), out_dtype_s or out_dtype)],
        scratch_shapes=scratch, name=name,
    )(*xps, *xss, *[w for w, _ in ws], *[a for a, _, _ in tile_extras], *[a for _, a, _ in tile_extras],
      *row_extras)


def _ep_plain(accs, extras):
    return accs[0]


def _ep_sigmoid(accs, extras):
    return _sigmoid(accs[0])


def _ep_residual(accs, extras):
    return extras[0] + accs[0]


def _ep_swiglu(accs, extras):
    g, u = accs
    return (g * _sigmoid(g)) * u


def _ep_merge(accs, extras):
    return (extras[0].astype(F32) * accs[0] + extras[1].astype(F32) * accs[1]
            + extras[2].astype(F32) * accs[2])


def _gmm_body(te_ref, nu_ref, *refs, n_te, n_ce, epilogue):
    x_ref, w_ref = refs[0], refs[1]
    extra_refs = refs[2:2 + n_te + n_ce]
    o_ref = refs[2 + n_te + n_ce]
    wb_ref = refs[3 + n_te + n_ce]
    i = pl.program_id(1)
    new_expert = jnp.logical_or(i == 0, te_ref[i] != te_ref[jnp.maximum(i - 1, 0)])

    @pl.when(new_expert)
    def _():
        wb_ref[...] = w_ref[...].astype(BF16)

    @pl.when(i < nu_ref[0])
    def _():
        acc = _dot(x_ref[...].astype(BF16), wb_ref[...])
        o_ref[...] = epilogue([acc], [e[...] for e in extra_refs]).astype(o_ref.dtype)

    @pl.when(i >= nu_ref[0])
    def _():
        o_ref[...] = jnp.zeros(o_ref.shape, o_ref.dtype)


def grouped_matmul_ws(tile_expert, n_used, x, w, lead, *, tm, tn, out_dtype, epilogue, tile_extras=(),
                      col_extras=(), name):
    m, k = x.shape
    n = w.shape[-1]
    grid = (n // tn, m // tm)
    block = (None,) * (len(lead) + 1) + (k, tn)
    in_specs = [pl.BlockSpec((tm, k), lambda j, i, te, nu: (i, 0)),
                pl.BlockSpec(block, lambda j, i, te, nu: lead + (te[i], 0, j))]
    in_specs += [pl.BlockSpec((tm, tn), lambda j, i, te, nu: (i, j)) for _ in tile_extras]
    in_specs += [pl.BlockSpec((tm, 1), lambda j, i, te, nu: (i, 0)) for _ in col_extras]
    body = functools.partial(_gmm_body, n_te=len(tile_extras), n_ce=len(col_extras), epilogue=epilogue)
    return _call(
        body, grid=grid, in_specs=in_specs,
        out_specs=pl.BlockSpec((tm, tn), lambda j, i, te, nu: (i, j)),
        out_shape=jax.ShapeDtypeStruct((m, n), out_dtype), scratch_shapes=[pltpu.VMEM((k, tn), BF16)],
        num_scalar_prefetch=2, name=name,
    )(tile_expert, n_used, x, w, *tile_extras, *col_extras)


def _ep_silu_times(accs, extras):
    g = extras[0].astype(F32)
    return (g * _sigmoid(g)) * accs[0]


def _ep_scale_rows(accs, extras):
    return extras[0] * accs[0]


def _qknorm_body(q_ref, k_ref, qg_ref, kg_ref, seg_ref, qo_ref, ko_ref):
    seg = seg_ref[...]
    for src, gain, dst, scale in ((q_ref, qg_ref, qo_ref, HEAD_DIM ** -0.5), (k_ref, kg_ref, ko_ref, 1.0)):
        for g in range(N_GROUPS):
            cols = slice(GROUP_W * g, GROUP_W * (g + 1))
            x = src[:, cols]
            ms = _dot_sel(x * x, seg, 3)
            y = x * lax.rsqrt(ms + EPS) * gain[:, cols]
            dst[:, cols] = (y * scale).astype(dst.dtype)


def qk_norm(qkv, q_gain, k_gain, tm, q_dtype):
    m = qkv.shape[0]
    head = jnp.arange(GROUP_W) // HEAD_DIM
    seg = jnp.where(head[:, None] == head[None, :], 1.0 / HEAD_DIM, 0.0).astype(BF16)
    qg = jnp.broadcast_to(q_gain[:, None, :], (N_GROUPS, HEADS, HEAD_DIM)).reshape(1, ATT_WIDTH)
    kg = jnp.broadcast_to(k_gain[:, None, :], (N_GROUPS, HEADS, HEAD_DIM)).reshape(1, ATT_WIDTH)
    fixed = lambda i: (0, 0)
    return _call(
        _qknorm_body, grid=(m // tm,),
        in_specs=[pl.BlockSpec((tm, ATT_WIDTH), lambda i: (i, 0)), pl.BlockSpec((tm, ATT_WIDTH), lambda i: (i, 1)),
                  pl.BlockSpec((1, ATT_WIDTH), fixed), pl.BlockSpec((1, ATT_WIDTH), fixed),
                  pl.BlockSpec((GROUP_W, GROUP_W), fixed)],
        out_specs=[pl.BlockSpec((tm, ATT_WIDTH), lambda i: (i, 0)), pl.BlockSpec((tm, ATT_WIDTH), lambda i: (i, 0))],
        out_shape=[jax.ShapeDtypeStruct((m, ATT_WIDTH), q_dtype), jax.ShapeDtypeStruct((m, ATT_WIDTH), F32)],
        name="qk_norm")(qkv, qkv, qg, kg, seg)


def _attn_body(*refs, dil):
    q_refs, kp_refs, kc_refs, vp_refs, vc_refs = (refs[2 * n:2 * n + 2] for n in range(5))
    o_ref, lse_ref = refs[10], refs[11]
    o_stage, lse_stage = refs[12:14], refs[14:16]
    a = pl.program_id(1)
    tq = N_KEYS
    r = lax.broadcasted_iota(jnp.int32, (HEADS * tq, 2 * tq), 0) % tq
    c = lax.broadcasted_iota(jnp.int32, (HEADS * tq, 2 * tq), 1)
    valid = (c >= r) & (c <= r + N_KEYS) & ((a > 0) | (c >= tq))
    head = lax.broadcasted_iota(jnp.int32, (1, GROUP_W), 1) // HEAD_DIM

    def residue(res, carry):
        rows = pl.ds(res, tq, stride=dil) if dil > 1 else slice(None)
        load = lambda halves: jnp.concatenate([halves[0][rows, :], halves[1][rows, :]], axis=1)
        q = load(q_refs)
        k2 = jnp.concatenate([load(kp_refs), load(kc_refs)], axis=0).astype(BF16)
        v2 = jnp.concatenate([load(vp_refs), load(vc_refs)], axis=0).astype(BF16)
        q4 = jnp.concatenate([jnp.where(head == h, q, 0.0) for h in range(HEADS)], axis=0).astype(BF16)
        s = lax.dot_general(q4, k2, (((1,), (1,)), ((), ())), preferred_element_type=F32)
        s = jnp.where(valid, s, NEG_BIG)
        m = jnp.max(s, axis=1, keepdims=True)
        p = jnp.exp(s - m)
        l = jnp.sum(p, axis=1, keepdims=True)
        o4 = _dot(p.astype(BF16), v2) / l
        lse4 = m + jnp.log(l)
        o = jnp.zeros((tq, GROUP_W), F32)
        lse = jnp.zeros((tq, GROUP_W), F32)
        for h in range(HEADS):
            blk = slice(h * tq, (h + 1) * tq)
            o = jnp.where(head == h, o4[blk, :], o)
            lse = jnp.where(head == h, lse4[blk, :], lse)
        for half in range(2):
            lanes = slice(LANES * half, LANES * (half + 1))
            o_stage[half][rows, :] = o[:, lanes]
            lse_stage[half][rows, :] = lse[:, lanes]
        return carry

    if dil == 1:
        residue(0, 0)
    else:
        lax.fori_loop(0, dil, residue, 0)
    for half in range(2):
        lanes = slice(LANES * half, LANES * (half + 1))
        o_ref[:, lanes] = o_stage[half][...]
        lse_ref[:, lanes] = lse_stage[half][...]


def prompt_attention(g, qn, kn, qkv):
    _, dil = ATT_GROUPS[g]
    span = N_KEYS * dil
    n_span = SEQ // span
    halves_per_group = GROUP_W // LANES
    cur = lambda col: (lambda b, a: (b * n_span + a, col))
    prev = lambda col: (lambda b, a: (b * n_span + jnp.maximum(a - 1, 0), col))
    blk = (span, LANES)
    qk_cols = [halves_per_group * g + half for half in range(halves_per_group)]
    v_cols = [halves_per_group * (2 * N_GROUPS + g) + half for half in range(halves_per_group)]
    in_specs = ([pl.BlockSpec(blk, cur(c)) for c in qk_cols] + [pl.BlockSpec(blk, prev(c)) for c in qk_cols]
                + [pl.BlockSpec(blk, cur(c)) for c in qk_cols] + [pl.BlockSpec(blk, prev(c)) for c in v_cols]
                + [pl.BlockSpec(blk, cur(c)) for c in v_cols])
    out = pl.BlockSpec((span, GROUP_W), lambda b, a: (b * n_span + a, 0))
    return _call(
        functools.partial(_attn_body, dil=dil), grid=(BATCH, n_span),
        in_specs=in_specs, out_specs=[out, out],
        out_shape=[jax.ShapeDtypeStruct((M_PROMPT, GROUP_W), F32)] * 2,
        scratch_shapes=[pltpu.VMEM((span, LANES), F32)] * 4,
        name="prompt_attention")(qn, qn, kn, kn, kn, kn, qkv, qkv, qkv, qkv)


def _dec_attn_body(*refs, dil, aliased):
    if aliased:
        qkv_ref, ck_ref, cv_ref, _, _, o_ref, lse_ref, nk_ref, nv_ref = refs
    else:
        qkv_ref, ck_ref, cv_ref, o_ref, lse_ref, nk_ref, nv_ref = refs
    bs = ck_ref.shape[1]
    n_chunk = ck_ref.shape[4] // LANES
    lane = lax.broadcasted_iota(jnp.int32, (1, LANES), 1)
    key_lane = (lane % dil) == 0
    not_last = lax.broadcasted_iota(jnp.int32, (HEAD_DIM, LANES), 1) < LANES - 1
    sample_row = lax.broadcasted_iota(jnp.int32, (qkv_ref.shape[1], LANES), 0)
    qkv_t = qkv_ref[...]
    for s in range(bs):
        pick = (sample_row == pl.program_id(0) * bs + s).astype(BF16)
        cols = _dot_sel(qkv_t, pick, 3)
        for h in range(HEADS):
            rows = slice(HEAD_DIM * h, HEAD_DIM * (h + 1))
            qb = cols[rows, :]
            kn = cols[GROUP_W + HEAD_DIM * h:GROUP_W + HEAD_DIM * (h + 1), :]
            vn = cols[2 * GROUP_W + HEAD_DIM * h:2 * GROUP_W + HEAD_DIM * (h + 1), :]
            chunk = lambda ref, c: ref[0, s, h, :, c * LANES:(c + 1) * LANES]
            scores = [jnp.where(key_lane, jnp.sum(chunk(ck_ref, c) * qb, axis=0, keepdims=True), NEG_BIG)
                      for c in range(n_chunk)]
            s_self = jnp.sum(kn * qb, axis=0, keepdims=True)
            m = s_self
            for sc in scores:
                m = jnp.maximum(m, jnp.max(sc, axis=1, keepdims=True))
            p_self = jnp.exp(s_self - m)
            l = p_self
            acc = None
            for c in range(n_chunk):
                p = jnp.exp(scores[c] - m)
                l = l + jnp.sum(p, axis=1, keepdims=True)
                pv = p * chunk(cv_ref, c)
                acc = pv if acc is None else acc + pv
            o = (jnp.sum(acc, axis=1, keepdims=True) + p_self[:, 0:1] * vn[:, 0:1]) / l[:, 0:1]
            o_ref[s, rows, :] = o
            lse_ref[s, rows, :] = jnp.broadcast_to(m[:, 0:1] + jnp.log(l[:, 0:1]), (HEAD_DIM, 1))
            for src, dst, new in ((ck_ref, nk_ref, kn), (cv_ref, nv_ref, vn)):
                rolled = [pltpu.roll(chunk(src, c), LANES - 1, axis=1) for c in range(n_chunk)] + [new]
                for c in range(n_chunk):
                    dst[0, s, h, :, c * LANES:(c + 1) * LANES] = jnp.where(not_last, rolled[c], rolled[c + 1])


def decode_attention(layer, g, qkv_cols, cache_k, cache_v, prev_k, prev_v, bs):
    window, dil = ATT_GROUPS[g]
    cblk = (1, bs, HEADS, HEAD_DIM, window)
    cmap = lambda i: (layer, i, 0, 0, 0)
    col = pl.BlockSpec((bs, GROUP_W, 1), lambda i: (i, 0, 0))
    in_specs = [pl.BlockSpec((3 * GROUP_W, DEC_BATCH), lambda i: (0, 0)),
                pl.BlockSpec(cblk, cmap), pl.BlockSpec(cblk, cmap)]
    args = [qkv_cols, cache_k, cache_v]
    aliases = None
    if prev_k is not None:
        in_specs += [pl.BlockSpec(memory_space=pl.ANY), pl.BlockSpec(memory_space=pl.ANY)]
        args += [prev_k, prev_v]
        aliases = {3: 2, 4: 3}
    return _call(
        functools.partial(_dec_attn_body, dil=dil, aliased=prev_k is not None), grid=(DEC_BATCH // bs,),
        in_specs=in_specs,
        out_specs=[col, col, pl.BlockSpec(cblk, cmap), pl.BlockSpec(cblk, cmap)],
        out_shape=[jax.ShapeDtypeStruct((DEC_BATCH, GROUP_W, 1), F32)] * 2
        + [jax.ShapeDtypeStruct(cache_k.shape, F32)] * 2,
        input_output_aliases=aliases, name="decode_attention")(*args)


def _mix_body(o0_ref, o1_ref, o2_ref, l0_ref, l1_ref, l2_ref, att_ref):
    l0, l1, l2 = l0_ref[...], l1_ref[...], l2_ref[...]
    m = jnp.maximum(jnp.maximum(l0, l1), l2)
    e0, e1, e2 = jnp.exp(l0 - m), jnp.exp(l1 - m), jnp.exp(l2 - m)
    num = e0 * o0_ref[...] + e1 * o1_ref[...] + e2 * o2_ref[...]
    att_ref[...] = (num / (e0 + e1 + e2)).astype(att_ref.dtype)


def mix_groups(outs, lses, tm, out_dtype):
    m = outs[0].shape[0]
    spec = pl.BlockSpec((tm, GROUP_W), lambda i: (i, 0))
    return _call(_mix_body, grid=(m // tm,), in_specs=[spec] * 6, out_specs=spec,
                 out_shape=jax.ShapeDtypeStruct((m, GROUP_W), out_dtype), name="mix_groups")(*outs, *lses)


def _s5_prompt_body(u_ref, toep_ref, wst_ref, wout_ref, a1_ref, a2_ref, y_ref, hl_ref, s_ref, hp_ref):
    n_chunk = s_ref.shape[0]
    u8 = jnp.concatenate([u_ref[pl.ds(s, n_chunk, stride=S5_CHUNK), :] for s in range(S5_CHUNK)],
                         axis=1).astype(BF16)
    s_ref[...] = _dot(u8, wst_ref[0])
    a1, a2 = a1_ref[0], a2_ref[0]
    half = a1.shape[1] // 2

    def step(c, h):
        hp_ref[pl.ds(c, 1), :] = h
        swapped = jnp.concatenate([h[:, half:], h[:, :half]], axis=1)
        return a1 * h + a2 * swapped + s_ref[pl.ds(c, 1), :]

    hl_ref[0, 0] = lax.fori_loop(0, n_chunk, step, jnp.zeros((1, 2 * half), F32))
    y8 = _dot(u8, toep_ref[0]) + _dot(hp_ref[...].astype(BF16), wout_ref[0])
    for t in range(S5_CHUNK):
        y_ref[pl.ds(t, n_chunk, stride=S5_CHUNK), :] = y8[:, t * LANES:(t + 1) * LANES]


def s5_prompt_weights(lam_re, lam_im, log_dt, b_re, b_im, c_re, c_im):
    hp = lax.Precision.HIGHEST
    chunk = S5_CHUNK
    dt = jnp.exp(log_dt)[:, None]
    tau = jnp.arange(chunk + 1, dtype=F32)[:, None, None]
    mag = jnp.exp(lam_re * dt * tau)
    pw_r = mag * jnp.cos(lam_im * dt * tau)
    pw_i = mag * jnp.sin(lam_im * dt * tau)
    abar_r, abar_i = pw_r[1], pw_i[1]
    den = lam_re * lam_re + lam_im * lam_im
    fr = ((abar_r - 1.0) * lam_re + abar_i * lam_im) / den
    fi = (abar_i * lam_re - (abar_r - 1.0) * lam_im) / den
    bbr = fr[..., None] * b_re - fi[..., None] * b_im
    bbi = fr[..., None] * b_im + fi[..., None] * b_re
    ab_r = pw_r[..., None] * bbr - pw_i[..., None] * bbi
    ab_i = pw_r[..., None] * bbi + pw_i[..., None] * bbr
    kern = (jnp.einsum("gop,tgpc->tgoc", c_re, ab_r[:chunk], precision=hp)
            - jnp.einsum("gop,tgpc->tgoc", c_im, ab_i[:chunk], precision=hp))
    s_idx = jnp.arange(chunk)[:, None]
    t_idx = jnp.arange(chunk)[None, :]
    lag = t_idx - s_idx
    k_st = kern[jnp.maximum(lag, 0)]
    k_st = jnp.where((lag >= 0)[:, :, None, None, None], k_st, 0.0)
    eye = jnp.eye(S5_PACK, dtype=F32)
    split = lambda x, axis: x.reshape(x.shape[:axis] + (N_SUPER, S5_PACK) + x.shape[axis + 1:])
    width = chunk * LANES
    n_state = S5_PACK * S5_STATE
    toep = jnp.einsum("stGgoc,gh->Gsgctho", split(k_st, 2), eye).reshape(N_SUPER, width, width)
    rev = chunk - 1 - jnp.arange(chunk)
    st_map = lambda ab: jnp.einsum("sGgpc,gh->Gsgchp", split(ab[rev], 1), eye).reshape(N_SUPER, width, n_state)
    wst = jnp.concatenate([st_map(ab_r), st_map(ab_i)], axis=2)
    ar, ai = pw_r[1:, :, None, :], pw_i[1:, :, None, :]
    co_r = c_re[None] * ar - c_im[None] * ai
    co_i = -c_re[None] * ai - c_im[None] * ar
    out_map = lambda co: jnp.einsum("tGgop,gh->Ggptho", split(co, 1), eye).reshape(N_SUPER, n_state, width)
    wout = jnp.concatenate([out_map(co_r), out_map(co_i)], axis=1)
    last_r = pw_r[chunk].reshape(N_SUPER, 1, n_state)
    last_i = pw_i[chunk].reshape(N_SUPER, 1, n_state)
    a1 = jnp.concatenate([last_r, last_r], axis=2)
    a2 = jnp.concatenate([-last_i, last_i], axis=2)
    return toep.astype(BF16), wst.astype(BF16), wout.astype(BF16), a1, a2


def s5_prompt(mid, weights):
    toep, wst, wout, a1, a2 = weights
    n_chunk = SEQ // S5_CHUNK
    n_state2 = 2 * S5_PACK * S5_STATE
    width = S5_CHUNK * LANES
    seq_blk = pl.BlockSpec((SEQ, LANES), lambda g, b: (b, g))
    per_g = lambda shape: pl.BlockSpec((1,) + shape, lambda g, b: (g, 0, 0))
    y, h_last = _call(
        _s5_prompt_body, grid=(N_SUPER, BATCH),
        in_specs=[seq_blk, per_g((width, width)), per_g((width, n_state2)), per_g((n_state2, width)),
                  per_g((1, n_state2)), per_g((1, n_state2))],
        out_specs=[seq_blk, pl.BlockSpec((1, 1, 1, n_state2), lambda g, b: (b, g, 0, 0))],
        out_shape=[jax.ShapeDtypeStruct((M_PROMPT, S5_WIDTH), F32),
                   jax.ShapeDtypeStruct((BATCH, N_SUPER, 1, n_state2), F32)],
        scratch_shapes=[pltpu.VMEM((n_chunk, n_state2), F32), pltpu.VMEM((n_chunk, n_state2), F32)],
        name="s5_prompt")(mid, toep, wst, wout, a1, a2)
    h = h_last.reshape(BATCH, N_SUPER, 2, S5_PACK, S5_STATE).transpose(0, 1, 3, 4, 2)
    return y, h.reshape(BATCH, S5_GROUPS, S5_STATE, 2)


def _s5_post_body(y_ref, u_ref, d_ref, w_ref, b_ref, o_ref):
    z = _gelu(y_ref[...] + d_ref[...] * u_ref[...])
    lin = _dot(z.astype(BF16), w_ref[...].astype(BF16))
    o_ref[...] = (z * _sigmoid(lin + b_ref[...])).astype(o_ref.dtype)


def s5_post(y, mid, d_skip, w_glu, b_glu, tm):
    m = y.shape[0]
    row = lambda i: (i, 0)
    fixed = lambda i: (0, 0)
    return _call(
        _s5_post_body, grid=(m // tm,),
        in_specs=[pl.BlockSpec((tm, S5_WIDTH), row), pl.BlockSpec((tm, S5_WIDTH), row),
                  pl.BlockSpec((1, S5_WIDTH), fixed), pl.BlockSpec((S5_WIDTH, S5_WIDTH), fixed),
                  pl.BlockSpec((1, S5_WIDTH), fixed)],
        out_specs=pl.BlockSpec((tm, S5_WIDTH), row),
        out_shape=jax.ShapeDtypeStruct((m, S5_WIDTH), BF16),
        name="s5_post")(y, mid, d_skip.reshape(1, -1), w_glu, b_glu.reshape(1, -1))


def _s5_decode_body(u_ref, hr_ref, hi_ref, ar_ref, ai_ref, bb_ref, cc_ref, y_ref, sr_ref, si_ref):
    n = hr_ref.shape[1]
    bu = _dot(u_ref[...].astype(BF16), bb_ref[...])
    ar, ai = ar_ref[...], ai_ref[...]
    hr, hi = hr_ref[...], hi_ref[...]
    sr = bu[:, :n] + (ar * hr - ai * hi)
    si = bu[:, n:] + (ar * hi + ai * hr)
    sr_ref[...] = sr
    si_ref[...] = si
    y_ref[...] = _dot(jnp.concatenate([sr, si], axis=1).astype(BF16), cc_ref[...])


def s5_decode(mid, state, lam_re, lam_im, log_dt, b_re, b_im, c_re, c_im):
    dt = jnp.exp(log_dt)[:, None]
    mag = jnp.exp(lam_re * dt)
    abar_r, abar_i = mag * jnp.cos(lam_im * dt), mag * jnp.sin(lam_im * dt)
    den = lam_re * lam_re + lam_im * lam_im
    fr = ((abar_r - 1.0) * lam_re + abar_i * lam_im) / den
    fi = (abar_i * lam_re - (abar_r - 1.0) * lam_im) / den
    bbr = fr[..., None] * b_re - fi[..., None] * b_im
    bbi = fr[..., None] * b_im + fi[..., None] * b_re
    eye = jnp.eye(S5_GROUPS, dtype=F32)
    n_state = S5_GROUPS * S5_STATE
    in_map = lambda w: jnp.einsum("gpc,gk->gckp", w, eye).reshape(S5_WIDTH, n_state)
    out_map = lambda w: jnp.einsum("gop,gk->gpko", w, eye).reshape(n_state, S5_WIDTH)
    bb = jnp.concatenate([in_map(bbr), in_map(bbi)], axis=1).astype(BF16)
    cc = jnp.concatenate([out_map(c_re), -out_map(c_im)], axis=0).astype(BF16)
    nb = DEC_BATCH
    fixed = lambda i: (0, 0)
    st = pl.BlockSpec((nb, n_state), fixed)
    vec = pl.BlockSpec((1, n_state), fixed)
    y, sr, si = _call(
        _s5_decode_body, grid=(1,),
        in_specs=[pl.BlockSpec((nb, S5_WIDTH), fixed), st, st, vec, vec,
                  pl.BlockSpec((S5_WIDTH, 2 * n_state), fixed), pl.BlockSpec((2 * n_state, S5_WIDTH), fixed)],
        out_specs=[pl.BlockSpec((nb, S5_WIDTH), fixed), st, st],
        out_shape=[jax.ShapeDtypeStruct((nb, S5_WIDTH), F32), jax.ShapeDtypeStruct((nb, n_state), F32),
                   jax.ShapeDtypeStruct((nb, n_state), F32)],
        name="s5_decode")(mid, state[..., 0].reshape(nb, n_state), state[..., 1].reshape(nb, n_state),
                          abar_r.reshape(1, n_state), abar_i.reshape(1, n_state), bb, cc)
    new_state = jnp.stack([sr.reshape(nb, S5_GROUPS, S5_STATE), si.reshape(nb, S5_GROUPS, S5_STATE)], axis=-1)
    return y, new_state


def _lru_gates(xc, wa_ref, ba_ref, wi_ref, bi_ref, c8_ref):
    xb = xc.astype(BF16)
    r = _sigmoid(_dot(xb, wa_ref[...]) + ba_ref[...])
    ig = _sigmoid(_dot(xb, wi_ref[...]) + bi_ref[...])
    log_a = c8_ref[...] * r
    a = jnp.exp(log_a)
    b = jnp.sqrt(-jnp.tanh(log_a) * (a * a + 1.0)) * (ig * xc)
    return a, b


def _lru_pre_body(x_ref, p_ref, cw_ref, cb_ref, wa_ref, ba_ref, wi_ref, bi_ref, c8_ref,
                  a_ref, b_ref, ext_ref, *, tiles_per_seq):
    tm = x_ref.shape[0]
    first = pl.program_id(0) % tiles_per_seq == 0
    ext_ref[0:8, :] = jnp.where(first, 0.0, p_ref[...])
    ext_ref[8:, :] = x_ref[...]
    xc = cb_ref[...] + cw_ref[3:4, :] * x_ref[...]
    for k in (1, 2, 3):
        xc = xc + cw_ref[3 - k:4 - k, :] * ext_ref[8 - k:8 - k + tm, :]
    a, b = _lru_gates(xc, wa_ref, ba_ref, wi_ref, bi_ref, c8_ref)
    a_ref[...] = a
    b_ref[...] = b


def _lru_scan_body(a_ref, b_ref, g_ref, y_ref, hl_ref, h_ref, p_ref):
    t_len = a_ref.shape[0]
    seg_len = t_len // LRU_SEG
    n_q = LRU_SEG // 8

    def step(t, carry):
        out = []
        for qi in range(n_q):
            h, p = carry[2 * qi], carry[2 * qi + 1]
            rows = pl.ds(qi * 8 * seg_len + t, 8, stride=seg_len)
            at = a_ref[rows, :]
            h = at * h + b_ref[rows, :]
            p = at * p
            h_ref[rows, :] = h
            p_ref[rows, :] = p
            out += [h, p]
        return tuple(out)

    init = (jnp.zeros((8, LANES), F32), jnp.ones((8, LANES), F32)) * n_q
    lax.fori_loop(0, seg_len, step, init)
    carry = jnp.zeros((1, LANES), F32)
    for s in range(LRU_SEG):
        rows = slice(s * seg_len, (s + 1) * seg_len)
        h = h_ref[rows, :] + p_ref[rows, :] * carry
        y_ref[rows, :] = (h * _gelu(g_ref[rows, :])).astype(y_ref.dtype)
        carry = h[seg_len - 1:seg_len, :]
    hl_ref[0] = carry


def lru_weights(conv_w, conv_b, w_a, b_a, w_i, b_i, lam):
    eye = jnp.eye(LRU_HEADS, dtype=F32)
    bd = lambda w: jnp.einsum("hij,hk->hikj", w, eye).reshape(LRU_WIDTH, LRU_WIDTH).astype(BF16)
    c8 = (-LRU_C) * jax.nn.softplus(-lam)
    return (conv_w, conv_b.reshape(1, -1), bd(w_a), b_a.reshape(1, -1), bd(w_i), b_i.reshape(1, -1),
            c8.reshape(1, -1))


def lru_prompt(mid, lw, tm):
    conv_w, conv_b, wa, ba, wi, bi, c8 = lw
    fixed = lambda i: (0, 0)
    vec = pl.BlockSpec((1, LRU_WIDTH), fixed)
    mat = pl.BlockSpec((LRU_WIDTH, LRU_WIDTH), fixed)
    a, b = _call(
        functools.partial(_lru_pre_body, tiles_per_seq=SEQ // tm), grid=(M_PROMPT // tm,),
        in_specs=[pl.BlockSpec((tm, LRU_WIDTH), lambda i: (i, 1)),
                  pl.BlockSpec((8, LRU_WIDTH), lambda i: (jnp.maximum(i * (tm // 8) - 1, 0), 1)),
                  pl.BlockSpec((4, LRU_WIDTH), fixed), vec, mat, vec, mat, vec, vec],
        out_specs=[pl.BlockSpec((tm, LRU_WIDTH), lambda i: (i, 0))] * 2,
        out_shape=[jax.ShapeDtypeStruct((M_PROMPT, LRU_WIDTH), F32)] * 2,
        scratch_shapes=[pltpu.VMEM((tm + 8, LRU_WIDTH), F32)],
        name="lru_pre")(mid, mid, conv_w, conv_b, wa, ba, wi, bi, c8)
    n_lane = LRU_WIDTH // LANES
    g_col0 = (S5_WIDTH + LRU_WIDTH) // LANES
    blk = lambda off: pl.BlockSpec((SEQ, LANES), lambda bi_, c: (bi_, off + c))
    y, h_last = _call(
        _lru_scan_body, grid=(BATCH, n_lane),
        in_specs=[blk(0), blk(0), blk(g_col0)],
        out_specs=[blk(0), pl.BlockSpec((1, 1, LANES), lambda bi_, c: (bi_, 0, c))],
        out_shape=[jax.ShapeDtypeStruct((M_PROMPT, LRU_WIDTH), BF16),
                   jax.ShapeDtypeStruct((BATCH, 1, LRU_WIDTH), F32)],
        scratch_shapes=[pltpu.VMEM((SEQ, LANES), F32), pltpu.VMEM((SEQ, LANES), F32)],
        name="lru_scan")(a, b, mid)
    return y, h_last.reshape(BATCH, LRU_WIDTH)


def _lru_decode_body(x_ref, g_ref, c0_ref, c1_ref, c2_ref, h0_ref, cw_ref, cb_ref, wa_ref, ba_ref,
                     wi_ref, bi_ref, c8_ref, y_ref, h_ref):
    xc = (cb_ref[...] + cw_ref[0:1, :] * c0_ref[...] + cw_ref[1:2, :] * c1_ref[...]
          + cw_ref[2:3, :] * c2_ref[...] + cw_ref[3:4, :] * x_ref[...])
    a, b = _lru_gates(xc, wa_ref, ba_ref, wi_ref, bi_ref, c8_ref)
    h = a * h0_ref[...] + b
    h_ref[...] = h
    y_ref[...] = (h * _gelu(g_ref[...])).astype(y_ref.dtype)


def lru_decode(mid, conv_buf, h0, lw):
    conv_w, conv_b, wa, ba, wi, bi, c8 = lw
    nb = DEC_BATCH
    fixed = lambda i: (0, 0)
    vec = pl.BlockSpec((1, LRU_WIDTH), fixed)
    mat = pl.BlockSpec((LRU_WIDTH, LRU_WIDTH), fixed)
    st = pl.BlockSpec((nb, LRU_WIDTH), fixed)
    return _call(
        _lru_decode_body, grid=(1,),
        in_specs=[pl.BlockSpec((nb, LRU_WIDTH), lambda i: (0, 1)), pl.BlockSpec((nb, LRU_WIDTH), lambda i: (0, 2)),
                  st, st, st, st, pl.BlockSpec((4, LRU_WIDTH), fixed), vec, mat, vec, mat, vec, vec],
        out_specs=[st, st],
        out_shape=[jax.ShapeDtypeStruct((nb, LRU_WIDTH), BF16), jax.ShapeDtypeStruct((nb, LRU_WIDTH), F32)],
        name="lru_decode")(mid, mid, conv_buf[:, 0], conv_buf[:, 1], conv_buf[:, 2], h0,
                           conv_w, conv_b, wa, ba, wi, bi, c8)


TM = 1024
TM_SMALL = 512
TM_EXPERT = 512
M_ALL = M_PROMPT + DEC_BATCH
N_EXPERT_TILES = (2 * M_ALL) // TM_EXPERT + N_EXPERTS
DEC_ATT_BS = (16, 8, 2)


def _token_mixers(layer, xp, xs, p, caches_t, win_prev, state_s5, state_conv, state_lru):
    n_p = rmsnorm(xp, p["norm_mix"][layer], TM_SMALL, BF16)
    n_s = rmsnorm(xs, p["norm_mix"][layer], DEC_BATCH, BF16)
    w_in = p["w_in"]
    proj = functools.partial(matmul_ws, [n_p], [n_s], w_of_x=[0], tm=TM)
    qkv_p, qkv_s = proj([(w_in, (layer,))], n_cols=OFF_MID, tn=768, out_dtype=F32, epilogue=_ep_plain,
                        name="proj_qkv")
    mid_p, mid_s = proj([(w_in[layer, :, OFF_MID:OFF_GATE], ())], tn=640, out_dtype=F32, epilogue=_ep_plain,
                        name="proj_mid")
    gates_p, gates_s = proj([(w_in[layer, :, OFF_GATE:], ())], tn=1024, out_dtype=BF16, out_dtype_s=F32,
                            epilogue=_ep_sigmoid, name="proj_gates")

    qn_p, kn_p = qk_norm(qkv_p, p["q_gain"][layer], p["k_gain"][layer], TM_SMALL, F32)
    qn_s, kn_s = qk_norm(qkv_s, p["q_gain"][layer], p["k_gain"][layer], DEC_BATCH, F32)
    outs, lses, outs_s, lses_s = [], [], [], []
    new_p, win = [], []
    for g, (window, dil) in enumerate(ATT_GROUPS):
        cols = slice(GROUP_W * g, GROUP_W * (g + 1))
        v_cols = slice(2 * ATT_WIDTH + GROUP_W * g, 2 * ATT_WIDTH + GROUP_W * (g + 1))
        o, lse = prompt_attention(g, qn_p, kn_p, qkv_p)
        outs.append(o)
        lses.append(lse)
        keep = min(window, SEQ)
        tail = lambda x, cs: (x.reshape(BATCH, SEQ, x.shape[1])[:, SEQ - keep:, cs]
                              .reshape(BATCH, keep, HEADS, HEAD_DIM))
        new_p.append(tail(kn_p, cols))
        new_p.append(tail(qkv_p, v_cols))
        cols_s = jnp.concatenate([qn_s[:, cols], kn_s[:, cols], qkv_s[:, v_cols]], axis=1).T
        prev_k, prev_v = (None, None) if win_prev is None else (win_prev[2 * g], win_prev[2 * g + 1])
        o_s, lse_s, wk, wv = decode_attention(layer, g, cols_s, caches_t[2 * g], caches_t[2 * g + 1],
                                              prev_k, prev_v, DEC_ATT_BS[g])
        outs_s.append(o_s.reshape(DEC_BATCH, GROUP_W))
        lses_s.append(lse_s.reshape(DEC_BATCH, GROUP_W))
        win += [wk, wv]
    att_p = mix_groups(outs, lses, TM_SMALL, BF16)
    att_s = mix_groups(outs_s, lses_s, DEC_BATCH, BF16)

    s5p = (p["s5_lam_re"][layer], p["s5_lam_im"][layer], p["s5_log_dt"][layer], p["s5_b_re"][layer],
           p["s5_b_im"][layer], p["s5_c_re"][layer], p["s5_c_im"][layer])
    y_p, h_p = s5_prompt(mid_p, s5_prompt_weights(*s5p))
    y_s, h_s = s5_decode(mid_s, state_s5[layer], *s5p)
    glu = (p["s5_d"][layer], p["s5_w_glu"][layer], p["s5_b_glu"][layer])
    s5_p = s5_post(y_p, mid_p, *glu, TM_SMALL)
    s5_s = s5_post(y_s, mid_s, *glu, DEC_BATCH)
    new_p.append(h_p)
    new_s = [h_s]

    lw = lru_weights(p["lru_conv_w"][layer], p["lru_conv_b"][layer], p["lru_w_a"][layer], p["lru_b_a"][layer],
                     p["lru_w_i"][layer], p["lru_b_i"][layer], p["lru_lam"][layer])
    lru_p, h_lp = lru_prompt(mid_p, lw, 512)
    lru_s, h_ls = lru_decode(mid_s, state_conv[layer], state_lru[layer], lw)
    x_lru = slice(S5_WIDTH, S5_WIDTH + LRU_WIDTH)
    new_p.append(mid_p[:, x_lru].reshape(BATCH, SEQ, LRU_WIDTH)[:, SEQ - 3:])
    new_p.append(h_lp)
    new_s.append(jnp.concatenate([state_conv[layer, :, 1:], mid_s[:, None, x_lru]], axis=1))
    new_s.append(h_ls)

    tn = 512
    gate_tiles = D_MODEL // tn
    merged_p, merged_s = matmul_ws(
        [att_p, s5_p, lru_p], [att_s, s5_s, lru_s],
        [(p["w_br_attn"], (layer,)), (p["w_br_s5"], (layer,)), (p["w_br_lru"], (layer,))],
        w_of_x=[0, 1, 2], tm=TM, tn=tn, out_dtype=BF16, epilogue=_ep_merge,
        tile_extras=[(gates_p, gates_s, 0), (gates_p, gates_s, gate_tiles), (gates_p, gates_s, 2 * gate_tiles)],
        name="merge")
    xp, xs = matmul_ws([merged_p], [merged_s], [(p["w_o"], (layer,))], w_of_x=[0], tm=TM, tn=512,
                       out_dtype=F32, epilogue=_ep_residual, tile_extras=[(xp, xs, 0)], name="out_proj")
    return xp, xs, new_p, new_s, win


def _dense_ffn(xp, xs, p, layer, j):
    n_p = rmsnorm(xp, p["norm_ffn"][layer], TM_SMALL, BF16)
    n_s = rmsnorm(xs, p["norm_ffn"][layer], DEC_BATCH, BF16)
    h_p, h_s = matmul_ws([n_p], [n_s], [(p["ffn_w_gate"], (j,)), (p["ffn_w_up"], (j,))], w_of_x=[0, 0],
                         tm=TM, tn=512, out_dtype=BF16, epilogue=_ep_swiglu, name="ffn_up")
    return matmul_ws([h_p], [h_s], [(p["ffn_w_down"], (j,))], w_of_x=[0], tm=512, tn=512, out_dtype=F32,
                     epilogue=_ep_residual, tile_extras=[(xp, xs, 0)], name="ffn_down")


def _moe_ffn(xp, xs, p, layer, j):
    router = (p["norm_ffn"][layer], p["moe_router_w"][j], p["moe_router_b"][j])
    n_p, comb_p, idx_p = rmsnorm_router(xp, *router, TM_SMALL)
    n_s, comb_s, idx_s = rmsnorm_router(xs, *router, DEC_BATCH)
    n = jnp.concatenate([n_p, n_s], axis=0)
    comb = jnp.concatenate([comb_p, comb_s], axis=0)
    idx = jnp.concatenate([idx_p, idx_s], axis=0)[:, :2]
    m = M_ALL
    top_e = idx.reshape(-1)
    top_w = jnp.take_along_axis(comb, idx, axis=1).reshape(-1)
    counts = jnp.sum(top_e[:, None] == jnp.arange(N_EXPERTS)[None, :], axis=0).astype(jnp.int32)
    padded = ((counts + TM_EXPERT - 1) // TM_EXPERT) * TM_EXPERT
    pad_end = jnp.cumsum(padded)
    pad_start = pad_end - padded
    start = jnp.cumsum(counts) - counts
    order = jnp.argsort(top_e, stable=True).astype(jnp.int32)
    rank = jnp.argsort(order).astype(jnp.int32)
    dest = pad_start[top_e] + rank - start[top_e]
    n_rows = N_EXPERT_TILES * TM_EXPERT
    tile_start = jnp.arange(N_EXPERT_TILES, dtype=jnp.int32) * TM_EXPERT
    tile_expert = jnp.minimum(jnp.sum(tile_start[:, None] >= pad_end[None, :], axis=1), N_EXPERTS - 1)
    tile_expert = tile_expert.astype(jnp.int32)
    n_used = (pad_end[-1] // TM_EXPERT).reshape(1).astype(jnp.int32)
    row_e = jnp.repeat(tile_expert, TM_EXPERT)
    within = jnp.arange(n_rows, dtype=jnp.int32) - pad_start[row_e]
    row_valid = within < counts[row_e]
    row_pair = order[jnp.clip(start[row_e] + within, 0, 2 * m - 1)]
    src_token = jnp.where(row_valid, row_pair // 2, 0)
    row_w = jnp.where(row_valid, top_w[row_pair], 0.0)
    x_sorted = jnp.take(n, src_token, axis=0, mode="clip")
    gmm = functools.partial(grouped_matmul_ws, tile_expert, n_used, lead=(j,), tm=TM_EXPERT)
    gate = gmm(x_sorted, p["moe_w_gate"], tn=1408, out_dtype=BF16, epilogue=_ep_plain, name="moe_gate")
    h = gmm(x_sorted, p["moe_w_up"], tn=1408, out_dtype=BF16, epilogue=_ep_silu_times, tile_extras=[gate],
            name="moe_up")
    ys = gmm(h, p["moe_w_down"], tn=1024, out_dtype=F32, epilogue=_ep_scale_rows,
             col_extras=[row_w.reshape(-1, 1)], name="moe_down")
    dest2 = dest.reshape(m, 2)
    y = jnp.take(ys, dest2[:, 0], axis=0, mode="clip") + jnp.take(ys, dest2[:, 1], axis=0, mode="clip")
    return xp + y[:M_PROMPT], xs + y[M_PROMPT:]


def kernel(x_prompt, x_sample, cache_k_w128, cache_v_w128, cache_k_w512, cache_v_w512, cache_k_w2048, cache_v_w2048, state_s5, state_conv, state_lru, norm_mix, w_in, q_gain, k_gain, s5_lam_re, s5_lam_im, s5_log_dt, s5_b_re, s5_b_im, s5_c_re, s5_c_im, s5_d, s5_w_glu, s5_b_glu, lru_conv_w, lru_conv_b, lru_w_a, lru_b_a, lru_w_i, lru_b_i, lru_lam, w_br_attn, w_br_s5, w_br_lru, w_o, norm_ffn, ffn_w_gate, ffn_w_up, ffn_w_down, moe_router_w, moe_router_b, moe_w_gate, moe_w_up, moe_w_down):
    p = dict(norm_mix=norm_mix, w_in=w_in, q_gain=q_gain, k_gain=k_gain, s5_lam_re=s5_lam_re,
             s5_lam_im=s5_lam_im, s5_log_dt=s5_log_dt, s5_b_re=s5_b_re, s5_b_im=s5_b_im, s5_c_re=s5_c_re,
             s5_c_im=s5_c_im, s5_d=s5_d, s5_w_glu=s5_w_glu, s5_b_glu=s5_b_glu, lru_conv_w=lru_conv_w,
             lru_conv_b=lru_conv_b, lru_w_a=lru_w_a, lru_b_a=lru_b_a, lru_w_i=lru_w_i, lru_b_i=lru_b_i,
             lru_lam=lru_lam, w_br_attn=w_br_attn, w_br_s5=w_br_s5, w_br_lru=w_br_lru, w_o=w_o,
             norm_ffn=norm_ffn, ffn_w_gate=ffn_w_gate, ffn_w_up=ffn_w_up, ffn_w_down=ffn_w_down,
             moe_router_w=moe_router_w, moe_router_b=moe_router_b, moe_w_gate=moe_w_gate,
             moe_w_up=moe_w_up, moe_w_down=moe_w_down)
    caches_t = [jnp.transpose(c, (0, 1, 3, 4, 2)) for c in
                (cache_k_w128, cache_v_w128, cache_k_w512, cache_v_w512, cache_k_w2048, cache_v_w2048)]
    xp = x_prompt.reshape(M_PROMPT, D_MODEL)
    xs = x_sample.reshape(DEC_BATCH, D_MODEL)
    states_p = [[] for _ in range(9)]
    states_s = [[] for _ in range(3)]
    win = None
    for layer in range(DEPTH):
        xp, xs, new_p, new_s, win = _token_mixers(layer, xp, xs, p, caches_t, win, state_s5, state_conv,
                                                  state_lru)
        if layer % 2 == 0:
            xp, xs = _dense_ffn(xp, xs, p, layer, layer // 2)
        else:
            xp, xs = _moe_ffn(xp, xs, p, layer, layer // 2)
        for i in range(9):
            states_p[i].append(new_p[i])
        for i in range(3):
            states_s[i].append(new_s[i])
    y_prompt = xp.reshape(BATCH, SEQ, D_MODEL)
    y_sample = xs.reshape(DEC_BATCH, 1, D_MODEL)
    windows = [jnp.transpose(w, (0, 1, 4, 2, 3)) for w in win]
    return (y_prompt, y_sample, *[jnp.stack(a) for a in states_p], *windows,
            *[jnp.stack(a) for a in states_s])
```

```python
import functools
import math

import jax
import jax.numpy as jnp
from jax import lax
from jax.experimental import pallas as pl
from jax.experimental.pallas import tpu as pltpu

F32 = jnp.float32
BF16 = jnp.bfloat16

D_MODEL = 2048
BATCH = 2
SEQ = 4096
DEPTH = 2
DEC_BATCH = 128
HEAD_DIM = 64
HEADS = 4
GROUP_W = HEADS * HEAD_DIM
ATT_GROUPS = ((128, 1), (512, 4), (2048, 16))
N_GROUPS = 3
ATT_WIDTH = N_GROUPS * GROUP_W
N_KEYS = 128
S5_WIDTH = 640
S5_GROUPS = 40
S5_CH = 16
S5_STATE = 64
LRU_WIDTH = 640
LRU_HEADS = 8
LRU_C = 8.0
OFF_MID = 3 * ATT_WIDTH
MID_WIDTH = S5_WIDTH + 2 * LRU_WIDTH
OFF_GATE = OFF_MID + MID_WIDTH
N_EXPERTS = 8
D_FF_EXPERT = 2816
EPS = 1e-6
LANES = 128

M_PROMPT = BATCH * SEQ

S5_CHUNK = 8
S5_PACK = LANES // S5_CH
N_SUPER = S5_GROUPS // S5_PACK
LRU_SEG = 32
NEG_BIG = -1e30

VMEM_LIMIT_BYTES = 56 * 1024 * 1024


def _call(body, *, grid, in_specs, out_specs, out_shape, name, scratch_shapes=(), num_scalar_prefetch=0,
          input_output_aliases=None):
    params = pltpu.CompilerParams(dimension_semantics=("arbitrary",) * len(grid),
                                  vmem_limit_bytes=VMEM_LIMIT_BYTES)
    kwargs = {}
    if input_output_aliases:
        kwargs["input_output_aliases"] = input_output_aliases
    if num_scalar_prefetch:
        grid_spec = pltpu.PrefetchScalarGridSpec(
            num_scalar_prefetch=num_scalar_prefetch, grid=grid, in_specs=in_specs,
            out_specs=out_specs, scratch_shapes=scratch_shapes)
        return pl.pallas_call(body, grid_spec=grid_spec, out_shape=out_shape,
                              compiler_params=params, name=name, **kwargs)
    return pl.pallas_call(body, grid=grid, in_specs=in_specs, out_specs=out_specs,
                          out_shape=out_shape, scratch_shapes=scratch_shapes,
                          compiler_params=params, name=name, **kwargs)


def _dot(a, b):
    return jnp.dot(a, b, preferred_element_type=F32)


def _dot_sel(x, sel, parts=2):
    acc = None
    r = x
    for _ in range(parts):
        piece = r.astype(BF16)
        r = r - piece.astype(F32)
        d = _dot(piece, sel)
        acc = d if acc is None else acc + d
    return acc


def _sigmoid(x):
    return 1.0 / (1.0 + jnp.exp(-x))


def _gelu(x):
    c = math.sqrt(2.0 / math.pi)
    return 0.5 * x * (1.0 + jnp.tanh(c * (x + 0.044715 * (x * x * x))))


def _rmsnorm_body(x_ref, g_ref, o_ref):
    x = x_ref[...]
    y = x * lax.rsqrt(jnp.mean(x * x, axis=-1, keepdims=True) + EPS)
    o_ref[...] = (y * g_ref[...]).astype(o_ref.dtype)


def _rmsnorm_router_body(x_ref, g_ref, rw_ref, rb_ref, o_ref, comb_ref, idx_ref):
    x = x_ref[...]
    y = x * lax.rsqrt(jnp.mean(x * x, axis=-1, keepdims=True) + EPS)
    xn = y * g_ref[...]
    o_ref[...] = xn.astype(o_ref.dtype)
    logits = _dot(xn.astype(BF16), rw_ref[...].astype(BF16)) + rb_ref[...]
    lane = lax.broadcasted_iota(jnp.int32, logits.shape, 1).astype(F32)
    m1 = jnp.max(logits, axis=1, keepdims=True)
    i1 = jnp.min(jnp.where(logits == m1, lane, float(LANES)), axis=1, keepdims=True)
    rest = jnp.where(lane == i1, -jnp.inf, logits)
    m2 = jnp.max(rest, axis=1, keepdims=True)
    i2 = jnp.min(jnp.where(rest == m2, lane, float(LANES)), axis=1, keepdims=True)
    e = jnp.exp(m2 - m1)
    w1 = 1.0 / (1.0 + e)
    w2 = e / (1.0 + e)
    comb_ref[...] = jnp.where(lane == i1, w1, 0.0) + jnp.where(lane == i2, w2, 0.0)
    idx_ref[...] = jnp.where(lane == 0.0, i1, jnp.where(lane == 1.0, i2, 0.0)).astype(jnp.int32)


def rmsnorm(x, gain, tm, out_dtype):
    m, d = x.shape
    return _call(
        _rmsnorm_body, grid=(m // tm,),
        in_specs=[pl.BlockSpec((tm, d), lambda i: (i, 0)), pl.BlockSpec((1, d), lambda i: (0, 0))],
        out_specs=pl.BlockSpec((tm, d), lambda i: (i, 0)),
        out_shape=jax.ShapeDtypeStruct((m, d), out_dtype), name="rmsnorm")(x, gain.reshape(1, d))


def rmsnorm_router(x, gain, router_w, router_b, tm):
    m, d = x.shape
    ne = router_w.shape[1]
    rw = jnp.zeros((d, LANES), F32).at[:, :ne].set(router_w)
    rb = jnp.full((1, LANES), NEG_BIG, F32).at[0, :ne].set(router_b)
    row = lambda i: (i, 0)
    fixed = lambda i: (0, 0)
    return _call(
        _rmsnorm_router_body, grid=(m // tm,),
        in_specs=[pl.BlockSpec((tm, d), row), pl.BlockSpec((1, d), fixed),
                  pl.BlockSpec((d, LANES), fixed), pl.BlockSpec((1, LANES), fixed)],
        out_specs=[pl.BlockSpec((tm, d), row), pl.BlockSpec((tm, LANES), row), pl.BlockSpec((tm, LANES), row)],
        out_shape=[jax.ShapeDtypeStruct((m, d), F32), jax.ShapeDtypeStruct((m, LANES), F32),
                   jax.ShapeDtypeStruct((m, LANES), jnp.int32)],
        name="rmsnorm_router")(x, gain.reshape(1, d), rw, rb)


def _mm_body(*refs, n_x, w_of_x, n_te, n_re, epilogue, n_ptiles):
    n_w = len(w_of_x)
    pos = 0
    xp_refs = refs[pos:pos + n_x]; pos += n_x
    xs_refs = refs[pos:pos + n_x]; pos += n_x
    w_refs = refs[pos:pos + n_w]; pos += n_w
    tep_refs = refs[pos:pos + n_te]; pos += n_te
    tes_refs = refs[pos:pos + n_te]; pos += n_te
    re_refs = refs[pos:pos + n_re]; pos += n_re
    op_ref, os_ref = refs[pos], refs[pos + 1]; pos += 2
    wb_refs = refs[pos:pos + n_w]
    i = pl.program_id(1)

    @pl.when(i == 0)
    def _():
        for w_ref, wb_ref in zip(w_refs, wb_refs):
            wb_ref[...] = w_ref[...].astype(BF16)

    def rows(x_refs, te_refs, o_ref):
        accs = [_dot(x_refs[xi][...].astype(BF16), wb_ref[...]) for xi, wb_ref in zip(w_of_x, wb_refs)]
        extras = [e[...] for e in te_refs] + [e[...] for e in re_refs]
        o_ref[...] = epilogue(accs, extras).astype(o_ref.dtype)

    @pl.when(i < n_ptiles)
    def _():
        rows(xp_refs, tep_refs, op_ref)

    @pl.when(i == n_ptiles)
    def _():
        rows(xs_refs, tes_refs, os_ref)


def matmul_ws(xps, xss, ws, *, w_of_x, tm, tn, out_dtype, epilogue, tile_extras=(), row_extras=(), name,
              out_dtype_s=None, n_cols=None):
    mp = xps[0].shape[0]
    ms = xss[0].shape[0]
    n = n_cols or ws[0][0].shape[-1]
    n_ptiles = mp // tm
    last = n_ptiles - 1
    grid = (n // tn, n_ptiles + 1)
    in_specs = [pl.BlockSpec((tm, x.shape[1]), lambda j, i: (jnp.minimum(i, last), 0)) for x in xps]
    in_specs += [pl.BlockSpec((ms, x.shape[1]), lambda j, i: (0, 0)) for x in xss]
    scratch = []
    for w, lead in ws:
        k = w.shape[-2]
        block = (None,) * len(lead) + (k, tn)
        in_specs.append(pl.BlockSpec(block, lambda j, i, lead=lead: lead + (0, j)))
        scratch.append(pltpu.VMEM((k, tn), BF16))
    for _, _, off in tile_extras:
        in_specs.append(pl.BlockSpec((tm, tn), lambda j, i, off=off: (jnp.minimum(i, last), off + j)))
    for _, _, off in tile_extras:
        in_specs.append(pl.BlockSpec((ms, tn), lambda j, i, off=off: (0, off + j)))
    for _ in row_extras:
        in_specs.append(pl.BlockSpec((1, tn), lambda j, i: (0, j)))
    body = functools.partial(_mm_body, n_x=len(xps), w_of_x=tuple(w_of_x), n_te=len(tile_extras),
                             n_re=len(row_extras), epilogue=epilogue, n_ptiles=n_ptiles)
    return _call(
        body, grid=grid, in_specs=in_specs,
        out_specs=[pl.BlockSpec((tm, tn), lambda j, i: (jnp.minimum(i, last), j)),
                   pl.BlockSpec((ms, tn), lambda j, i: (0, j))],
        out_shape=[jax.ShapeDtypeStruct((mp, n), out_dtype),
                   jax.ShapeDtypeStruct((ms, n), out_dtype_s or out_dtype)],
        scratch_shapes=scratch, name=name,
    )(*xps, *xss, *[w for w, _ in ws], *[a for a, _, _ in tile_extras], *[a for _, a, _ in tile_extras],
      *row_extras)


def _ep_plain(accs, extras):
    return accs[0]


def _ep_sigmoid(accs, extras):
    return _sigmoid(accs[0])


def _ep_residual(accs, extras):
    return extras[0] + accs[0]


def _ep_swiglu(accs, extras):
    g, u = accs
    return (g * _sigmoid(g)) * u


def _ep_merge(accs, extras):
    return (extras[0].astype(F32) * accs[0] + extras[1].astype(F32) * accs[1]
            + extras[2].astype(F32) * accs[2])


def _gmm_body(te_ref, nu_ref, *refs, n_te, n_ce, epilogue):
    x_ref, w_ref = refs[0], refs[1]
    extra_refs = refs[2:2 + n_te + n_ce]
    o_ref = refs[2 + n_te + n_ce]
    wb_ref = refs[3 + n_te + n_ce]
    i = pl.program_id(1)
    new_expert = jnp.logical_or(i == 0, te_ref[i] != te_ref[jnp.maximum(i - 1, 0)])

    @pl.when(new_expert)
    def _():
        wb_ref[...] = w_ref[...].astype(BF16)

    @pl.when(i < nu_ref[0])
    def _():
        acc = _dot(x_ref[...].astype(BF16), wb_ref[...])
        o_ref[...] = epilogue([acc], [e[...] for e in extra_refs]).astype(o_ref.dtype)

    @pl.when(i >= nu_ref[0])
    def _():
        o_ref[...] = jnp.zeros(o_ref.shape, o_ref.dtype)


def grouped_matmul_ws(tile_expert, n_used, x, w, lead, *, tm, tn, out_dtype, epilogue, tile_extras=(),
                      col_extras=(), name):
    m = x.shape[0]
    k, n = w.shape[-2:]
    grid = (n // tn, m // tm)
    block = (None,) * (len(lead) + 1) + (k, tn)
    in_specs = [pl.BlockSpec((tm, x.shape[1]), lambda j, i, te, nu: (i, 0)),
                pl.BlockSpec(block, lambda j, i, te, nu: lead + (te[i], 0, j))]
    in_specs += [pl.BlockSpec((tm, tn), lambda j, i, te, nu: (i, j)) for _ in tile_extras]
    in_specs += [pl.BlockSpec((tm, 1), lambda j, i, te, nu: (i, 0)) for _ in col_extras]
    body = functools.partial(_gmm_body, n_te=len(tile_extras), n_ce=len(col_extras), epilogue=epilogue)
    return _call(
        body, grid=grid, in_specs=in_specs,
        out_specs=pl.BlockSpec((tm, tn), lambda j, i, te, nu: (i, j)),
        out_shape=jax.ShapeDtypeStruct((m, n), out_dtype), scratch_shapes=[pltpu.VMEM((k, tn), BF16)],
        num_scalar_prefetch=2, name=name,
    )(tile_expert, n_used, x, w, *tile_extras, *col_extras)


def _ep_silu_times(accs, extras):
    g = extras[0].astype(F32)
    return (g * _sigmoid(g)) * accs[0]


def _ep_scale_rows(accs, extras):
    return extras[0] * accs[0]


def _qknorm_body(q_ref, k_ref, qg_ref, kg_ref, seg_ref, qo_ref, ko_ref):
    seg = seg_ref[...]
    for src, gain, dst, scale in ((q_ref, qg_ref, qo_ref, HEAD_DIM ** -0.5), (k_ref, kg_ref, ko_ref, 1.0)):
        for g in range(N_GROUPS):
            cols = slice(GROUP_W * g, GROUP_W * (g + 1))
            x = src[:, cols]
            ms = _dot_sel(x * x, seg, 3)
            y = x * lax.rsqrt(ms + EPS) * gain[:, cols]
            dst[:, cols] = (y * scale).astype(dst.dtype)


def qk_norm(qkv, q_gain, k_gain, tm, q_dtype):
    m = qkv.shape[0]
    head = jnp.arange(GROUP_W) // HEAD_DIM
    seg = jnp.where(head[:, None] == head[None, :], 1.0 / HEAD_DIM, 0.0).astype(BF16)
    qg = jnp.broadcast_to(q_gain[:, None, :], (N_GROUPS, HEADS, HEAD_DIM)).reshape(1, ATT_WIDTH)
    kg = jnp.broadcast_to(k_gain[:, None, :], (N_GROUPS, HEADS, HEAD_DIM)).reshape(1, ATT_WIDTH)
    fixed = lambda i: (0, 0)
    return _call(
        _qknorm_body, grid=(m // tm,),
        in_specs=[pl.BlockSpec((tm, ATT_WIDTH), lambda i: (i, 0)), pl.BlockSpec((tm, ATT_WIDTH), lambda i: (i, 1)),
                  pl.BlockSpec((1, ATT_WIDTH), fixed), pl.BlockSpec((1, ATT_WIDTH), fixed),
                  pl.BlockSpec((GROUP_W, GROUP_W), fixed)],
        out_specs=[pl.BlockSpec((tm, ATT_WIDTH), lambda i: (i, 0)), pl.BlockSpec((tm, ATT_WIDTH), lambda i: (i, 0))],
        out_shape=[jax.ShapeDtypeStruct((m, ATT_WIDTH), q_dtype), jax.ShapeDtypeStruct((m, ATT_WIDTH), F32)],
        name="qk_norm")(qkv, qkv, qg, kg, seg)


def _attn_body(*refs, dil):
    q_refs, kp_refs, kc_refs, vp_refs, vc_refs = (refs[2 * n:2 * n + 2] for n in range(5))
    o_ref, lse_ref = refs[10], refs[11]
    o_stage, lse_stage = refs[12:14], refs[14:16]
    a = pl.program_id(1)
    tq = N_KEYS
    r = lax.broadcasted_iota(jnp.int32, (HEADS * tq, 2 * tq), 0) % tq
    c = lax.broadcasted_iota(jnp.int32, (HEADS * tq, 2 * tq), 1)
    valid = (c >= r) & (c <= r + N_KEYS) & ((a > 0) | (c >= tq))
    head = lax.broadcasted_iota(jnp.int32, (1, GROUP_W), 1) // HEAD_DIM

    def residue(res, carry):
        rows = pl.ds(res, tq, stride=dil) if dil > 1 else slice(None)
        load = lambda halves: jnp.concatenate([halves[0][rows, :], halves[1][rows, :]], axis=1)
        q = load(q_refs)
        k2 = jnp.concatenate([load(kp_refs), load(kc_refs)], axis=0).astype(BF16)
        v2 = jnp.concatenate([load(vp_refs), load(vc_refs)], axis=0).astype(BF16)
        q4 = jnp.concatenate([jnp.where(head == h, q, 0.0) for h in range(HEADS)], axis=0).astype(BF16)
        s = lax.dot_general(q4, k2, (((1,), (1,)), ((), ())), preferred_element_type=F32)
        s = jnp.where(valid, s, NEG_BIG)
        m = jnp.max(s, axis=1, keepdims=True)
        p = jnp.exp(s - m)
        l = jnp.sum(p, axis=1, keepdims=True)
        o4 = _dot(p.astype(BF16), v2) / l
        lse4 = m + jnp.log(l)
        o = jnp.zeros((tq, GROUP_W), F32)
        lse = jnp.zeros((tq, GROUP_W), F32)
        for h in range(HEADS):
            blk = slice(h * tq, (h + 1) * tq)
            o = jnp.where(head == h, o4[blk, :], o)
            lse = jnp.where(head == h, lse4[blk, :], lse)
        for half in range(2):
            lanes = slice(LANES * half, LANES * (half + 1))
            o_stage[half][rows, :] = o[:, lanes]
            lse_stage[half][rows, :] = lse[:, lanes]
        return carry

    if dil == 1:
        residue(0, 0)
    else:
        lax.fori_loop(0, dil, residue, 0)
    for half in range(2):
        lanes = slice(LANES * half, LANES * (half + 1))
        o_ref[:, lanes] = o_stage[half][...]
        lse_ref[:, lanes] = lse_stage[half][...]


def prompt_attention(g, qn, kn, qkv):
    _, dil = ATT_GROUPS[g]
    span = N_KEYS * dil
    n_span = SEQ // span
    halves_per_group = GROUP_W // LANES
    cur = lambda col: (lambda b, a: (b * n_span + a, col))
    prev = lambda col: (lambda b, a: (b * n_span + jnp.maximum(a - 1, 0), col))
    blk = (span, LANES)
    qk_cols = [halves_per_group * g + half for half in range(halves_per_group)]
    v_cols = [halves_per_group * (2 * N_GROUPS + g) + half for half in range(halves_per_group)]
    in_specs = ([pl.BlockSpec(blk, cur(c)) for c in qk_cols] + [pl.BlockSpec(blk, prev(c)) for c in qk_cols]
                + [pl.BlockSpec(blk, cur(c)) for c in qk_cols] + [pl.BlockSpec(blk, prev(c)) for c in v_cols]
                + [pl.BlockSpec(blk, cur(c)) for c in v_cols])
    out = pl.BlockSpec((span, GROUP_W), lambda b, a: (b * n_span + a, 0))
    return _call(
        functools.partial(_attn_body, dil=dil), grid=(BATCH, n_span),
        in_specs=in_specs, out_specs=[out, out],
        out_shape=[jax.ShapeDtypeStruct((M_PROMPT, GROUP_W), F32)] * 2,
        scratch_shapes=[pltpu.VMEM((span, LANES), F32)] * 4,
        name="prompt_attention")(qn, qn, kn, kn, kn, kn, qkv, qkv, qkv, qkv)


def _dec_attn_body(*refs, dil, aliased):
    if aliased:
        qkv_ref, ck_ref, cv_ref, _, _, o_ref, lse_ref, nk_ref, nv_ref = refs
    else:
        qkv_ref, ck_ref, cv_ref, o_ref, lse_ref, nk_ref, nv_ref = refs
    bs = ck_ref.shape[1]
    n_chunk = ck_ref.shape[4] // LANES
    lane = lax.broadcasted_iota(jnp.int32, (1, LANES), 1)
    key_lane = (lane % dil) == 0
    not_last = lax.broadcasted_iota(jnp.int32, (HEAD_DIM, LANES), 1) < LANES - 1
    sample_row = lax.broadcasted_iota(jnp.int32, (qkv_ref.shape[1], LANES), 0)
    qkv_t = qkv_ref[...]
    for s in range(bs):
        pick = (sample_row == pl.program_id(0) * bs + s).astype(BF16)
        cols = _dot_sel(qkv_t, pick, 3)
        for h in range(HEADS):
            rows = slice(HEAD_DIM * h, HEAD_DIM * (h + 1))
            qb = cols[rows, :]
            kn = cols[GROUP_W + HEAD_DIM * h:GROUP_W + HEAD_DIM * (h + 1), :]
            vn = cols[2 * GROUP_W + HEAD_DIM * h:2 * GROUP_W + HEAD_DIM * (h + 1), :]
            chunk = lambda ref, c: ref[0, s, h, :, c * LANES:(c + 1) * LANES]
            scores = [jnp.where(key_lane, jnp.sum(chunk(ck_ref, c) * qb, axis=0, keepdims=True), NEG_BIG)
                      for c in range(n_chunk)]
            s_self = jnp.sum(kn * qb, axis=0, keepdims=True)
            m = s_self
            for sc in scores:
                m = jnp.maximum(m, jnp.max(sc, axis=1, keepdims=True))
            p_self = jnp.exp(s_self - m)
            l = p_self
            acc = None
            for c in range(n_chunk):
                p = jnp.exp(scores[c] - m)
                l = l + jnp.sum(p, axis=1, keepdims=True)
                pv = p * chunk(cv_ref, c)
                acc = pv if acc is None else acc + pv
            o = (jnp.sum(acc, axis=1, keepdims=True) + p_self[:, 0:1] * vn[:, 0:1]) / l[:, 0:1]
            o_ref[s, rows, :] = o
            lse_ref[s, rows, :] = jnp.broadcast_to(m[:, 0:1] + jnp.log(l[:, 0:1]), (HEAD_DIM, 1))
            for src, dst, new in ((ck_ref, nk_ref, kn), (cv_ref, nv_ref, vn)):
                rolled = [pltpu.roll(chunk(src, c), LANES - 1, axis=1) for c in range(n_chunk)] + [new]
                for c in range(n_chunk):
                    dst[0, s, h, :, c * LANES:(c + 1) * LANES] = jnp.where(not_last, rolled[c], rolled[c + 1])


def decode_attention(layer, g, qkv_cols, cache_k, cache_v, prev_k, prev_v, bs):
    window, dil = ATT_GROUPS[g]
    cblk = (1, bs, HEADS, HEAD_DIM, window)
    cmap = lambda i: (layer, i, 0, 0, 0)
    col = pl.BlockSpec((bs, GROUP_W, 1), lambda i: (i, 0, 0))
    in_specs = [pl.BlockSpec((3 * GROUP_W, DEC_BATCH), lambda i: (0, 0)),
                pl.BlockSpec(cblk, cmap), pl.BlockSpec(cblk, cmap)]
    args = [qkv_cols, cache_k, cache_v]
    aliases = None
    if prev_k is not None:
        in_specs += [pl.BlockSpec(memory_space=pl.ANY), pl.BlockSpec(memory_space=pl.ANY)]
        args += [prev_k, prev_v]
        aliases = {3: 2, 4: 3}
    return _call(
        functools.partial(_dec_attn_body, dil=dil, aliased=prev_k is not None), grid=(DEC_BATCH // bs,),
        in_specs=in_specs,
        out_specs=[col, col, pl.BlockSpec(cblk, cmap), pl.BlockSpec(cblk, cmap)],
        out_shape=[jax.ShapeDtypeStruct((DEC_BATCH, GROUP_W, 1), F32)] * 2
        + [jax.ShapeDtypeStruct(cache_k.shape, F32)] * 2,
        input_output_aliases=aliases, name="decode_attention")(*args)


def _mix_body(o0_ref, o1_ref, o2_ref, l0_ref, l1_ref, l2_ref, att_ref):
    l0, l1, l2 = l0_ref[...], l1_ref[...], l2_ref[...]
    m = jnp.maximum(jnp.maximum(l0, l1), l2)
    e0, e1, e2 = jnp.exp(l0 - m), jnp.exp(l1 - m), jnp.exp(l2 - m)
    num = e0 * o0_ref[...] + e1 * o1_ref[...] + e2 * o2_ref[...]
    att_ref[...] = (num / (e0 + e1 + e2)).astype(att_ref.dtype)


def mix_groups(outs, lses, tm, out_dtype):
    m = outs[0].shape[0]
    spec = pl.BlockSpec((tm, GROUP_W), lambda i: (i, 0))
    return _call(_mix_body, grid=(m // tm,), in_specs=[spec] * 6, out_specs=spec,
                 out_shape=jax.ShapeDtypeStruct((m, GROUP_W), out_dtype), name="mix_groups")(*outs, *lses)


def _s5_prompt_body(u_ref, mt_ref, ms_ref, mo_ref, et_ref, es_ref, a1_ref, a2_ref, y_ref, hl_ref,
                    s_ref, hp_ref, toep_ref, wst_ref, wout_ref):
    @pl.when(pl.program_id(1) == 0)
    def _():
        def expand(m_ref, e_ref, row_div, col_div, dst_ref):
            full = _dot(m_ref[0].astype(BF16), e_ref[...])
            row_g = (lax.broadcasted_iota(jnp.int32, full.shape, 0) // row_div) % S5_PACK
            col_g = (lax.broadcasted_iota(jnp.int32, full.shape, 1) // col_div) % S5_PACK
            dst_ref[...] = jnp.where(row_g == col_g, full, 0.0).astype(BF16)

        expand(mt_ref, et_ref, S5_CH, S5_CH, toep_ref)
        expand(ms_ref, es_ref, S5_CH, S5_STATE, wst_ref)
        expand(mo_ref, et_ref, S5_STATE, S5_CH, wout_ref)

    n_chunk = s_ref.shape[0]
    u8 = jnp.concatenate([u_ref[pl.ds(s, n_chunk, stride=S5_CHUNK), :] for s in range(S5_CHUNK)],
                         axis=1).astype(BF16)
    s_ref[...] = _dot(u8, wst_ref[...])
    a1, a2 = a1_ref[0], a2_ref[0]
    half = a1.shape[1] // 2

    def step(c, h):
        hp_ref[pl.ds(c, 1), :] = h
        swapped = jnp.concatenate([h[:, half:], h[:, :half]], axis=1)
        return a1 * h + a2 * swapped + s_ref[pl.ds(c, 1), :]

    hl_ref[0, 0] = lax.fori_loop(0, n_chunk, step, jnp.zeros((1, 2 * half), F32))
    y8 = _dot(u8, toep_ref[...]) + _dot(hp_ref[...].astype(BF16), wout_ref[...])
    for t in range(S5_CHUNK):
        y_ref[pl.ds(t, n_chunk, stride=S5_CHUNK), :] = y8[:, t * LANES:(t + 1) * LANES]


def s5_prompt_weights(lam_re, lam_im, log_dt, b_re, b_im, c_re, c_im):
    chunk = S5_CHUNK
    dt = jnp.exp(log_dt)[:, None]
    tau = jnp.arange(chunk + 1, dtype=F32)[:, None, None]
    mag = jnp.exp(lam_re * dt * tau)
    pw_r = mag * jnp.cos(lam_im * dt * tau)
    pw_i = mag * jnp.sin(lam_im * dt * tau)
    abar_r, abar_i = pw_r[1], pw_i[1]
    den = lam_re * lam_re + lam_im * lam_im
    fr = ((abar_r - 1.0) * lam_re + abar_i * lam_im) / den
    fi = (abar_i * lam_re - (abar_r - 1.0) * lam_im) / den
    bbr = fr[..., None] * b_re - fi[..., None] * b_im
    bbi = fr[..., None] * b_im + fi[..., None] * b_re
    ab_r = pw_r[..., None] * bbr - pw_i[..., None] * bbi
    ab_i = pw_r[..., None] * bbi + pw_i[..., None] * bbr
    kern = jnp.sum(c_re[None, :, :, :, None] * ab_r[:chunk, :, None] - c_im[None, :, :, :, None] * ab_i[:chunk, :, None],
                   axis=3)
    s_idx = jnp.arange(chunk)[:, None]
    t_idx = jnp.arange(chunk)[None, :]
    lag = t_idx - s_idx
    k_st = kern[jnp.maximum(lag, 0)]
    k_st = jnp.where((lag >= 0)[:, :, None, None, None], k_st, 0.0)
    split = lambda x, axis: x.reshape(x.shape[:axis] + (N_SUPER, S5_PACK) + x.shape[axis + 1:])
    width = chunk * LANES
    n_state = S5_PACK * S5_STATE
    m_toep = split(k_st, 2).transpose(2, 0, 3, 5, 1, 4).reshape(N_SUPER, width, LANES)
    rev = chunk - 1 - jnp.arange(chunk)
    ab_rev = jnp.stack([ab_r[rev], ab_i[rev]])
    m_st = split(ab_rev, 2).transpose(2, 1, 3, 5, 0, 4).reshape(N_SUPER, width, LANES)
    ar, ai = pw_r[1:, :, None, :], pw_i[1:, :, None, :]
    co_r = c_re[None] * ar - c_im[None] * ai
    co_i = -c_re[None] * ai - c_im[None] * ar
    co = jnp.stack([co_r, co_i])
    m_out = split(co, 2).transpose(2, 0, 3, 5, 1, 4).reshape(N_SUPER, 2 * n_state, LANES)
    src = jnp.arange(LANES)[:, None]
    col = jnp.arange(width)[None, :]
    e_to = ((src // S5_CH == col // LANES) & (src % S5_CH == col % S5_CH)).astype(BF16)
    e_st = ((src // S5_STATE == col // n_state) & (src % S5_STATE == col % S5_STATE)).astype(BF16)
    last_r = pw_r[chunk].reshape(N_SUPER, 1, n_state)
    last_i = pw_i[chunk].reshape(N_SUPER, 1, n_state)
    a1 = jnp.concatenate([last_r, last_r], axis=2)
    a2 = jnp.concatenate([-last_i, last_i], axis=2)
    return m_toep, m_st, m_out, e_to, e_st, a1, a2


def s5_prompt(mid, weights):
    n_chunk = SEQ // S5_CHUNK
    n_state2 = 2 * S5_PACK * S5_STATE
    width = S5_CHUNK * LANES
    assert width == n_state2
    seq_blk = pl.BlockSpec((SEQ, LANES), lambda g, b: (b, g))
    per_g = lambda shape: pl.BlockSpec((1,) + shape, lambda g, b: (g, 0, 0))
    spread = pl.BlockSpec((LANES, width), lambda g, b: (0, 0))
    y, h_last = _call(
        _s5_prompt_body, grid=(N_SUPER, BATCH),
        in_specs=[seq_blk, per_g((width, LANES)), per_g((width, LANES)), per_g((n_state2, LANES)),
                  spread, spread, per_g((1, n_state2)), per_g((1, n_state2))],
        out_specs=[seq_blk, pl.BlockSpec((1, 1, 1, n_state2), lambda g, b: (b, g, 0, 0))],
        out_shape=[jax.ShapeDtypeStruct((M_PROMPT, S5_WIDTH), F32),
                   jax.ShapeDtypeStruct((BATCH, N_SUPER, 1, n_state2), F32)],
        scratch_shapes=[pltpu.VMEM((n_chunk, n_state2), F32), pltpu.VMEM((n_chunk, n_state2), F32)]
        + [pltpu.VMEM((width, width), BF16)] * 3,
        name="s5_prompt")(mid, *weights)
    h = h_last.reshape(BATCH, N_SUPER, 2, S5_PACK, S5_STATE).transpose(0, 1, 3, 4, 2)
    return y, h.reshape(BATCH, S5_GROUPS, S5_STATE, 2)


def _s5_post_body(y_ref, u_ref, d_ref, w_ref, b_ref, o_ref):
    z = _gelu(y_ref[...] + d_ref[...] * u_ref[...])
    lin = _dot(z.astype(BF16), w_ref[...].astype(BF16))
    o_ref[...] = (z * _sigmoid(lin + b_ref[...])).astype(o_ref.dtype)


def s5_post(y, mid, d_skip, w_glu, b_glu, tm):
    m = y.shape[0]
    row = lambda i: (i, 0)
    fixed = lambda i: (0, 0)
    return _call(
        _s5_post_body, grid=(m // tm,),
        in_specs=[pl.BlockSpec((tm, S5_WIDTH), row), pl.BlockSpec((tm, S5_WIDTH), row),
                  pl.BlockSpec((1, S5_WIDTH), fixed), pl.BlockSpec((S5_WIDTH, S5_WIDTH), fixed),
                  pl.BlockSpec((1, S5_WIDTH), fixed)],
        out_specs=pl.BlockSpec((tm, S5_WIDTH), row),
        out_shape=jax.ShapeDtypeStruct((m, S5_WIDTH), BF16),
        name="s5_post")(y, mid, d_skip.reshape(1, -1), w_glu, b_glu.reshape(1, -1))


def _s5_decode_body(u_ref, hr_ref, hi_ref, ar_ref, ai_ref, bb_ref, cc_ref, y_ref, sr_ref, si_ref):
    n = hr_ref.shape[1]
    bu = _dot(u_ref[...].astype(BF16), bb_ref[...])
    ar, ai = ar_ref[...], ai_ref[...]
    hr, hi = hr_ref[...], hi_ref[...]
    sr = bu[:, :n] + (ar * hr - ai * hi)
    si = bu[:, n:] + (ar * hi + ai * hr)
    sr_ref[...] = sr
    si_ref[...] = si
    y_ref[...] = _dot(jnp.concatenate([sr, si], axis=1).astype(BF16), cc_ref[...])


def s5_decode(mid, state, lam_re, lam_im, log_dt, b_re, b_im, c_re, c_im):
    dt = jnp.exp(log_dt)[:, None]
    mag = jnp.exp(lam_re * dt)
    abar_r, abar_i = mag * jnp.cos(lam_im * dt), mag * jnp.sin(lam_im * dt)
    den = lam_re * lam_re + lam_im * lam_im
    fr = ((abar_r - 1.0) * lam_re + abar_i * lam_im) / den
    fi = (abar_i * lam_re - (abar_r - 1.0) * lam_im) / den
    bbr = fr[..., None] * b_re - fi[..., None] * b_im
    bbi = fr[..., None] * b_im + fi[..., None] * b_re
    eye = jnp.eye(S5_GROUPS, dtype=F32)
    n_state = S5_GROUPS * S5_STATE
    in_map = lambda w: jnp.einsum("gpc,gk->gckp", w, eye).reshape(S5_WIDTH, n_state)
    out_map = lambda w: jnp.einsum("gop,gk->gpko", w, eye).reshape(n_state, S5_WIDTH)
    bb = jnp.concatenate([in_map(bbr), in_map(bbi)], axis=1).astype(BF16)
    cc = jnp.concatenate([out_map(c_re), -out_map(c_im)], axis=0).astype(BF16)
    nb = DEC_BATCH
    fixed = lambda i: (0, 0)
    st = pl.BlockSpec((nb, n_state), fixed)
    vec = pl.BlockSpec((1, n_state), fixed)
    y, sr, si = _call(
        _s5_decode_body, grid=(1,),
        in_specs=[pl.BlockSpec((nb, S5_WIDTH), fixed), st, st, vec, vec,
                  pl.BlockSpec((S5_WIDTH, 2 * n_state), fixed), pl.BlockSpec((2 * n_state, S5_WIDTH), fixed)],
        out_specs=[pl.BlockSpec((nb, S5_WIDTH), fixed), st, st],
        out_shape=[jax.ShapeDtypeStruct((nb, S5_WIDTH), F32), jax.ShapeDtypeStruct((nb, n_state), F32),
                   jax.ShapeDtypeStruct((nb, n_state), F32)],
        name="s5_decode")(mid, state[..., 0].reshape(nb, n_state), state[..., 1].reshape(nb, n_state),
                          abar_r.reshape(1, n_state), abar_i.reshape(1, n_state), bb, cc)
    new_state = jnp.stack([sr.reshape(nb, S5_GROUPS, S5_STATE), si.reshape(nb, S5_GROUPS, S5_STATE)], axis=-1)
    return y, new_state


def _lru_gates(xc, wa_ref, ba_ref, wi_ref, bi_ref, c8_ref):
    xb = xc.astype(BF16)
    r = _sigmoid(_dot(xb, wa_ref[...]) + ba_ref[...])
    ig = _sigmoid(_dot(xb, wi_ref[...]) + bi_ref[...])
    log_a = c8_ref[...] * r
    a = jnp.exp(log_a)
    b = jnp.sqrt(-jnp.tanh(log_a) * (a * a + 1.0)) * (ig * xc)
    return a, b


def _lru_pre_body(x_ref, p_ref, cw_ref, cb_ref, wa_ref, ba_ref, wi_ref, bi_ref, c8_ref,
                  a_ref, b_ref, ext_ref, *, tiles_per_seq):
    tm = x_ref.shape[0]
    first = pl.program_id(0) % tiles_per_seq == 0
    ext_ref[0:8, :] = jnp.where(first, 0.0, p_ref[...])
    ext_ref[8:, :] = x_ref[...]
    xc = cb_ref[...] + cw_ref[3:4, :] * x_ref[...]
    for k in (1, 2, 3):
        xc = xc + cw_ref[3 - k:4 - k, :] * ext_ref[8 - k:8 - k + tm, :]
    a, b = _lru_gates(xc, wa_ref, ba_ref, wi_ref, bi_ref, c8_ref)
    a_ref[...] = a
    b_ref[...] = b


def _lru_scan_body(a_ref, b_ref, g_ref, y_ref, hl_ref, h_ref, p_ref):
    t_len = a_ref.shape[0]
    seg_len = t_len // LRU_SEG
    n_q = LRU_SEG // 8

    def step(t, carry):
        out = []
        for qi in range(n_q):
            h, p = carry[2 * qi], carry[2 * qi + 1]
            rows = pl.ds(qi * 8 * seg_len + t, 8, stride=seg_len)
            at = a_ref[rows, :]
            h = at * h + b_ref[rows, :]
            p = at * p
            h_ref[rows, :] = h
            p_ref[rows, :] = p
            out += [h, p]
        return tuple(out)

    init = (jnp.zeros((8, LANES), F32), jnp.ones((8, LANES), F32)) * n_q
    lax.fori_loop(0, seg_len, step, init)
    carry = jnp.zeros((1, LANES), F32)
    for s in range(LRU_SEG):
        rows = slice(s * seg_len, (s + 1) * seg_len)
        h = h_ref[rows, :] + p_ref[rows, :] * carry
        y_ref[rows, :] = (h * _gelu(g_ref[rows, :])).astype(y_ref.dtype)
        carry = h[seg_len - 1:seg_len, :]
    hl_ref[0] = carry


def lru_weights(conv_w, conv_b, w_a, b_a, w_i, b_i, lam):
    eye = jnp.eye(LRU_HEADS, dtype=F32)
    bd = lambda w: jnp.einsum("hij,hk->hikj", w, eye).reshape(LRU_WIDTH, LRU_WIDTH).astype(BF16)
    c8 = (-LRU_C) * jax.nn.softplus(-lam)
    return (conv_w, conv_b.reshape(1, -1), bd(w_a), b_a.reshape(1, -1), bd(w_i), b_i.reshape(1, -1),
            c8.reshape(1, -1))


def lru_prompt(mid, lw, tm):
    conv_w, conv_b, wa, ba, wi, bi, c8 = lw
    fixed = lambda i: (0, 0)
    vec = pl.BlockSpec((1, LRU_WIDTH), fixed)
    mat = pl.BlockSpec((LRU_WIDTH, LRU_WIDTH), fixed)
    a, b = _call(
        functools.partial(_lru_pre_body, tiles_per_seq=SEQ // tm), grid=(M_PROMPT // tm,),
        in_specs=[pl.BlockSpec((tm, LRU_WIDTH), lambda i: (i, 1)),
                  pl.BlockSpec((8, LRU_WIDTH), lambda i: (jnp.maximum(i * (tm // 8) - 1, 0), 1)),
                  pl.BlockSpec((4, LRU_WIDTH), fixed), vec, mat, vec, mat, vec, vec],
        out_specs=[pl.BlockSpec((tm, LRU_WIDTH), lambda i: (i, 0))] * 2,
        out_shape=[jax.ShapeDtypeStruct((M_PROMPT, LRU_WIDTH), F32)] * 2,
        scratch_shapes=[pltpu.VMEM((tm + 8, LRU_WIDTH), F32)],
        name="lru_pre")(mid, mid, conv_w, conv_b, wa, ba, wi, bi, c8)
    n_lane = LRU_WIDTH // LANES
    g_col0 = (S5_WIDTH + LRU_WIDTH) // LANES
    blk = lambda off: pl.BlockSpec((SEQ, LANES), lambda bi_, c: (bi_, off + c))
    y, h_last = _call(
        _lru_scan_body, grid=(BATCH, n_lane),
        in_specs=[blk(0), blk(0), blk(g_col0)],
        out_specs=[blk(0), pl.BlockSpec((1, 1, LANES), lambda bi_, c: (bi_, 0, c))],
        out_shape=[jax.ShapeDtypeStruct((M_PROMPT, LRU_WIDTH), BF16),
                   jax.ShapeDtypeStruct((BATCH, 1, LRU_WIDTH), F32)],
        scratch_shapes=[pltpu.VMEM((SEQ, LANES), F32), pltpu.VMEM((SEQ, LANES), F32)],
        name="lru_scan")(a, b, mid)
    return y, h_last.reshape(BATCH, LRU_WIDTH)


def _lru_decode_body(x_ref, g_ref, c0_ref, c1_ref, c2_ref, h0_ref, cw_ref, cb_ref, wa_ref, ba_ref,
                     wi_ref, bi_ref, c8_ref, y_ref, h_ref):
    xc = (cb_ref[...] + cw_ref[0:1, :] * c0_ref[...] + cw_ref[1:2, :] * c1_ref[...]
          + cw_ref[2:3, :] * c2_ref[...] + cw_ref[3:4, :] * x_ref[...])
    a, b = _lru_gates(xc, wa_ref, ba_ref, wi_ref, bi_ref, c8_ref)
    h = a * h0_ref[...] + b
    h_ref[...] = h
    y_ref[...] = (h * _gelu(g_ref[...])).astype(y_ref.dtype)


def lru_decode(mid, conv_buf, h0, lw):
    conv_w, conv_b, wa, ba, wi, bi, c8 = lw
    nb = DEC_BATCH
    fixed = lambda i: (0, 0)
    vec = pl.BlockSpec((1, LRU_WIDTH), fixed)
    mat = pl.BlockSpec((LRU_WIDTH, LRU_WIDTH), fixed)
    st = pl.BlockSpec((nb, LRU_WIDTH), fixed)
    return _call(
        _lru_decode_body, grid=(1,),
        in_specs=[pl.BlockSpec((nb, LRU_WIDTH), lambda i: (0, 1)), pl.BlockSpec((nb, LRU_WIDTH), lambda i: (0, 2)),
                  st, st, st, st, pl.BlockSpec((4, LRU_WIDTH), fixed), vec, mat, vec, mat, vec, vec],
        out_specs=[st, st],
        out_shape=[jax.ShapeDtypeStruct((nb, LRU_WIDTH), BF16), jax.ShapeDtypeStruct((nb, LRU_WIDTH), F32)],
        name="lru_decode")(mid, mid, conv_buf[:, 0], conv_buf[:, 1], conv_buf[:, 2], h0,
                           conv_w, conv_b, wa, ba, wi, bi, c8)


TM = 1024
TM_SMALL = 512
TM_EXPERT = 512
M_ALL = M_PROMPT + DEC_BATCH
N_EXPERT_TILES = (2 * M_ALL) // TM_EXPERT + N_EXPERTS
TM_ROUTER = M_ALL // 16
DEC_ATT_BS = (16, 8, 2)


def _token_mixers(layer, xp, xs, p, caches_t, win_prev, state_s5, state_conv, state_lru):
    n_p = rmsnorm(xp, p["norm_mix"][layer], TM_SMALL, BF16)
    n_s = rmsnorm(xs, p["norm_mix"][layer], DEC_BATCH, BF16)
    w_in = p["w_in"]
    proj = functools.partial(matmul_ws, [n_p], [n_s], w_of_x=[0], tm=TM)
    qkv_p, qkv_s = proj([(w_in, (layer,))], n_cols=OFF_MID, tn=768, out_dtype=F32, epilogue=_ep_plain,
                        name="proj_qkv")
    mid_p, mid_s = proj([(w_in[layer, :, OFF_MID:OFF_GATE], ())], tn=640, out_dtype=F32, epilogue=_ep_plain,
                        name="proj_mid")
    gates_p, gates_s = proj([(w_in[layer, :, OFF_GATE:], ())], tn=1024, out_dtype=BF16, out_dtype_s=F32,
                            epilogue=_ep_sigmoid, name="proj_gates")

    qn_p, kn_p = qk_norm(qkv_p, p["q_gain"][layer], p["k_gain"][layer], TM_SMALL, F32)
    qn_s, kn_s = qk_norm(qkv_s, p["q_gain"][layer], p["k_gain"][layer], DEC_BATCH, F32)
    outs, lses, outs_s, lses_s = [], [], [], []
    new_p, win = [], []
    for g, (window, dil) in enumerate(ATT_GROUPS):
        cols = slice(GROUP_W * g, GROUP_W * (g + 1))
        v_cols = slice(2 * ATT_WIDTH + GROUP_W * g, 2 * ATT_WIDTH + GROUP_W * (g + 1))
        o, lse = prompt_attention(g, qn_p, kn_p, qkv_p)
        outs.append(o)
        lses.append(lse)
        keep = min(window, SEQ)
        tail = lambda x, cs: (x.reshape(BATCH, SEQ, x.shape[1])[:, SEQ - keep:, cs]
                              .reshape(BATCH, keep, HEADS, HEAD_DIM))
        new_p.append(tail(kn_p, cols))
        new_p.append(tail(qkv_p, v_cols))
        cols_s = jnp.concatenate([qn_s[:, cols], kn_s[:, cols], qkv_s[:, v_cols]], axis=1).T
        prev_k, prev_v = (None, None) if win_prev is None else (win_prev[2 * g], win_prev[2 * g + 1])
        o_s, lse_s, wk, wv = decode_attention(layer, g, cols_s, caches_t[2 * g], caches_t[2 * g + 1],
                                              prev_k, prev_v, DEC_ATT_BS[g])
        outs_s.append(o_s.reshape(DEC_BATCH, GROUP_W))
        lses_s.append(lse_s.reshape(DEC_BATCH, GROUP_W))
        win += [wk, wv]
    att_p = mix_groups(outs, lses, TM_SMALL, BF16)
    att_s = mix_groups(outs_s, lses_s, DEC_BATCH, BF16)

    s5p = (p["s5_lam_re"][layer], p["s5_lam_im"][layer], p["s5_log_dt"][layer], p["s5_b_re"][layer],
           p["s5_b_im"][layer], p["s5_c_re"][layer], p["s5_c_im"][layer])
    y_p, h_p = s5_prompt(mid_p, s5_prompt_weights(*s5p))
    y_s, h_s = s5_decode(mid_s, state_s5[layer], *s5p)
    glu = (p["s5_d"][layer], p["s5_w_glu"][layer], p["s5_b_glu"][layer])
    s5_p = s5_post(y_p, mid_p, *glu, TM_SMALL)
    s5_s = s5_post(y_s, mid_s, *glu, DEC_BATCH)
    new_p.append(h_p)
    new_s = [h_s]

    lw = lru_weights(p["lru_conv_w"][layer], p["lru_conv_b"][layer], p["lru_w_a"][layer], p["lru_b_a"][layer],
                     p["lru_w_i"][layer], p["lru_b_i"][layer], p["lru_lam"][layer])
    lru_p, h_lp = lru_prompt(mid_p, lw, 512)
    lru_s, h_ls = lru_decode(mid_s, state_conv[layer], state_lru[layer], lw)
    x_lru = slice(S5_WIDTH, S5_WIDTH + LRU_WIDTH)
    new_p.append(mid_p[:, x_lru].reshape(BATCH, SEQ, LRU_WIDTH)[:, SEQ - 3:])
    new_p.append(h_lp)
    new_s.append(jnp.concatenate([state_conv[layer, :, 1:], mid_s[:, None, x_lru]], axis=1))
    new_s.append(h_ls)

    tn = 512
    gate_tiles = D_MODEL // tn
    merged_p, merged_s = matmul_ws(
        [att_p, s5_p, lru_p], [att_s, s5_s, lru_s],
        [(p["w_br_attn"], (layer,)), (p["w_br_s5"], (layer,)), (p["w_br_lru"], (layer,))],
        w_of_x=[0, 1, 2], tm=TM, tn=tn, out_dtype=BF16, epilogue=_ep_merge,
        tile_extras=[(gates_p, gates_s, 0), (gates_p, gates_s, gate_tiles), (gates_p, gates_s, 2 * gate_tiles)],
        name="merge")
    xp, xs = matmul_ws([merged_p], [merged_s], [(p["w_o"], (layer,))], w_of_x=[0], tm=TM, tn=512,
                       out_dtype=F32, epilogue=_ep_residual, tile_extras=[(xp, xs, 0)], name="out_proj")
    return xp, xs, new_p, new_s, win


def _dense_ffn(xp, xs, p, layer, j):
    n_p = rmsnorm(xp, p["norm_ffn"][layer], TM_SMALL, BF16)
    n_s = rmsnorm(xs, p["norm_ffn"][layer], DEC_BATCH, BF16)
    h_p, h_s = matmul_ws([n_p], [n_s], [(p["ffn_w_gate"], (j,)), (p["ffn_w_up"], (j,))], w_of_x=[0, 0],
                         tm=TM, tn=512, out_dtype=BF16, epilogue=_ep_swiglu, name="ffn_up")
    return matmul_ws([h_p], [h_s], [(p["ffn_w_down"], (j,))], w_of_x=[0], tm=512, tn=512, out_dtype=F32,
                     epilogue=_ep_residual, tile_extras=[(xp, xs, 0)], name="ffn_down")


def _moe_ffn(xp, xs, p, layer, j):
    router = (p["norm_ffn"][layer], p["moe_router_w"][j], p["moe_router_b"][j])
    n, comb, idx = rmsnorm_router(jnp.concatenate([xp, xs], axis=0), *router, TM_ROUTER)
    idx = idx[:, :2]
    m = M_ALL
    top_e = idx.reshape(-1)
    top_w = jnp.take_along_axis(comb, idx, axis=1).reshape(-1)
    counts = jnp.sum(top_e[:, None] == jnp.arange(N_EXPERTS)[None, :], axis=0).astype(jnp.int32)
    padded = ((counts + TM_EXPERT - 1) // TM_EXPERT) * TM_EXPERT
    pad_end = jnp.cumsum(padded)
    pad_start = pad_end - padded
    start = jnp.cumsum(counts) - counts
    order = jnp.argsort(top_e, stable=True).astype(jnp.int32)
    rank = jnp.argsort(order).astype(jnp.int32)
    dest = pad_start[top_e] + rank - start[top_e]
    n_rows = N_EXPERT_TILES * TM_EXPERT
    tile_start = jnp.arange(N_EXPERT_TILES, dtype=jnp.int32) * TM_EXPERT
    tile_expert = jnp.minimum(jnp.sum(tile_start[:, None] >= pad_end[None, :], axis=1), N_EXPERTS - 1)
    tile_expert = tile_expert.astype(jnp.int32)
    n_used = (pad_end[-1] // TM_EXPERT).reshape(1).astype(jnp.int32)
    row_e = jnp.repeat(tile_expert, TM_EXPERT)
    within = jnp.arange(n_rows, dtype=jnp.int32) - pad_start[row_e]
    row_valid = within < counts[row_e]
    row_pair = order[jnp.clip(start[row_e] + within, 0, 2 * m - 1)]
    src_token = jnp.where(row_valid, row_pair // 2, 0)
    row_w = jnp.where(row_valid, top_w[row_pair], 0.0)
    x_sorted = jnp.take(n, src_token, axis=0, mode="clip")
    gmm = functools.partial(grouped_matmul_ws, tile_expert, n_used, lead=(j,), tm=TM_EXPERT)
    gate = gmm(x_sorted, p["moe_w_gate"], tn=1408, out_dtype=BF16, epilogue=_ep_plain, name="moe_gate")
    h = gmm(x_sorted, p["moe_w_up"], tn=1408, out_dtype=BF16, epilogue=_ep_silu_times, tile_extras=[gate],
            name="moe_up")
    ys = gmm(h, p["moe_w_down"], tn=1024, out_dtype=F32, epilogue=_ep_scale_rows,
             col_extras=[row_w.reshape(-1, 1)], name="moe_down")
    dest2 = dest.reshape(m, 2)
    y = jnp.take(ys, dest2[:, 0], axis=0, mode="clip") + jnp.take(ys, dest2[:, 1], axis=0, mode="clip")
    return xp + y[:M_PROMPT], xs + y[M_PROMPT:]


def kernel(x_prompt, x_sample, cache_k_w128, cache_v_w128, cache_k_w512, cache_v_w512, cache_k_w2048, cache_v_w2048, state_s5, state_conv, state_lru, norm_mix, w_in, q_gain, k_gain, s5_lam_re, s5_lam_im, s5_log_dt, s5_b_re, s5_b_im, s5_c_re, s5_c_im, s5_d, s5_w_glu, s5_b_glu, lru_conv_w, lru_conv_b, lru_w_a, lru_b_a, lru_w_i, lru_b_i, lru_lam, w_br_attn, w_br_s5, w_br_lru, w_o, norm_ffn, ffn_w_gate, ffn_w_up, ffn_w_down, moe_router_w, moe_router_b, moe_w_gate, moe_w_up, moe_w_down):
    p = dict(norm_mix=norm_mix, w_in=w_in, q_gain=q_gain, k_gain=k_gain, s5_lam_re=s5_lam_re,
             s5_lam_im=s5_lam_im, s5_log_dt=s5_log_dt, s5_b_re=s5_b_re, s5_b_im=s5_b_im, s5_c_re=s5_c_re,
             s5_c_im=s5_c_im, s5_d=s5_d, s5_w_glu=s5_w_glu, s5_b_glu=s5_b_glu, lru_conv_w=lru_conv_w,
             lru_conv_b=lru_conv_b, lru_w_a=lru_w_a, lru_b_a=lru_b_a, lru_w_i=lru_w_i, lru_b_i=lru_b_i,
             lru_lam=lru_lam, w_br_attn=w_br_attn, w_br_s5=w_br_s5, w_br_lru=w_br_lru, w_o=w_o,
             norm_ffn=norm_ffn, ffn_w_gate=ffn_w_gate, ffn_w_up=ffn_w_up, ffn_w_down=ffn_w_down,
             moe_router_w=moe_router_w, moe_router_b=moe_router_b, moe_w_gate=moe_w_gate,
             moe_w_up=moe_w_up, moe_w_down=moe_w_down)
    caches_t = [jnp.transpose(c, (0, 1, 3, 4, 2)) for c in
                (cache_k_w128, cache_v_w128, cache_k_w512, cache_v_w512, cache_k_w2048, cache_v_w2048)]
    xp = x_prompt.reshape(M_PROMPT, D_MODEL)
    xs = x_sample.reshape(DEC_BATCH, D_MODEL)
    states_p = [[] for _ in range(9)]
    states_s = [[] for _ in range(3)]
    win = None
    for layer in range(DEPTH):
        xp, xs, new_p, new_s, win = _token_mixers(layer, xp, xs, p, caches_t, win, state_s5, state_conv,
                                                  state_lru)
        if layer % 2 == 0:
            xp, xs = _dense_ffn(xp, xs, p, layer, layer // 2)
        else:
            xp, xs = _moe_ffn(xp, xs, p, layer, layer // 2)
        for i in range(9):
            states_p[i].append(new_p[i])
        for i in range(3):
            states_s[i].append(new_s[i])
    y_prompt = xp.reshape(BATCH, SEQ, D_MODEL)
    y_sample = xs.reshape(DEC_BATCH, 1, D_MODEL)
    windows = [jnp.transpose(w, (0, 1, 4, 2, 3)) for w in win]
    return (y_prompt, y_sample, *[jnp.stack(a) for a in states_p], *windows,
            *[jnp.stack(a) for a in states_s])
```

```python
import functools
import math

import jax
import jax.numpy as jnp
from jax import lax
from jax.experimental import pallas as pl
from jax.experimental.pallas import tpu as pltpu

F32 = jnp.float32
BF16 = jnp.bfloat16

D_MODEL = 2048
BATCH = 2
SEQ = 4096
DEPTH = 2
DEC_BATCH = 128
HEAD_DIM = 64
HEADS = 4
GROUP_W = HEADS * HEAD_DIM
ATT_GROUPS = ((128, 1), (512, 4), (2048, 16))
N_GROUPS = 3
ATT_WIDTH = N_GROUPS * GROUP_W
N_KEYS = 128
S5_WIDTH = 640
S5_GROUPS = 40
S5_CH = 16
S5_STATE = 64
LRU_WIDTH = 640
LRU_HEADS = 8
LRU_C = 8.0
OFF_MID = 3 * ATT_WIDTH
MID_WIDTH = S5_WIDTH + 2 * LRU_WIDTH
OFF_GATE = OFF_MID + MID_WIDTH
N_EXPERTS = 8
D_FF_EXPERT = 2816
EPS = 1e-6
LANES = 128

M_PROMPT = BATCH * SEQ

S5_CHUNK = 8
S5_PACK = LANES // S5_CH
N_SUPER = S5_GROUPS // S5_PACK
LRU_SEG = 32
NEG_BIG = -1e30

VMEM_LIMIT_BYTES = 56 * 1024 * 1024


def _call(body, *, grid, in_specs, out_specs, out_shape, name, scratch_shapes=(), num_scalar_prefetch=0,
          input_output_aliases=None):
    params = pltpu.CompilerParams(dimension_semantics=("arbitrary",) * len(grid),
                                  vmem_limit_bytes=VMEM_LIMIT_BYTES)
    kwargs = {}
    if input_output_aliases:
        kwargs["input_output_aliases"] = input_output_aliases
    if num_scalar_prefetch:
        grid_spec = pltpu.PrefetchScalarGridSpec(
            num_scalar_prefetch=num_scalar_prefetch, grid=grid, in_specs=in_specs,
            out_specs=out_specs, scratch_shapes=scratch_shapes)
        return pl.pallas_call(body, grid_spec=grid_spec, out_shape=out_shape,
                              compiler_params=params, name=name, **kwargs)
    return pl.pallas_call(body, grid=grid, in_specs=in_specs, out_specs=out_specs,
                          out_shape=out_shape, scratch_shapes=scratch_shapes,
                          compiler_params=params, name=name, **kwargs)


def _dot(a, b):
    return jnp.dot(a, b, preferred_element_type=F32)


def _dot_sel(x, sel, parts=2):
    acc = None
    r = x
    for _ in range(parts):
        piece = r.astype(BF16)
        r = r - piece.astype(F32)
        d = _dot(piece, sel)
        acc = d if acc is None else acc + d
    return acc


def _sigmoid(x):
    return 1.0 / (1.0 + jnp.exp(-x))


def _gelu(x):
    c = math.sqrt(2.0 / math.pi)
    return 0.5 * x * (1.0 + jnp.tanh(c * (x + 0.044715 * (x * x * x))))


def _rmsnorm_body(x_ref, g_ref, o_ref):
    x = x_ref[...]
    y = x * lax.rsqrt(jnp.mean(x * x, axis=-1, keepdims=True) + EPS)
    o_ref[...] = (y * g_ref[...]).astype(o_ref.dtype)


def _rmsnorm_router_body(x_ref, g_ref, rw_ref, rb_ref, o_ref, comb_ref, idx_ref):
    x = x_ref[...]
    y = x * lax.rsqrt(jnp.mean(x * x, axis=-1, keepdims=True) + EPS)
    xn = y * g_ref[...]
    o_ref[...] = xn.astype(o_ref.dtype)
    logits = _dot(xn.astype(BF16), rw_ref[...].astype(BF16)) + rb_ref[...]
    lane = lax.broadcasted_iota(jnp.int32, logits.shape, 1).astype(F32)
    m1 = jnp.max(logits, axis=1, keepdims=True)
    i1 = jnp.min(jnp.where(logits == m1, lane, float(LANES)), axis=1, keepdims=True)
    rest = jnp.where(lane == i1, -jnp.inf, logits)
    m2 = jnp.max(rest, axis=1, keepdims=True)
    i2 = jnp.min(jnp.where(rest == m2, lane, float(LANES)), axis=1, keepdims=True)
    e = jnp.exp(m2 - m1)
    w1 = 1.0 / (1.0 + e)
    w2 = e / (1.0 + e)
    comb_ref[...] = jnp.where(lane == i1, w1, 0.0) + jnp.where(lane == i2, w2, 0.0)
    idx_ref[...] = jnp.where(lane == 0.0, i1, jnp.where(lane == 1.0, i2, 0.0)).astype(jnp.int32)


def rmsnorm(x, gain, tm, out_dtype):
    m, d = x.shape
    return _call(
        _rmsnorm_body, grid=(m // tm,),
        in_specs=[pl.BlockSpec((tm, d), lambda i: (i, 0)), pl.BlockSpec((1, d), lambda i: (0, 0))],
        out_specs=pl.BlockSpec((tm, d), lambda i: (i, 0)),
        out_shape=jax.ShapeDtypeStruct((m, d), out_dtype), name="rmsnorm")(x, gain.reshape(1, d))


def rmsnorm_router(x, gain, router_w, router_b, tm):
    m, d = x.shape
    ne = router_w.shape[1]
    rw = jnp.zeros((d, LANES), F32).at[:, :ne].set(router_w)
    rb = jnp.full((1, LANES), NEG_BIG, F32).at[0, :ne].set(router_b)
    row = lambda i: (i, 0)
    fixed = lambda i: (0, 0)
    return _call(
        _rmsnorm_router_body, grid=(m // tm,),
        in_specs=[pl.BlockSpec((tm, d), row), pl.BlockSpec((1, d), fixed),
                  pl.BlockSpec((d, LANES), fixed), pl.BlockSpec((1, LANES), fixed)],
        out_specs=[pl.BlockSpec((tm, d), row), pl.BlockSpec((tm, LANES), row), pl.BlockSpec((tm, LANES), row)],
        out_shape=[jax.ShapeDtypeStruct((m, d), F32), jax.ShapeDtypeStruct((m, LANES), F32),
                   jax.ShapeDtypeStruct((m, LANES), jnp.int32)],
        name="rmsnorm_router")(x, gain.reshape(1, d), rw, rb)


def _mm_body(*refs, n_x, w_of_x, n_te, n_re, epilogue, n_ptiles):
    n_w = len(w_of_x)
    pos = 0
    xp_refs = refs[pos:pos + n_x]; pos += n_x
    xs_refs = refs[pos:pos + n_x]; pos += n_x
    w_refs = refs[pos:pos + n_w]; pos += n_w
    tep_refs = refs[pos:pos + n_te]; pos += n_te
    tes_refs = refs[pos:pos + n_te]; pos += n_te
    re_refs = refs[pos:pos + n_re]; pos += n_re
    op_ref, os_ref = refs[pos], refs[pos + 1]; pos += 2
    wb_refs = refs[pos:pos + n_w]
    i = pl.program_id(1)

    @pl.when(i == 0)
    def _():
        for w_ref, wb_ref in zip(w_refs, wb_refs):
            wb_ref[...] = w_ref[...].astype(BF16)

    def rows(x_refs, te_refs, o_ref):
        accs = [_dot(x_refs[xi][...].astype(BF16), wb_ref[...]) for xi, wb_ref in zip(w_of_x, wb_refs)]
        extras = [e[...] for e in te_refs] + [e[...] for e in re_refs]
        o_ref[...] = epilogue(accs, extras).astype(o_ref.dtype)

    @pl.when(i < n_ptiles)
    def _():
        rows(xp_refs, tep_refs, op_ref)

    @pl.when(i == n_ptiles)
    def _():
        rows(xs_refs, tes_refs, os_ref)


def matmul_ws(xps, xss, ws, *, w_of_x, tm, tn, out_dtype, epilogue, tile_extras=(), row_extras=(), name,
              out_dtype_s=None, n_cols=None):
    mp = xps[0].shape[0]
    ms = xss[0].shape[0]
    n = n_cols or ws[0][0].shape[-1]
    n_ptiles = mp // tm
    last = n_ptiles - 1
    grid = (n // tn, n_ptiles + 1)
    in_specs = [pl.BlockSpec((tm, x.shape[1]), lambda j, i: (jnp.minimum(i, last), 0)) for x in xps]
    in_specs += [pl.BlockSpec((ms, x.shape[1]), lambda j, i: (0, 0)) for x in xss]
    scratch = []
    for w, lead in ws:
        k = w.shape[-2]
        block = (None,) * len(lead) + (k, tn)
        in_specs.append(pl.BlockSpec(block, lambda j, i, lead=lead: lead + (0, j)))
        scratch.append(pltpu.VMEM((k, tn), BF16))
    for _, _, off in tile_extras:
        in_specs.append(pl.BlockSpec((tm, tn), lambda j, i, off=off: (jnp.minimum(i, last), off + j)))
    for _, _, off in tile_extras:
        in_specs.append(pl.BlockSpec((ms, tn), lambda j, i, off=off: (0, off + j)))
    for _ in row_extras:
        in_specs.append(pl.BlockSpec((1, tn), lambda j, i: (0, j)))
    body = functools.partial(_mm_body, n_x=len(xps), w_of_x=tuple(w_of_x), n_te=len(tile_extras),
                             n_re=len(row_extras), epilogue=epilogue, n_ptiles=n_ptiles)
    return _call(
        body, grid=grid, in_specs=in_specs,
        out_specs=[pl.BlockSpec((tm, tn), lambda j, i: (jnp.minimum(i, last), j)),
                   pl.BlockSpec((ms, tn), lambda j, i: (0, j))],
        out_shape=[jax.ShapeDtypeStruct((mp, n), out_dtype),
                   jax.ShapeDtypeStruct((ms, n), out_dtype_s or out_dtype)],
        scratch_shapes=scratch, name=name,
    )(*xps, *xss, *[w for w, _ in ws], *[a for a, _, _ in tile_extras], *[a for _, a, _ in tile_extras],
      *row_extras)


def _ep_plain(accs, extras):
    return accs[0]


def _ep_sigmoid(accs, extras):
    return _sigmoid(accs[0])


def _ep_residual(accs, extras):
    return extras[0] + accs[0]


def _ep_swiglu(accs, extras):
    g, u = accs
    return (g * _sigmoid(g)) * u


def _ep_merge(accs, extras):
    return (extras[0].astype(F32) * accs[0] + extras[1].astype(F32) * accs[1]
            + extras[2].astype(F32) * accs[2])


def _gmm_body(te_ref, nu_ref, *refs, n_te, n_ce, epilogue):
    x_ref, w_ref = refs[0], refs[1]
    extra_refs = refs[2:2 + n_te + n_ce]
    o_ref = refs[2 + n_te + n_ce]
    wb_ref = refs[3 + n_te + n_ce]
    i = pl.program_id(1)
    new_expert = jnp.logical_or(i == 0, te_ref[i] != te_ref[jnp.maximum(i - 1, 0)])

    @pl.when(new_expert)
    def _():
        wb_ref[...] = w_ref[...].astype(BF16)

    @pl.when(i < nu_ref[0])
    def _():
        acc = _dot(x_ref[...].astype(BF16), wb_ref[...])
        o_ref[...] = epilogue([acc], [e[...] for e in extra_refs]).astype(o_ref.dtype)

    @pl.when(i >= nu_ref[0])
    def _():
        o_ref[...] = jnp.zeros(o_ref.shape, o_ref.dtype)


def grouped_matmul_ws(tile_expert, n_used, x, w, lead, *, tm, tn, out_dtype, epilogue, tile_extras=(),
                      col_extras=(), name):
    m = x.shape[0]
    k, n = w.shape[-2:]
    grid = (n // tn, m // tm)
    block = (None,) * (len(lead) + 1) + (k, tn)
    in_specs = [pl.BlockSpec((tm, x.shape[1]), lambda j, i, te, nu: (i, 0)),
                pl.BlockSpec(block, lambda j, i, te, nu: lead + (te[i], 0, j))]
    in_specs += [pl.BlockSpec((tm, tn), lambda j, i, te, nu: (i, j)) for _ in tile_extras]
    in_specs += [pl.BlockSpec((tm, 1), lambda j, i, te, nu: (i, 0)) for _ in col_extras]
    body = functools.partial(_gmm_body, n_te=len(tile_extras), n_ce=len(col_extras), epilogue=epilogue)
    return _call(
        body, grid=grid, in_specs=in_specs,
        out_specs=pl.BlockSpec((tm, tn), lambda j, i, te, nu: (i, j)),
        out_shape=jax.ShapeDtypeStruct((m, n), out_dtype), scratch_shapes=[pltpu.VMEM((k, tn), BF16)],
        num_scalar_prefetch=2, name=name,
    )(tile_expert, n_used, x, w, *tile_extras, *col_extras)


def _ep_silu_times(accs, extras):
    g = extras[0].astype(F32)
    return (g * _sigmoid(g)) * accs[0]


def _ep_scale_rows(accs, extras):
    return extras[0] * accs[0]


def _qknorm_body(q_ref, k_ref, qg_ref, kg_ref, seg_ref, qo_ref, ko_ref):
    seg = seg_ref[...]
    for src, gain, dst, scale in ((q_ref, qg_ref, qo_ref, HEAD_DIM ** -0.5), (k_ref, kg_ref, ko_ref, 1.0)):
        for g in range(N_GROUPS):
            cols = slice(GROUP_W * g, GROUP_W * (g + 1))
            x = src[:, cols]
            ms = _dot_sel(x * x, seg, 3)
            y = x * lax.rsqrt(ms + EPS) * gain[:, cols]
            dst[:, cols] = (y * scale).astype(dst.dtype)


def qk_norm(qkv, q_gain, k_gain, tm, q_dtype):
    m = qkv.shape[0]
    head = jnp.arange(GROUP_W) // HEAD_DIM
    seg = jnp.where(head[:, None] == head[None, :], 1.0 / HEAD_DIM, 0.0).astype(BF16)
    qg = jnp.broadcast_to(q_gain[:, None, :], (N_GROUPS, HEADS, HEAD_DIM)).reshape(1, ATT_WIDTH)
    kg = jnp.broadcast_to(k_gain[:, None, :], (N_GROUPS, HEADS, HEAD_DIM)).reshape(1, ATT_WIDTH)
    fixed = lambda i: (0, 0)
    return _call(
        _qknorm_body, grid=(m // tm,),
        in_specs=[pl.BlockSpec((tm, ATT_WIDTH), lambda i: (i, 0)), pl.BlockSpec((tm, ATT_WIDTH), lambda i: (i, 1)),
                  pl.BlockSpec((1, ATT_WIDTH), fixed), pl.BlockSpec((1, ATT_WIDTH), fixed),
                  pl.BlockSpec((GROUP_W, GROUP_W), fixed)],
        out_specs=[pl.BlockSpec((tm, ATT_WIDTH), lambda i: (i, 0)), pl.BlockSpec((tm, ATT_WIDTH), lambda i: (i, 0))],
        out_shape=[jax.ShapeDtypeStruct((m, ATT_WIDTH), q_dtype), jax.ShapeDtypeStruct((m, ATT_WIDTH), F32)],
        name="qk_norm")(qkv, qkv, qg, kg, seg)


def _attn_body(*refs, dil):
    q_refs, kp_refs, kc_refs, vp_refs, vc_refs = (refs[2 * n:2 * n + 2] for n in range(5))
    o_ref, lse_ref = refs[10], refs[11]
    o_stage, lse_stage = refs[12:14], refs[14:16]
    a = pl.program_id(1)
    tq = N_KEYS
    r = lax.broadcasted_iota(jnp.int32, (HEADS * tq, 2 * tq), 0) % tq
    c = lax.broadcasted_iota(jnp.int32, (HEADS * tq, 2 * tq), 1)
    valid = (c >= r) & (c <= r + N_KEYS) & ((a > 0) | (c >= tq))
    head = lax.broadcasted_iota(jnp.int32, (1, GROUP_W), 1) // HEAD_DIM

    def residue(res, carry):
        rows = pl.ds(res, tq, stride=dil) if dil > 1 else slice(None)
        load = lambda halves: jnp.concatenate([halves[0][rows, :], halves[1][rows, :]], axis=1)
        q = load(q_refs)
        k2 = jnp.concatenate([load(kp_refs), load(kc_refs)], axis=0).astype(BF16)
        v2 = jnp.concatenate([load(vp_refs), load(vc_refs)], axis=0).astype(BF16)
        q4 = jnp.concatenate([jnp.where(head == h, q, 0.0) for h in range(HEADS)], axis=0).astype(BF16)
        s = lax.dot_general(q4, k2, (((1,), (1,)), ((), ())), preferred_element_type=F32)
        s = jnp.where(valid, s, NEG_BIG)
        m = jnp.max(s, axis=1, keepdims=True)
        p = jnp.exp(s - m)
        l = jnp.sum(p, axis=1, keepdims=True)
        o4 = _dot(p.astype(BF16), v2) / l
        lse4 = m + jnp.log(l)
        o = jnp.zeros((tq, GROUP_W), F32)
        lse = jnp.zeros((tq, GROUP_W), F32)
        for h in range(HEADS):
            blk = slice(h * tq, (h + 1) * tq)
            o = jnp.where(head == h, o4[blk, :], o)
            lse = jnp.where(head == h, lse4[blk, :], lse)
        for half in range(2):
            lanes = slice(LANES * half, LANES * (half + 1))
            o_stage[half][rows, :] = o[:, lanes]
            lse_stage[half][rows, :] = lse[:, lanes]
        return carry

    if dil == 1:
        residue(0, 0)
    else:
        lax.fori_loop(0, dil, residue, 0)
    for half in range(2):
        lanes = slice(LANES * half, LANES * (half + 1))
        o_ref[:, lanes] = o_stage[half][...]
        lse_ref[:, lanes] = lse_stage[half][...]


def prompt_attention(g, qn, kn, qkv):
    _, dil = ATT_GROUPS[g]
    span = N_KEYS * dil
    n_span = SEQ // span
    halves_per_group = GROUP_W // LANES
    cur = lambda col: (lambda b, a: (b * n_span + a, col))
    prev = lambda col: (lambda b, a: (b * n_span + jnp.maximum(a - 1, 0), col))
    blk = (span, LANES)
    qk_cols = [halves_per_group * g + half for half in range(halves_per_group)]
    v_cols = [halves_per_group * (2 * N_GROUPS + g) + half for half in range(halves_per_group)]
    in_specs = ([pl.BlockSpec(blk, cur(c)) for c in qk_cols] + [pl.BlockSpec(blk, prev(c)) for c in qk_cols]
                + [pl.BlockSpec(blk, cur(c)) for c in qk_cols] + [pl.BlockSpec(blk, prev(c)) for c in v_cols]
                + [pl.BlockSpec(blk, cur(c)) for c in v_cols])
    out = pl.BlockSpec((span, GROUP_W), lambda b, a: (b * n_span + a, 0))
    return _call(
        functools.partial(_attn_body, dil=dil), grid=(BATCH, n_span),
        in_specs=in_specs, out_specs=[out, out],
        out_shape=[jax.ShapeDtypeStruct((M_PROMPT, GROUP_W), F32)] * 2,
        scratch_shapes=[pltpu.VMEM((span, LANES), F32)] * 4,
        name="prompt_attention")(qn, qn, kn, kn, kn, kn, qkv, qkv, qkv, qkv)


def _dec_attn_body(*refs, dil, aliased):
    if aliased:
        qkv_ref, row_ref, ck_ref, cv_ref, _, _, o_ref, lse_ref, nk_ref, nv_ref = refs
    else:
        qkv_ref, row_ref, ck_ref, cv_ref, o_ref, lse_ref, nk_ref, nv_ref = refs
    bs = ck_ref.shape[1]
    ones = jnp.ones((8, LANES), BF16)
    n_chunk = ck_ref.shape[4] // LANES
    lane = lax.broadcasted_iota(jnp.int32, (1, LANES), 1)
    key_lane = (lane % dil) == 0
    not_last = lax.broadcasted_iota(jnp.int32, (HEAD_DIM, LANES), 1) < LANES - 1
    sample_row = lax.broadcasted_iota(jnp.int32, (qkv_ref.shape[1], LANES), 0)
    qkv_t = qkv_ref[...]
    for s in range(bs):
        sample = pl.program_id(0) * bs + s
        pick = (sample_row == sample).astype(BF16)
        cols = _dot_sel(qkv_t, pick, 3)
        sample_rows = row_ref[pl.ds(sample, 1), :]
        accs, own_heads, den_heads, lse_heads = [], [], [], []
        for h in range(HEADS):
            rows = slice(HEAD_DIM * h, HEAD_DIM * (h + 1))
            qb = cols[rows, :]
            kn = cols[GROUP_W + HEAD_DIM * h:GROUP_W + HEAD_DIM * (h + 1), :]
            vn = cols[2 * GROUP_W + HEAD_DIM * h:2 * GROUP_W + HEAD_DIM * (h + 1), :]
            chunk = lambda ref, c: ref[0, s, h, :, c * LANES:(c + 1) * LANES]
            scores = [jnp.where(key_lane, jnp.sum(chunk(ck_ref, c) * qb, axis=0, keepdims=True), NEG_BIG)
                      for c in range(n_chunk)]
            s_self = jnp.sum(kn * qb, axis=0, keepdims=True)
            m = s_self
            for sc in scores:
                m = jnp.maximum(m, jnp.max(sc, axis=1, keepdims=True))
            p_self = jnp.exp(s_self - m)
            l = p_self
            acc = None
            for c in range(n_chunk):
                p = jnp.exp(scores[c] - m)
                l = l + jnp.sum(p, axis=1, keepdims=True)
                pv = p * chunk(cv_ref, c)
                acc = pv if acc is None else acc + pv
            accs.append(acc)
            own_heads.append(p_self[:, :HEAD_DIM])
            den_heads.append(l[:, :HEAD_DIM])
            lse_heads.append((m + jnp.log(l))[:, :HEAD_DIM])
            for src, dst, new in ((ck_ref, nk_ref, kn), (cv_ref, nv_ref, vn)):
                rolled = [pltpu.roll(chunk(src, c), LANES - 1, axis=1) for c in range(n_chunk)] + [new]
                for c in range(n_chunk):
                    dst[0, s, h, :, c * LANES:(c + 1) * LANES] = jnp.where(not_last, rolled[c], rolled[c + 1])
        acc_t, rest = None, jnp.concatenate(accs, axis=0)
        for _ in range(3):
            piece = rest.astype(BF16)
            rest = rest - piece.astype(F32)
            part = lax.dot_general(ones, piece, (((1,), (1,)), ((), ())), preferred_element_type=F32)
            acc_t = part if acc_t is None else acc_t + part
        own = jnp.concatenate(own_heads, axis=1) * sample_rows[:, 2 * GROUP_W:]
        o_ref[pl.ds(sample, 1), :] = (acc_t[0:1, :] + own) / jnp.concatenate(den_heads, axis=1)
        lse_ref[pl.ds(sample, 1), :] = jnp.concatenate(lse_heads, axis=1)


def decode_attention(layer, g, qkv_rows, cache_k, cache_v, prev_k, prev_v, bs):
    window, dil = ATT_GROUPS[g]
    cblk = (1, bs, HEADS, HEAD_DIM, window)
    cmap = lambda i: (layer, i, 0, 0, 0)
    whole = lambda shape: pl.BlockSpec(shape, lambda i: (0, 0))
    in_specs = [whole((3 * GROUP_W, DEC_BATCH)), whole((DEC_BATCH, 3 * GROUP_W)),
                pl.BlockSpec(cblk, cmap), pl.BlockSpec(cblk, cmap)]
    args = [qkv_rows.T, qkv_rows, cache_k, cache_v]
    aliases = None
    if prev_k is not None:
        in_specs += [pl.BlockSpec(memory_space=pl.ANY), pl.BlockSpec(memory_space=pl.ANY)]
        args += [prev_k, prev_v]
        aliases = {4: 2, 5: 3}
    out = whole((DEC_BATCH, GROUP_W))
    return _call(
        functools.partial(_dec_attn_body, dil=dil, aliased=prev_k is not None), grid=(DEC_BATCH // bs,),
        in_specs=in_specs,
        out_specs=[out, out, pl.BlockSpec(cblk, cmap), pl.BlockSpec(cblk, cmap)],
        out_shape=[jax.ShapeDtypeStruct((DEC_BATCH, GROUP_W), F32)] * 2
        + [jax.ShapeDtypeStruct(cache_k.shape, F32)] * 2,
        input_output_aliases=aliases, name="decode_attention")(*args)


def _mix_body(o0_ref, o1_ref, o2_ref, l0_ref, l1_ref, l2_ref, att_ref):
    l0, l1, l2 = l0_ref[...], l1_ref[...], l2_ref[...]
    m = jnp.maximum(jnp.maximum(l0, l1), l2)
    e0, e1, e2 = jnp.exp(l0 - m), jnp.exp(l1 - m), jnp.exp(l2 - m)
    num = e0 * o0_ref[...] + e1 * o1_ref[...] + e2 * o2_ref[...]
    att_ref[...] = (num / (e0 + e1 + e2)).astype(att_ref.dtype)


def mix_groups(outs, lses, tm, out_dtype):
    m = outs[0].shape[0]
    spec = pl.BlockSpec((tm, GROUP_W), lambda i: (i, 0))
    return _call(_mix_body, grid=(m // tm,), in_specs=[spec] * 6, out_specs=spec,
                 out_shape=jax.ShapeDtypeStruct((m, GROUP_W), out_dtype), name="mix_groups")(*outs, *lses)


def _s5_prompt_body(u_ref, mt_ref, ms_ref, mo_ref, et_ref, es_ref, a1_ref, a2_ref, y_ref, hl_ref,
                    s_ref, hp_ref, toep_ref, wst_ref, wout_ref):
    @pl.when(pl.program_id(1) == 0)
    def _():
        def expand(m_ref, e_ref, row_div, col_div, dst_ref):
            full = _dot(m_ref[0].astype(BF16), e_ref[...])
            row_g = (lax.broadcasted_iota(jnp.int32, full.shape, 0) // row_div) % S5_PACK
            col_g = (lax.broadcasted_iota(jnp.int32, full.shape, 1) // col_div) % S5_PACK
            dst_ref[...] = jnp.where(row_g == col_g, full, 0.0).astype(BF16)

        expand(mt_ref, et_ref, S5_CH, S5_CH, toep_ref)
        expand(ms_ref, es_ref, S5_CH, S5_STATE, wst_ref)
        expand(mo_ref, et_ref, S5_STATE, S5_CH, wout_ref)

    n_chunk = s_ref.shape[0]
    u8 = jnp.concatenate([u_ref[pl.ds(s, n_chunk, stride=S5_CHUNK), :] for s in range(S5_CHUNK)],
                         axis=1).astype(BF16)
    s_ref[...] = _dot(u8, wst_ref[...])
    a1, a2 = a1_ref[0], a2_ref[0]
    half = a1.shape[1] // 2

    def step(c, h):
        hp_ref[pl.ds(c, 1), :] = h
        swapped = jnp.concatenate([h[:, half:], h[:, :half]], axis=1)
        return a1 * h + a2 * swapped + s_ref[pl.ds(c, 1), :]

    hl_ref[0, 0] = lax.fori_loop(0, n_chunk, step, jnp.zeros((1, 2 * half), F32))
    y8 = _dot(u8, toep_ref[...]) + _dot(hp_ref[...].astype(BF16), wout_ref[...])
    for t in range(S5_CHUNK):
        y_ref[pl.ds(t, n_chunk, stride=S5_CHUNK), :] = y8[:, t * LANES:(t + 1) * LANES]


def s5_prompt_weights(lam_re, lam_im, log_dt, b_re, b_im, c_re, c_im):
    chunk = S5_CHUNK
    dt = jnp.exp(log_dt)[:, None]
    tau = jnp.arange(chunk + 1, dtype=F32)[:, None, None]
    mag = jnp.exp(lam_re * dt * tau)
    pw_r = mag * jnp.cos(lam_im * dt * tau)
    pw_i = mag * jnp.sin(lam_im * dt * tau)
    abar_r, abar_i = pw_r[1], pw_i[1]
    den = lam_re * lam_re + lam_im * lam_im
    fr = ((abar_r - 1.0) * lam_re + abar_i * lam_im) / den
    fi = (abar_i * lam_re - (abar_r - 1.0) * lam_im) / den
    bbr = fr[..., None] * b_re - fi[..., None] * b_im
    bbi = fr[..., None] * b_im + fi[..., None] * b_re
    ab_r = pw_r[..., None] * bbr - pw_i[..., None] * bbi
    ab_i = pw_r[..., None] * bbi + pw_i[..., None] * bbr
    kern = jnp.sum(c_re[None, :, :, :, None] * ab_r[:chunk, :, None] - c_im[None, :, :, :, None] * ab_i[:chunk, :, None],
                   axis=3)
    s_idx = jnp.arange(chunk)[:, None]
    t_idx = jnp.arange(chunk)[None, :]
    lag = t_idx - s_idx
    k_st = kern[jnp.maximum(lag, 0)]
    k_st = jnp.where((lag >= 0)[:, :, None, None, None], k_st, 0.0)
    split = lambda x, axis: x.reshape(x.shape[:axis] + (N_SUPER, S5_PACK) + x.shape[axis + 1:])
    width = chunk * LANES
    n_state = S5_PACK * S5_STATE
    m_toep = split(k_st, 2).transpose(2, 0, 3, 5, 1, 4).reshape(N_SUPER, width, LANES)
    rev = chunk - 1 - jnp.arange(chunk)
    ab_rev = jnp.stack([ab_r[rev], ab_i[rev]])
    m_st = split(ab_rev, 2).transpose(2, 1, 3, 5, 0, 4).reshape(N_SUPER, width, LANES)
    ar, ai = pw_r[1:, :, None, :], pw_i[1:, :, None, :]
    co_r = c_re[None] * ar - c_im[None] * ai
    co_i = -c_re[None] * ai - c_im[None] * ar
    co = jnp.stack([co_r, co_i])
    m_out = split(co, 2).transpose(2, 0, 3, 5, 1, 4).reshape(N_SUPER, 2 * n_state, LANES)
    src = jnp.arange(LANES)[:, None]
    col = jnp.arange(width)[None, :]
    e_to = ((src // S5_CH == col // LANES) & (src % S5_CH == col % S5_CH)).astype(BF16)
    e_st = ((src // S5_STATE == col // n_state) & (src % S5_STATE == col % S5_STATE)).astype(BF16)
    last_r = pw_r[chunk].reshape(N_SUPER, 1, n_state)
    last_i = pw_i[chunk].reshape(N_SUPER, 1, n_state)
    a1 = jnp.concatenate([last_r, last_r], axis=2)
    a2 = jnp.concatenate([-last_i, last_i], axis=2)
    return m_toep, m_st, m_out, e_to, e_st, a1, a2


def s5_prompt(mid, weights):
    n_chunk = SEQ // S5_CHUNK
    n_state2 = 2 * S5_PACK * S5_STATE
    width = S5_CHUNK * LANES
    assert width == n_state2
    seq_blk = pl.BlockSpec((SEQ, LANES), lambda g, b: (b, g))
    per_g = lambda shape: pl.BlockSpec((1,) + shape, lambda g, b: (g, 0, 0))
    spread = pl.BlockSpec((LANES, width), lambda g, b: (0, 0))
    y, h_last = _call(
        _s5_prompt_body, grid=(N_SUPER, BATCH),
        in_specs=[seq_blk, per_g((width, LANES)), per_g((width, LANES)), per_g((n_state2, LANES)),
                  spread, spread, per_g((1, n_state2)), per_g((1, n_state2))],
        out_specs=[seq_blk, pl.BlockSpec((1, 1, 1, n_state2), lambda g, b: (b, g, 0, 0))],
        out_shape=[jax.ShapeDtypeStruct((M_PROMPT, S5_WIDTH), F32),
                   jax.ShapeDtypeStruct((BATCH, N_SUPER, 1, n_state2), F32)],
        scratch_shapes=[pltpu.VMEM((n_chunk, n_state2), F32), pltpu.VMEM((n_chunk, n_state2), F32)]
        + [pltpu.VMEM((width, width), BF16)] * 3,
        name="s5_prompt")(mid, *weights)
    h = h_last.reshape(BATCH, N_SUPER, 2, S5_PACK, S5_STATE).transpose(0, 1, 3, 4, 2)
    return y, h.reshape(BATCH, S5_GROUPS, S5_STATE, 2)


def _s5_post_body(y_ref, u_ref, d_ref, w_ref, b_ref, o_ref):
    z = _gelu(y_ref[...] + d_ref[...] * u_ref[...])
    lin = _dot(z.astype(BF16), w_ref[...].astype(BF16))
    o_ref[...] = (z * _sigmoid(lin + b_ref[...])).astype(o_ref.dtype)


def s5_post(y, mid, d_skip, w_glu, b_glu, tm):
    m = y.shape[0]
    row = lambda i: (i, 0)
    fixed = lambda i: (0, 0)
    return _call(
        _s5_post_body, grid=(m // tm,),
        in_specs=[pl.BlockSpec((tm, S5_WIDTH), row), pl.BlockSpec((tm, S5_WIDTH), row),
                  pl.BlockSpec((1, S5_WIDTH), fixed), pl.BlockSpec((S5_WIDTH, S5_WIDTH), fixed),
                  pl.BlockSpec((1, S5_WIDTH), fixed)],
        out_specs=pl.BlockSpec((tm, S5_WIDTH), row),
        out_shape=jax.ShapeDtypeStruct((m, S5_WIDTH), BF16),
        name="s5_post")(y, mid, d_skip.reshape(1, -1), w_glu, b_glu.reshape(1, -1))


def _s5_decode_body(u_ref, hr_ref, hi_ref, ar_ref, ai_ref, bb_ref, cc_ref, y_ref, sr_ref, si_ref):
    n = hr_ref.shape[1]
    bu = _dot(u_ref[...].astype(BF16), bb_ref[...])
    ar, ai = ar_ref[...], ai_ref[...]
    hr, hi = hr_ref[...], hi_ref[...]
    sr = bu[:, :n] + (ar * hr - ai * hi)
    si = bu[:, n:] + (ar * hi + ai * hr)
    sr_ref[...] = sr
    si_ref[...] = si
    y_ref[...] = _dot(jnp.concatenate([sr, si], axis=1).astype(BF16), cc_ref[...])


def s5_decode(mid, state, lam_re, lam_im, log_dt, b_re, b_im, c_re, c_im):
    dt = jnp.exp(log_dt)[:, None]
    mag = jnp.exp(lam_re * dt)
    abar_r, abar_i = mag * jnp.cos(lam_im * dt), mag * jnp.sin(lam_im * dt)
    den = lam_re * lam_re + lam_im * lam_im
    fr = ((abar_r - 1.0) * lam_re + abar_i * lam_im) / den
    fi = (abar_i * lam_re - (abar_r - 1.0) * lam_im) / den
    bbr = fr[..., None] * b_re - fi[..., None] * b_im
    bbi = fr[..., None] * b_im + fi[..., None] * b_re
    eye = jnp.eye(S5_GROUPS, dtype=F32)
    n_state = S5_GROUPS * S5_STATE
    in_map = lambda w: jnp.einsum("gpc,gk->gckp", w, eye).reshape(S5_WIDTH, n_state)
    out_map = lambda w: jnp.einsum("gop,gk->gpko", w, eye).reshape(n_state, S5_WIDTH)
    bb = jnp.concatenate([in_map(bbr), in_map(bbi)], axis=1).astype(BF16)
    cc = jnp.concatenate([out_map(c_re), -out_map(c_im)], axis=0).astype(BF16)
    nb = DEC_BATCH
    fixed = lambda i: (0, 0)
    st = pl.BlockSpec((nb, n_state), fixed)
    vec = pl.BlockSpec((1, n_state), fixed)
    y, sr, si = _call(
        _s5_decode_body, grid=(1,),
        in_specs=[pl.BlockSpec((nb, S5_WIDTH), fixed), st, st, vec, vec,
                  pl.BlockSpec((S5_WIDTH, 2 * n_state), fixed), pl.BlockSpec((2 * n_state, S5_WIDTH), fixed)],
        out_specs=[pl.BlockSpec((nb, S5_WIDTH), fixed), st, st],
        out_shape=[jax.ShapeDtypeStruct((nb, S5_WIDTH), F32), jax.ShapeDtypeStruct((nb, n_state), F32),
                   jax.ShapeDtypeStruct((nb, n_state), F32)],
        name="s5_decode")(mid, state[..., 0].reshape(nb, n_state), state[..., 1].reshape(nb, n_state),
                          abar_r.reshape(1, n_state), abar_i.reshape(1, n_state), bb, cc)
    new_state = jnp.stack([sr.reshape(nb, S5_GROUPS, S5_STATE), si.reshape(nb, S5_GROUPS, S5_STATE)], axis=-1)
    return y, new_state


def _lru_gates(xc, wa_ref, ba_ref, wi_ref, bi_ref, c8_ref):
    xb = xc.astype(BF16)
    r = _sigmoid(_dot(xb, wa_ref[...]) + ba_ref[...])
    ig = _sigmoid(_dot(xb, wi_ref[...]) + bi_ref[...])
    log_a = c8_ref[...] * r
    a = jnp.exp(log_a)
    b = jnp.sqrt(-jnp.tanh(log_a) * (a * a + 1.0)) * (ig * xc)
    return a, b


def _lru_pre_body(x_ref, p_ref, cw_ref, cb_ref, wa_ref, ba_ref, wi_ref, bi_ref, c8_ref,
                  a_ref, b_ref, ext_ref, *, tiles_per_seq):
    tm = x_ref.shape[0]
    first = pl.program_id(0) % tiles_per_seq == 0
    ext_ref[0:8, :] = jnp.where(first, 0.0, p_ref[...])
    ext_ref[8:, :] = x_ref[...]
    xc = cb_ref[...] + cw_ref[3:4, :] * x_ref[...]
    for k in (1, 2, 3):
        xc = xc + cw_ref[3 - k:4 - k, :] * ext_ref[8 - k:8 - k + tm, :]
    a, b = _lru_gates(xc, wa_ref, ba_ref, wi_ref, bi_ref, c8_ref)
    a_ref[...] = a
    b_ref[...] = b


def _lru_scan_body(a_ref, b_ref, g_ref, y_ref, hl_ref, h_ref, p_ref):
    t_len = a_ref.shape[0]
    seg_len = t_len // LRU_SEG
    n_q = LRU_SEG // 8

    def step(t, carry):
        out = []
        for qi in range(n_q):
            h, p = carry[2 * qi], carry[2 * qi + 1]
            rows = pl.ds(qi * 8 * seg_len + t, 8, stride=seg_len)
            at = a_ref[rows, :]
            h = at * h + b_ref[rows, :]
            p = at * p
            h_ref[rows, :] = h
            p_ref[rows, :] = p
            out += [h, p]
        return tuple(out)

    init = (jnp.zeros((8, LANES), F32), jnp.ones((8, LANES), F32)) * n_q
    lax.fori_loop(0, seg_len, step, init)
    carry = jnp.zeros((1, LANES), F32)
    for s in range(LRU_SEG):
        rows = slice(s * seg_len, (s + 1) * seg_len)
        h = h_ref[rows, :] + p_ref[rows, :] * carry
        y_ref[rows, :] = (h * _gelu(g_ref[rows, :])).astype(y_ref.dtype)
        carry = h[seg_len - 1:seg_len, :]
    hl_ref[0] = carry


def lru_weights(conv_w, conv_b, w_a, b_a, w_i, b_i, lam):
    eye = jnp.eye(LRU_HEADS, dtype=F32)
    bd = lambda w: jnp.einsum("hij,hk->hikj", w, eye).reshape(LRU_WIDTH, LRU_WIDTH).astype(BF16)
    c8 = (-LRU_C) * jax.nn.softplus(-lam)
    return (conv_w, conv_b.reshape(1, -1), bd(w_a), b_a.reshape(1, -1), bd(w_i), b_i.reshape(1, -1),
            c8.reshape(1, -1))


def lru_prompt(mid, lw, tm):
    conv_w, conv_b, wa, ba, wi, bi, c8 = lw
    fixed = lambda i: (0, 0)
    vec = pl.BlockSpec((1, LRU_WIDTH), fixed)
    mat = pl.BlockSpec((LRU_WIDTH, LRU_WIDTH), fixed)
    a, b = _call(
        functools.partial(_lru_pre_body, tiles_per_seq=SEQ // tm), grid=(M_PROMPT // tm,),
        in_specs=[pl.BlockSpec((tm, LRU_WIDTH), lambda i: (i, 1)),
                  pl.BlockSpec((8, LRU_WIDTH), lambda i: (jnp.maximum(i * (tm // 8) - 1, 0), 1)),
                  pl.BlockSpec((4, LRU_WIDTH), fixed), vec, mat, vec, mat, vec, vec],
        out_specs=[pl.BlockSpec((tm, LRU_WIDTH), lambda i: (i, 0))] * 2,
        out_shape=[jax.ShapeDtypeStruct((M_PROMPT, LRU_WIDTH), F32)] * 2,
        scratch_shapes=[pltpu.VMEM((tm + 8, LRU_WIDTH), F32)],
        name="lru_pre")(mid, mid, conv_w, conv_b, wa, ba, wi, bi, c8)
    n_lane = LRU_WIDTH // LANES
    g_col0 = (S5_WIDTH + LRU_WIDTH) // LANES
    blk = lambda off: pl.BlockSpec((SEQ, LANES), lambda bi_, c: (bi_, off + c))
    y, h_last = _call(
        _lru_scan_body, grid=(BATCH, n_lane),
        in_specs=[blk(0), blk(0), blk(g_col0)],
        out_specs=[blk(0), pl.BlockSpec((1, 1, LANES), lambda bi_, c: (bi_, 0, c))],
        out_shape=[jax.ShapeDtypeStruct((M_PROMPT, LRU_WIDTH), BF16),
                   jax.ShapeDtypeStruct((BATCH, 1, LRU_WIDTH), F32)],
        scratch_shapes=[pltpu.VMEM((SEQ, LANES), F32), pltpu.VMEM((SEQ, LANES), F32)],
        name="lru_scan")(a, b, mid)
    return y, h_last.reshape(BATCH, LRU_WIDTH)


def _lru_decode_body(x_ref, g_ref, c0_ref, c1_ref, c2_ref, h0_ref, cw_ref, cb_ref, wa_ref, ba_ref,
                     wi_ref, bi_ref, c8_ref, y_ref, h_ref):
    xc = (cb_ref[...] + cw_ref[0:1, :] * c0_ref[...] + cw_ref[1:2, :] * c1_ref[...]
          + cw_ref[2:3, :] * c2_ref[...] + cw_ref[3:4, :] * x_ref[...])
    a, b = _lru_gates(xc, wa_ref, ba_ref, wi_ref, bi_ref, c8_ref)
    h = a * h0_ref[...] + b
    h_ref[...] = h
    y_ref[...] = (h * _gelu(g_ref[...])).astype(y_ref.dtype)


def lru_decode(mid, conv_buf, h0, lw):
    conv_w, conv_b, wa, ba, wi, bi, c8 = lw
    nb = DEC_BATCH
    fixed = lambda i: (0, 0)
    vec = pl.BlockSpec((1, LRU_WIDTH), fixed)
    mat = pl.BlockSpec((LRU_WIDTH, LRU_WIDTH), fixed)
    st = pl.BlockSpec((nb, LRU_WIDTH), fixed)
    return _call(
        _lru_decode_body, grid=(1,),
        in_specs=[pl.BlockSpec((nb, LRU_WIDTH), lambda i: (0, 1)), pl.BlockSpec((nb, LRU_WIDTH), lambda i: (0, 2)),
                  st, st, st, st, pl.BlockSpec((4, LRU_WIDTH), fixed), vec, mat, vec, mat, vec, vec],
        out_specs=[st, st],
        out_shape=[jax.ShapeDtypeStruct((nb, LRU_WIDTH), BF16), jax.ShapeDtypeStruct((nb, LRU_WIDTH), F32)],
        name="lru_decode")(mid, mid, conv_buf[:, 0], conv_buf[:, 1], conv_buf[:, 2], h0,
                           conv_w, conv_b, wa, ba, wi, bi, c8)


TM = 1024
TM_SMALL = 512
TM_EXPERT = 512
M_ALL = M_PROMPT + DEC_BATCH
N_EXPERT_TILES = (2 * M_ALL) // TM_EXPERT + N_EXPERTS
TM_ROUTER = M_ALL // 16
DEC_ATT_BS = (16, 8, 2)


def _token_mixers(layer, xp, xs, p, caches_t, win_prev, state_s5, state_conv, state_lru):
    n_p = rmsnorm(xp, p["norm_mix"][layer], TM_SMALL, BF16)
    n_s = rmsnorm(xs, p["norm_mix"][layer], DEC_BATCH, BF16)
    w_in = p["w_in"]
    proj = functools.partial(matmul_ws, [n_p], [n_s], w_of_x=[0], tm=TM)
    qkv_p, qkv_s = proj([(w_in, (layer,))], n_cols=OFF_MID, tn=768, out_dtype=F32, epilogue=_ep_plain,
                        name="proj_qkv")
    mid_p, mid_s = proj([(w_in[layer, :, OFF_MID:OFF_GATE], ())], tn=640, out_dtype=F32, epilogue=_ep_plain,
                        name="proj_mid")
    gates_p, gates_s = proj([(w_in[layer, :, OFF_GATE:], ())], tn=1024, out_dtype=BF16, out_dtype_s=F32,
                            epilogue=_ep_sigmoid, name="proj_gates")

    qn_p, kn_p = qk_norm(qkv_p, p["q_gain"][layer], p["k_gain"][layer], TM_SMALL, F32)
    qn_s, kn_s = qk_norm(qkv_s, p["q_gain"][layer], p["k_gain"][layer], DEC_BATCH, F32)
    outs, lses, outs_s, lses_s = [], [], [], []
    new_p, win = [], []
    for g, (window, dil) in enumerate(ATT_GROUPS):
        cols = slice(GROUP_W * g, GROUP_W * (g + 1))
        v_cols = slice(2 * ATT_WIDTH + GROUP_W * g, 2 * ATT_WIDTH + GROUP_W * (g + 1))
        o, lse = prompt_attention(g, qn_p, kn_p, qkv_p)
        outs.append(o)
        lses.append(lse)
        keep = min(window, SEQ)
        tail = lambda x, cs: (x.reshape(BATCH, SEQ, x.shape[1])[:, SEQ - keep:, cs]
                              .reshape(BATCH, keep, HEADS, HEAD_DIM))
        new_p.append(tail(kn_p, cols))
        new_p.append(tail(qkv_p, v_cols))
        cols_s = jnp.concatenate([qn_s[:, cols], kn_s[:, cols], qkv_s[:, v_cols]], axis=1)
        prev_k, prev_v = (None, None) if win_prev is None else (win_prev[2 * g], win_prev[2 * g + 1])
        o_s, lse_s, wk, wv = decode_attention(layer, g, cols_s, caches_t[2 * g], caches_t[2 * g + 1],
                                              prev_k, prev_v, DEC_ATT_BS[g])
        outs_s.append(o_s)
        lses_s.append(lse_s)
        win += [wk, wv]
    att_p = mix_groups(outs, lses, TM_SMALL, BF16)
    att_s = mix_groups(outs_s, lses_s, DEC_BATCH, BF16)

    s5p = (p["s5_lam_re"][layer], p["s5_lam_im"][layer], p["s5_log_dt"][layer], p["s5_b_re"][layer],
           p["s5_b_im"][layer], p["s5_c_re"][layer], p["s5_c_im"][layer])
    y_p, h_p = s5_prompt(mid_p, s5_prompt_weights(*s5p))
    y_s, h_s = s5_decode(mid_s, state_s5[layer], *s5p)
    glu = (p["s5_d"][layer], p["s5_w_glu"][layer], p["s5_b_glu"][layer])
    s5_p = s5_post(y_p, mid_p, *glu, TM_SMALL)
    s5_s = s5_post(y_s, mid_s, *glu, DEC_BATCH)
    new_p.append(h_p)
    new_s = [h_s]

    lw = lru_weights(p["lru_conv_w"][layer], p["lru_conv_b"][layer], p["lru_w_a"][layer], p["lru_b_a"][layer],
                     p["lru_w_i"][layer], p["lru_b_i"][layer], p["lru_lam"][layer])
    lru_p, h_lp = lru_prompt(mid_p, lw, 512)
    lru_s, h_ls = lru_decode(mid_s, state_conv[layer], state_lru[layer], lw)
    x_lru = slice(S5_WIDTH, S5_WIDTH + LRU_WIDTH)
    new_p.append(mid_p[:, x_lru].reshape(BATCH, SEQ, LRU_WIDTH)[:, SEQ - 3:])
    new_p.append(h_lp)
    new_s.append(jnp.concatenate([state_conv[layer, :, 1:], mid_s[:, None, x_lru]], axis=1))
    new_s.append(h_ls)

    tn = 1024
    gate_tiles = D_MODEL // tn
    merged_p, merged_s = matmul_ws(
        [att_p, s5_p, lru_p], [att_s, s5_s, lru_s],
        [(p["w_br_attn"], (layer,)), (p["w_br_s5"], (layer,)), (p["w_br_lru"], (layer,))],
        w_of_x=[0, 1, 2], tm=TM, tn=tn, out_dtype=BF16, epilogue=_ep_merge,
        tile_extras=[(gates_p, gates_s, 0), (gates_p, gates_s, gate_tiles), (gates_p, gates_s, 2 * gate_tiles)],
        name="merge")
    xp, xs = matmul_ws([merged_p], [merged_s], [(p["w_o"], (layer,))], w_of_x=[0], tm=TM, tn=tn,
                       out_dtype=F32, epilogue=_ep_residual, tile_extras=[(xp, xs, 0)], name="out_proj")
    return xp, xs, new_p, new_s, win


def _dense_ffn(xp, xs, p, layer, j):
    n_p = rmsnorm(xp, p["norm_ffn"][layer], TM_SMALL, BF16)
    n_s = rmsnorm(xs, p["norm_ffn"][layer], DEC_BATCH, BF16)
    h_p, h_s = matmul_ws([n_p], [n_s], [(p["ffn_w_gate"], (j,)), (p["ffn_w_up"], (j,))], w_of_x=[0, 0],
                         tm=TM, tn=512, out_dtype=BF16, epilogue=_ep_swiglu, name="ffn_up")
    return matmul_ws([h_p], [h_s], [(p["ffn_w_down"], (j,))], w_of_x=[0], tm=512, tn=512, out_dtype=F32,
                     epilogue=_ep_residual, tile_extras=[(xp, xs, 0)], name="ffn_down")


def _moe_ffn(xp, xs, p, layer, j):
    router = (p["norm_ffn"][layer], p["moe_router_w"][j], p["moe_router_b"][j])
    n, comb, idx = rmsnorm_router(jnp.concatenate([xp, xs], axis=0), *router, TM_ROUTER)
    idx = idx[:, :2]
    m = M_ALL
    top_e = idx.reshape(-1)
    top_w = jnp.take_along_axis(comb, idx, axis=1).reshape(-1)
    counts = jnp.sum(top_e[:, None] == jnp.arange(N_EXPERTS)[None, :], axis=0).astype(jnp.int32)
    padded = ((counts + TM_EXPERT - 1) // TM_EXPERT) * TM_EXPERT
    pad_end = jnp.cumsum(padded)
    pad_start = pad_end - padded
    start = jnp.cumsum(counts) - counts
    order = jnp.argsort(top_e, stable=True).astype(jnp.int32)
    rank = jnp.argsort(order).astype(jnp.int32)
    dest = pad_start[top_e] + rank - start[top_e]
    n_rows = N_EXPERT_TILES * TM_EXPERT
    tile_start = jnp.arange(N_EXPERT_TILES, dtype=jnp.int32) * TM_EXPERT
    tile_expert = jnp.minimum(jnp.sum(tile_start[:, None] >= pad_end[None, :], axis=1), N_EXPERTS - 1)
    tile_expert = tile_expert.astype(jnp.int32)
    n_used = (pad_end[-1] // TM_EXPERT).reshape(1).astype(jnp.int32)
    row_e = jnp.repeat(tile_expert, TM_EXPERT)
    within = jnp.arange(n_rows, dtype=jnp.int32) - pad_start[row_e]
    row_valid = within < counts[row_e]
    row_pair = order[jnp.clip(start[row_e] + within, 0, 2 * m - 1)]
    src_token = jnp.where(row_valid, row_pair // 2, 0)
    row_w = jnp.where(row_valid, top_w[row_pair], 0.0)
    x_sorted = jnp.take(n, src_token, axis=0, mode="clip")
    gmm = functools.partial(grouped_matmul_ws, tile_expert, n_used, lead=(j,), tm=TM_EXPERT)
    gate = gmm(x_sorted, p["moe_w_gate"], tn=1408, out_dtype=BF16, epilogue=_ep_plain, name="moe_gate")
    h = gmm(x_sorted, p["moe_w_up"], tn=1408, out_dtype=BF16, epilogue=_ep_silu_times, tile_extras=[gate],
            name="moe_up")
    ys = gmm(h, p["moe_w_down"], tn=1024, out_dtype=F32, epilogue=_ep_scale_rows,
             col_extras=[row_w.reshape(-1, 1)], name="moe_down")
    dest2 = dest.reshape(m, 2)
    y0 = jnp.take(ys, dest2[:, 0], axis=0, mode="clip")
    y1 = jnp.take(ys, dest2[:, 1], axis=0, mode="clip")
    return xp + (y0[:M_PROMPT] + y1[:M_PROMPT]), xs + (y0[M_PROMPT:] + y1[M_PROMPT:])


def kernel(x_prompt, x_sample, cache_k_w128, cache_v_w128, cache_k_w512, cache_v_w512, cache_k_w2048, cache_v_w2048, state_s5, state_conv, state_lru, norm_mix, w_in, q_gain, k_gain, s5_lam_re, s5_lam_im, s5_log_dt, s5_b_re, s5_b_im, s5_c_re, s5_c_im, s5_d, s5_w_glu, s5_b_glu, lru_conv_w, lru_conv_b, lru_w_a, lru_b_a, lru_w_i, lru_b_i, lru_lam, w_br_attn, w_br_s5, w_br_lru, w_o, norm_ffn, ffn_w_gate, ffn_w_up, ffn_w_down, moe_router_w, moe_router_b, moe_w_gate, moe_w_up, moe_w_down):
    p = dict(norm_mix=norm_mix, w_in=w_in, q_gain=q_gain, k_gain=k_gain, s5_lam_re=s5_lam_re,
             s5_lam_im=s5_lam_im, s5_log_dt=s5_log_dt, s5_b_re=s5_b_re, s5_b_im=s5_b_im, s5_c_re=s5_c_re,
             s5_c_im=s5_c_im, s5_d=s5_d, s5_w_glu=s5_w_glu, s5_b_glu=s5_b_glu, lru_conv_w=lru_conv_w,
             lru_conv_b=lru_conv_b, lru_w_a=lru_w_a, lru_b_a=lru_b_a, lru_w_i=lru_w_i, lru_b_i=lru_b_i,
             lru_lam=lru_lam, w_br_attn=w_br_attn, w_br_s5=w_br_s5, w_br_lru=w_br_lru, w_o=w_o,
             norm_ffn=norm_ffn, ffn_w_gate=ffn_w_gate, ffn_w_up=ffn_w_up, ffn_w_down=ffn_w_down,
             moe_router_w=moe_router_w, moe_router_b=moe_router_b, moe_w_gate=moe_w_gate,
             moe_w_up=moe_w_up, moe_w_down=moe_w_down)
    caches_t = [jnp.transpose(c, (0, 1, 3, 4, 2)) for c in
                (cache_k_w128, cache_v_w128, cache_k_w512, cache_v_w512, cache_k_w2048, cache_v_w2048)]
    xp = x_prompt.reshape(M_PROMPT, D_MODEL)
    xs = x_sample.reshape(DEC_BATCH, D_MODEL)
    states_p = [[] for _ in range(9)]
    states_s = [[] for _ in range(3)]
    win = None
    for layer in range(DEPTH):
        xp, xs, new_p, new_s, win = _token_mixers(layer, xp, xs, p, caches_t, win, state_s5, state_conv,
                                                  state_lru)
        if layer % 2 == 0:
            xp, xs = _dense_ffn(xp, xs, p, layer, layer // 2)
        else:
            xp, xs = _moe_ffn(xp, xs, p, layer, layer // 2)
        for i in range(9):
            states_p[i].append(new_p[i])
        for i in range(3):
            states_s[i].append(new_s[i])
    y_prompt = xp.reshape(BATCH, SEQ, D_MODEL)
    y_sample = xs.reshape(DEC_BATCH, 1, D_MODEL)
    windows = [jnp.transpose(w, (0, 1, 4, 2, 3)) for w in win]
    return (y_prompt, y_sample, *[jnp.stack(a) for a in states_p], *windows,
            *[jnp.stack(a) for a in states_s])
```

```python
import functools
import math

import jax
import jax.numpy as jnp
from jax import lax
from jax.experimental import pallas as pl
from jax.experimental.pallas import tpu as pltpu

F32 = jnp.float32
BF16 = jnp.bfloat16

D_MODEL = 2048
BATCH = 2
SEQ = 4096
DEPTH = 2
DEC_BATCH = 128
HEAD_DIM = 64
HEADS = 4
GROUP_W = HEADS * HEAD_DIM
ATT_GROUPS = ((128, 1), (512, 4), (2048, 16))
N_GROUPS = 3
ATT_WIDTH = N_GROUPS * GROUP_W
N_KEYS = 128
S5_WIDTH = 640
S5_GROUPS = 40
S5_CH = 16
S5_STATE = 64
LRU_WIDTH = 640
LRU_HEADS = 8
LRU_C = 8.0
OFF_MID = 3 * ATT_WIDTH
MID_WIDTH = S5_WIDTH + 2 * LRU_WIDTH
OFF_GATE = OFF_MID + MID_WIDTH
N_EXPERTS = 8
D_FF_EXPERT = 2816
EPS = 1e-6
LANES = 128

M_PROMPT = BATCH * SEQ

S5_CHUNK = 8
S5_PACK = LANES // S5_CH
N_SUPER = S5_GROUPS // S5_PACK
LRU_SEG = 32
NEG_BIG = -1e30

VMEM_LIMIT_BYTES = 56 * 1024 * 1024


def _call(body, *, grid, in_specs, out_specs, out_shape, name, scratch_shapes=(), num_scalar_prefetch=0,
          input_output_aliases=None):
    params = pltpu.CompilerParams(dimension_semantics=("arbitrary",) * len(grid),
                                  vmem_limit_bytes=VMEM_LIMIT_BYTES)
    kwargs = {}
    if input_output_aliases:
        kwargs["input_output_aliases"] = input_output_aliases
    if num_scalar_prefetch:
        grid_spec = pltpu.PrefetchScalarGridSpec(
            num_scalar_prefetch=num_scalar_prefetch, grid=grid, in_specs=in_specs,
            out_specs=out_specs, scratch_shapes=scratch_shapes)
        return pl.pallas_call(body, grid_spec=grid_spec, out_shape=out_shape,
                              compiler_params=params, name=name, **kwargs)
    return pl.pallas_call(body, grid=grid, in_specs=in_specs, out_specs=out_specs,
                          out_shape=out_shape, scratch_shapes=scratch_shapes,
                          compiler_params=params, name=name, **kwargs)


def _dot(a, b):
    return jnp.dot(a, b, preferred_element_type=F32)


def _dot_sel(x, sel, parts=2):
    acc = None
    r = x
    for _ in range(parts):
        piece = r.astype(BF16)
        r = r - piece.astype(F32)
        d = _dot(piece, sel)
        acc = d if acc is None else acc + d
    return acc


def _sigmoid(x):
    return 1.0 / (1.0 + jnp.exp(-x))


def _gelu(x):
    c = math.sqrt(2.0 / math.pi)
    return 0.5 * x * (1.0 + jnp.tanh(c * (x + 0.044715 * (x * x * x))))


def _rmsnorm_body(x_ref, g_ref, o_ref):
    x = x_ref[...]
    y = x * lax.rsqrt(jnp.mean(x * x, axis=-1, keepdims=True) + EPS)
    o_ref[...] = (y * g_ref[...]).astype(o_ref.dtype)


def _rmsnorm_router_body(x_ref, g_ref, rw_ref, rb_ref, o_ref, comb_ref, idx_ref):
    x = x_ref[...]
    y = x * lax.rsqrt(jnp.mean(x * x, axis=-1, keepdims=True) + EPS)
    xn = y * g_ref[...]
    o_ref[...] = xn.astype(o_ref.dtype)
    logits = _dot(xn.astype(BF16), rw_ref[...].astype(BF16)) + rb_ref[...]
    lane = lax.broadcasted_iota(jnp.int32, logits.shape, 1).astype(F32)
    m1 = jnp.max(logits, axis=1, keepdims=True)
    i1 = jnp.min(jnp.where(logits == m1, lane, float(LANES)), axis=1, keepdims=True)
    rest = jnp.where(lane == i1, -jnp.inf, logits)
    m2 = jnp.max(rest, axis=1, keepdims=True)
    i2 = jnp.min(jnp.where(rest == m2, lane, float(LANES)), axis=1, keepdims=True)
    e = jnp.exp(m2 - m1)
    w1 = 1.0 / (1.0 + e)
    w2 = e / (1.0 + e)
    comb_ref[...] = jnp.where(lane == i1, w1, 0.0) + jnp.where(lane == i2, w2, 0.0)
    idx_ref[...] = jnp.where(lane == 0.0, i1, jnp.where(lane == 1.0, i2, 0.0)).astype(jnp.int32)


def rmsnorm(x, gain, tm, out_dtype):
    m, d = x.shape
    return _call(
        _rmsnorm_body, grid=(m // tm,),
        in_specs=[pl.BlockSpec((tm, d), lambda i: (i, 0)), pl.BlockSpec((1, d), lambda i: (0, 0))],
        out_specs=pl.BlockSpec((tm, d), lambda i: (i, 0)),
        out_shape=jax.ShapeDtypeStruct((m, d), out_dtype), name="rmsnorm")(x, gain.reshape(1, d))


def rmsnorm_router(x, gain, router_w, router_b, tm):
    m, d = x.shape
    ne = router_w.shape[1]
    rw = jnp.zeros((d, LANES), F32).at[:, :ne].set(router_w)
    rb = jnp.full((1, LANES), NEG_BIG, F32).at[0, :ne].set(router_b)
    row = lambda i: (i, 0)
    fixed = lambda i: (0, 0)
    return _call(
        _rmsnorm_router_body, grid=(m // tm,),
        in_specs=[pl.BlockSpec((tm, d), row), pl.BlockSpec((1, d), fixed),
                  pl.BlockSpec((d, LANES), fixed), pl.BlockSpec((1, LANES), fixed)],
        out_specs=[pl.BlockSpec((tm, d), row), pl.BlockSpec((tm, LANES), row), pl.BlockSpec((tm, LANES), row)],
        out_shape=[jax.ShapeDtypeStruct((m, d), F32), jax.ShapeDtypeStruct((m, LANES), F32),
                   jax.ShapeDtypeStruct((m, LANES), jnp.int32)],
        name="rmsnorm_router")(x, gain.reshape(1, d), rw, rb)


def _mm_body(*refs, n_x, w_of_x, n_te, n_re, epilogue, n_ptiles):
    n_w = len(w_of_x)
    pos = 0
    xp_refs = refs[pos:pos + n_x]; pos += n_x
    xs_refs = refs[pos:pos + n_x]; pos += n_x
    w_refs = refs[pos:pos + n_w]; pos += n_w
    tep_refs = refs[pos:pos + n_te]; pos += n_te
    tes_refs = refs[pos:pos + n_te]; pos += n_te
    re_refs = refs[pos:pos + n_re]; pos += n_re
    op_ref, os_ref = refs[pos], refs[pos + 1]; pos += 2
    wb_refs = refs[pos:pos + n_w]
    i = pl.program_id(1)

    @pl.when(i == 0)
    def _():
        for w_ref, wb_ref in zip(w_refs, wb_refs):
            wb_ref[...] = w_ref[...].astype(BF16)

    def rows(x_refs, te_refs, o_ref):
        accs = [_dot(x_refs[xi][...].astype(BF16), wb_ref[...]) for xi, wb_ref in zip(w_of_x, wb_refs)]
        extras = [e[...] for e in te_refs] + [e[...] for e in re_refs]
        o_ref[...] = epilogue(accs, extras).astype(o_ref.dtype)

    @pl.when(i < n_ptiles)
    def _():
        rows(xp_refs, tep_refs, op_ref)

    @pl.when(i == n_ptiles)
    def _():
        rows(xs_refs, tes_refs, os_ref)


def matmul_ws(xps, xss, ws, *, w_of_x, tm, tn, out_dtype, epilogue, tile_extras=(), row_extras=(), name,
              out_dtype_s=None, n_cols=None):
    mp = xps[0].shape[0]
    ms = xss[0].shape[0]
    n = n_cols or ws[0][0].shape[-1]
    n_ptiles = mp // tm
    last = n_ptiles - 1
    grid = (n // tn, n_ptiles + 1)
    in_specs = [pl.BlockSpec((tm, x.shape[1]), lambda j, i: (jnp.minimum(i, last), 0)) for x in xps]
    in_specs += [pl.BlockSpec((ms, x.shape[1]), lambda j, i: (0, 0)) for x in xss]
    scratch = []
    for w, lead in ws:
        k = w.shape[-2]
        block = (None,) * len(lead) + (k, tn)
        in_specs.append(pl.BlockSpec(block, lambda j, i, lead=lead: lead + (0, j)))
        scratch.append(pltpu.VMEM((k, tn), BF16))
    for _, _, off in tile_extras:
        in_specs.append(pl.BlockSpec((tm, tn), lambda j, i, off=off: (jnp.minimum(i, last), off + j)))
    for _, _, off in tile_extras:
        in_specs.append(pl.BlockSpec((ms, tn), lambda j, i, off=off: (0, off + j)))
    for _ in row_extras:
        in_specs.append(pl.BlockSpec((1, tn), lambda j, i: (0, j)))
    body = functools.partial(_mm_body, n_x=len(xps), w_of_x=tuple(w_of_x), n_te=len(tile_extras),
                             n_re=len(row_extras), epilogue=epilogue, n_ptiles=n_ptiles)
    return _call(
        body, grid=grid, in_specs=in_specs,
        out_specs=[pl.BlockSpec((tm, tn), lambda j, i: (jnp.minimum(i, last), j)),
                   pl.BlockSpec((ms, tn), lambda j, i: (0, j))],
        out_shape=[jax.ShapeDtypeStruct((mp, n), out_dtype),
                   jax.ShapeDtypeStruct((ms, n), out_dtype_s or out_dtype)],
        scratch_shapes=scratch, name=name,
    )(*xps, *xss, *[w for w, _ in ws], *[a for a, _, _ in tile_extras], *[a for _, a, _ in tile_extras],
      *row_extras)


def _ep_plain(accs, extras):
    return accs[0]


def _ep_sigmoid(accs, extras):
    return _sigmoid(accs[0])


def _ep_residual(accs, extras):
    return extras[0] + accs[0]


def _ep_swiglu(accs, extras):
    g, u = accs
    return (g * _sigmoid(g)) * u


def _ep_merge(accs, extras):
    return (extras[0].astype(F32) * accs[0] + extras[1].astype(F32) * accs[1]
            + extras[2].astype(F32) * accs[2])


def _gmm_body(te_ref, nu_ref, *refs, n_te, n_ce, epilogue):
    x_ref, w_ref = refs[0], refs[1]
    extra_refs = refs[2:2 + n_te + n_ce]
    o_ref = refs[2 + n_te + n_ce]
    wb_ref = refs[3 + n_te + n_ce]
    i = pl.program_id(1)
    new_expert = jnp.logical_or(i == 0, te_ref[i] != te_ref[jnp.maximum(i - 1, 0)])

    @pl.when(new_expert)
    def _():
        wb_ref[...] = w_ref[...].astype(BF16)

    @pl.when(i < nu_ref[0])
    def _():
        acc = _dot(x_ref[...].astype(BF16), wb_ref[...])
        o_ref[...] = epilogue([acc], [e[...] for e in extra_refs]).astype(o_ref.dtype)

    @pl.when(i >= nu_ref[0])
    def _():
        o_ref[...] = jnp.zeros(o_ref.shape, o_ref.dtype)


def grouped_matmul_ws(tile_expert, n_used, x, w, lead, *, tm, tn, out_dtype, epilogue, tile_extras=(),
                      col_extras=(), name):
    m = x.shape[0]
    k, n = w.shape[-2:]
    grid = (n // tn, m // tm)
    block = (None,) * (len(lead) + 1) + (k, tn)
    in_specs = [pl.BlockSpec((tm, x.shape[1]), lambda j, i, te, nu: (i, 0)),
                pl.BlockSpec(block, lambda j, i, te, nu: lead + (te[i], 0, j))]
    in_specs += [pl.BlockSpec((tm, tn), lambda j, i, te, nu: (i, j)) for _ in tile_extras]
    in_specs += [pl.BlockSpec((tm, 1), lambda j, i, te, nu: (i, 0)) for _ in col_extras]
    body = functools.partial(_gmm_body, n_te=len(tile_extras), n_ce=len(col_extras), epilogue=epilogue)
    return _call(
        body, grid=grid, in_specs=in_specs,
        out_specs=pl.BlockSpec((tm, tn), lambda j, i, te, nu: (i, j)),
        out_shape=jax.ShapeDtypeStruct((m, n), out_dtype), scratch_shapes=[pltpu.VMEM((k, tn), BF16)],
        num_scalar_prefetch=2, name=name,
    )(tile_expert, n_used, x, w, *tile_extras, *col_extras)


def _ep_silu_times(accs, extras):
    g = extras[0].astype(F32)
    return (g * _sigmoid(g)) * accs[0]


def _ep_scale_rows(accs, extras):
    return extras[0] * accs[0]


def _qknorm_body(q_ref, k_ref, qg_ref, kg_ref, seg_ref, qo_ref, ko_ref):
    seg = seg_ref[...]
    for src, gain, dst, scale in ((q_ref, qg_ref, qo_ref, HEAD_DIM ** -0.5), (k_ref, kg_ref, ko_ref, 1.0)):
        for g in range(N_GROUPS):
            cols = slice(GROUP_W * g, GROUP_W * (g + 1))
            x = src[:, cols]
            ms = _dot_sel(x * x, seg, 3)
            y = x * lax.rsqrt(ms + EPS) * gain[:, cols]
            dst[:, cols] = (y * scale).astype(dst.dtype)


def qk_norm(qkv, q_gain, k_gain, tm, q_dtype):
    m = qkv.shape[0]
    head = jnp.arange(GROUP_W) // HEAD_DIM
    seg = jnp.where(head[:, None] == head[None, :], 1.0 / HEAD_DIM, 0.0).astype(BF16)
    qg = jnp.broadcast_to(q_gain[:, None, :], (N_GROUPS, HEADS, HEAD_DIM)).reshape(1, ATT_WIDTH)
    kg = jnp.broadcast_to(k_gain[:, None, :], (N_GROUPS, HEADS, HEAD_DIM)).reshape(1, ATT_WIDTH)
    fixed = lambda i: (0, 0)
    return _call(
        _qknorm_body, grid=(m // tm,),
        in_specs=[pl.BlockSpec((tm, ATT_WIDTH), lambda i: (i, 0)), pl.BlockSpec((tm, ATT_WIDTH), lambda i: (i, 1)),
                  pl.BlockSpec((1, ATT_WIDTH), fixed), pl.BlockSpec((1, ATT_WIDTH), fixed),
                  pl.BlockSpec((GROUP_W, GROUP_W), fixed)],
        out_specs=[pl.BlockSpec((tm, ATT_WIDTH), lambda i: (i, 0)), pl.BlockSpec((tm, ATT_WIDTH), lambda i: (i, 0))],
        out_shape=[jax.ShapeDtypeStruct((m, ATT_WIDTH), q_dtype), jax.ShapeDtypeStruct((m, ATT_WIDTH), F32)],
        name="qk_norm")(qkv, qkv, qg, kg, seg)


def _attn_body(*refs, dil):
    q_refs, kp_refs, kc_refs, vp_refs, vc_refs = (refs[2 * n:2 * n + 2] for n in range(5))
    o_ref, lse_ref = refs[10], refs[11]
    o_stage, lse_stage = refs[12:14], refs[14:16]
    a = pl.program_id(1)
    tq = N_KEYS
    r = lax.broadcasted_iota(jnp.int32, (HEADS * tq, 2 * tq), 0) % tq
    c = lax.broadcasted_iota(jnp.int32, (HEADS * tq, 2 * tq), 1)
    valid = (c >= r) & (c <= r + N_KEYS) & ((a > 0) | (c >= tq))
    head = lax.broadcasted_iota(jnp.int32, (1, GROUP_W), 1) // HEAD_DIM

    def residue(res, carry):
        rows = pl.ds(res, tq, stride=dil) if dil > 1 else slice(None)
        load = lambda halves: jnp.concatenate([halves[0][rows, :], halves[1][rows, :]], axis=1)
        q = load(q_refs)
        k2 = jnp.concatenate([load(kp_refs), load(kc_refs)], axis=0).astype(BF16)
        v2 = jnp.concatenate([load(vp_refs), load(vc_refs)], axis=0).astype(BF16)
        q4 = jnp.concatenate([jnp.where(head == h, q, 0.0) for h in range(HEADS)], axis=0).astype(BF16)
        s = lax.dot_general(q4, k2, (((1,), (1,)), ((), ())), preferred_element_type=F32)
        s = jnp.where(valid, s, NEG_BIG)
        m = jnp.max(s, axis=1, keepdims=True)
        p = jnp.exp(s - m)
        l = jnp.sum(p, axis=1, keepdims=True)
        o4 = _dot(p.astype(BF16), v2) / l
        lse4 = m + jnp.log(l)
        o = jnp.zeros((tq, GROUP_W), F32)
        lse = jnp.zeros((tq, GROUP_W), F32)
        for h in range(HEADS):
            blk = slice(h * tq, (h + 1) * tq)
            o = jnp.where(head == h, o4[blk, :], o)
            lse = jnp.where(head == h, lse4[blk, :], lse)
        for half in range(2):
            lanes = slice(LANES * half, LANES * (half + 1))
            o_stage[half][rows, :] = o[:, lanes]
            lse_stage[half][rows, :] = lse[:, lanes]
        return carry

    if dil == 1:
        residue(0, 0)
    else:
        lax.fori_loop(0, dil, residue, 0, unroll=2)
    for half in range(2):
        lanes = slice(LANES * half, LANES * (half + 1))
        o_ref[:, lanes] = o_stage[half][...]
        lse_ref[:, lanes] = lse_stage[half][...]


def prompt_attention(g, qn, kn, qkv):
    _, dil = ATT_GROUPS[g]
    span = N_KEYS * dil
    n_span = SEQ // span
    halves_per_group = GROUP_W // LANES
    cur = lambda col: (lambda b, a: (b * n_span + a, col))
    prev = lambda col: (lambda b, a: (b * n_span + jnp.maximum(a - 1, 0), col))
    blk = (span, LANES)
    qk_cols = [halves_per_group * g + half for half in range(halves_per_group)]
    v_cols = [halves_per_group * (2 * N_GROUPS + g) + half for half in range(halves_per_group)]
    in_specs = ([pl.BlockSpec(blk, cur(c)) for c in qk_cols] + [pl.BlockSpec(blk, prev(c)) for c in qk_cols]
                + [pl.BlockSpec(blk, cur(c)) for c in qk_cols] + [pl.BlockSpec(blk, prev(c)) for c in v_cols]
                + [pl.BlockSpec(blk, cur(c)) for c in v_cols])
    out = pl.BlockSpec((span, GROUP_W), lambda b, a: (b * n_span + a, 0))
    return _call(
        functools.partial(_attn_body, dil=dil), grid=(BATCH, n_span),
        in_specs=in_specs, out_specs=[out, out],
        out_shape=[jax.ShapeDtypeStruct((M_PROMPT, GROUP_W), F32)] * 2,
        scratch_shapes=[pltpu.VMEM((span, LANES), F32)] * 4,
        name="prompt_attention")(qn, qn, kn, kn, kn, kn, qkv, qkv, qkv, qkv)


def _dec_attn_body(*refs, dil, aliased):
    if aliased:
        qkv_ref, row_ref, ck_ref, cv_ref, _, _, o_ref, lse_ref, nk_ref, nv_ref = refs
    else:
        qkv_ref, row_ref, ck_ref, cv_ref, o_ref, lse_ref, nk_ref, nv_ref = refs
    bs = ck_ref.shape[1]
    ones = jnp.ones((8, LANES), BF16)
    n_chunk = ck_ref.shape[4] // LANES
    lane = lax.broadcasted_iota(jnp.int32, (1, LANES), 1)
    key_lane = (lane % dil) == 0
    not_last = lax.broadcasted_iota(jnp.int32, (HEAD_DIM, LANES), 1) < LANES - 1
    sample_row = lax.broadcasted_iota(jnp.int32, (qkv_ref.shape[1], LANES), 0)
    qkv_t = qkv_ref[...]
    for s in range(bs):
        sample = pl.program_id(0) * bs + s
        pick = (sample_row == sample).astype(BF16)
        cols = _dot_sel(qkv_t, pick, 3)
        sample_rows = row_ref[pl.ds(sample, 1), :]
        accs, own_heads, den_heads, lse_heads = [], [], [], []
        for h in range(HEADS):
            rows = slice(HEAD_DIM * h, HEAD_DIM * (h + 1))
            qb = cols[rows, :]
            kn = cols[GROUP_W + HEAD_DIM * h:GROUP_W + HEAD_DIM * (h + 1), :]
            vn = cols[2 * GROUP_W + HEAD_DIM * h:2 * GROUP_W + HEAD_DIM * (h + 1), :]
            chunk = lambda ref, c: ref[0, s, h, :, c * LANES:(c + 1) * LANES]
            scores = [jnp.where(key_lane, jnp.sum(chunk(ck_ref, c) * qb, axis=0, keepdims=True), NEG_BIG)
                      for c in range(n_chunk)]
            s_self = jnp.sum(kn * qb, axis=0, keepdims=True)
            m = s_self
            for sc in scores:
                m = jnp.maximum(m, jnp.max(sc, axis=1, keepdims=True))
            p_self = jnp.exp(s_self - m)
            l = p_self
            acc = None
            for c in range(n_chunk):
                p = jnp.exp(scores[c] - m)
                l = l + jnp.sum(p, axis=1, keepdims=True)
                pv = p * chunk(cv_ref, c)
                acc = pv if acc is None else acc + pv
            accs.append(acc)
            own_heads.append(p_self[:, :HEAD_DIM])
            den_heads.append(l[:, :HEAD_DIM])
            lse_heads.append((m + jnp.log(l))[:, :HEAD_DIM])
            for src, dst, new in ((ck_ref, nk_ref, kn), (cv_ref, nv_ref, vn)):
                rolled = [pltpu.roll(chunk(src, c), LANES - 1, axis=1) for c in range(n_chunk)] + [new]
                for c in range(n_chunk):
                    dst[0, s, h, :, c * LANES:(c + 1) * LANES] = jnp.where(not_last, rolled[c], rolled[c + 1])
        acc_t, rest = None, jnp.concatenate(accs, axis=0)
        for _ in range(3):
            piece = rest.astype(BF16)
            rest = rest - piece.astype(F32)
            part = lax.dot_general(ones, piece, (((1,), (1,)), ((), ())), preferred_element_type=F32)
            acc_t = part if acc_t is None else acc_t + part
        own = jnp.concatenate(own_heads, axis=1) * sample_rows[:, 2 * GROUP_W:]
        o_ref[pl.ds(sample, 1), :] = (acc_t[0:1, :] + own) / jnp.concatenate(den_heads, axis=1)
        lse_ref[pl.ds(sample, 1), :] = jnp.concatenate(lse_heads, axis=1)


def decode_attention(layer, g, qkv_rows, cache_k, cache_v, prev_k, prev_v, bs):
    window, dil = ATT_GROUPS[g]
    cblk = (1, bs, HEADS, HEAD_DIM, window)
    cmap = lambda i: (layer, i, 0, 0, 0)
    whole = lambda shape: pl.BlockSpec(shape, lambda i: (0, 0))
    in_specs = [whole((3 * GROUP_W, DEC_BATCH)), whole((DEC_BATCH, 3 * GROUP_W)),
                pl.BlockSpec(cblk, cmap), pl.BlockSpec(cblk, cmap)]
    args = [qkv_rows.T, qkv_rows, cache_k, cache_v]
    aliases = None
    if prev_k is not None:
        in_specs += [pl.BlockSpec(memory_space=pl.ANY), pl.BlockSpec(memory_space=pl.ANY)]
        args += [prev_k, prev_v]
        aliases = {4: 2, 5: 3}
    out = whole((DEC_BATCH, GROUP_W))
    return _call(
        functools.partial(_dec_attn_body, dil=dil, aliased=prev_k is not None), grid=(DEC_BATCH // bs,),
        in_specs=in_specs,
        out_specs=[out, out, pl.BlockSpec(cblk, cmap), pl.BlockSpec(cblk, cmap)],
        out_shape=[jax.ShapeDtypeStruct((DEC_BATCH, GROUP_W), F32)] * 2
        + [jax.ShapeDtypeStruct(cache_k.shape, F32)] * 2,
        input_output_aliases=aliases, name="decode_attention")(*args)


def _mix_body(o0_ref, o1_ref, o2_ref, l0_ref, l1_ref, l2_ref, att_ref):
    l0, l1, l2 = l0_ref[...], l1_ref[...], l2_ref[...]
    m = jnp.maximum(jnp.maximum(l0, l1), l2)
    e0, e1, e2 = jnp.exp(l0 - m), jnp.exp(l1 - m), jnp.exp(l2 - m)
    num = e0 * o0_ref[...] + e1 * o1_ref[...] + e2 * o2_ref[...]
    att_ref[...] = (num / (e0 + e1 + e2)).astype(att_ref.dtype)


def mix_groups(outs, lses, tm, out_dtype):
    m = outs[0].shape[0]
    spec = pl.BlockSpec((tm, GROUP_W), lambda i: (i, 0))
    return _call(_mix_body, grid=(m // tm,), in_specs=[spec] * 6, out_specs=spec,
                 out_shape=jax.ShapeDtypeStruct((m, GROUP_W), out_dtype), name="mix_groups")(*outs, *lses)


def _s5_prompt_body(u_ref, mt_ref, ms_ref, mo_ref, et_ref, es_ref, a1_ref, a2_ref, y_ref, hl_ref,
                    s_ref, hp_ref, toep_ref, wst_ref, wout_ref):
    @pl.when(pl.program_id(1) == 0)
    def _():
        def expand(m_ref, e_ref, row_div, col_div, dst_ref):
            full = _dot(m_ref[0].astype(BF16), e_ref[...])
            row_g = (lax.broadcasted_iota(jnp.int32, full.shape, 0) // row_div) % S5_PACK
            col_g = (lax.broadcasted_iota(jnp.int32, full.shape, 1) // col_div) % S5_PACK
            dst_ref[...] = jnp.where(row_g == col_g, full, 0.0).astype(BF16)

        expand(mt_ref, et_ref, S5_CH, S5_CH, toep_ref)
        expand(ms_ref, es_ref, S5_CH, S5_STATE, wst_ref)
        expand(mo_ref, et_ref, S5_STATE, S5_CH, wout_ref)

    n_chunk = s_ref.shape[0]
    u8 = jnp.concatenate([u_ref[pl.ds(s, n_chunk, stride=S5_CHUNK), :] for s in range(S5_CHUNK)],
                         axis=1).astype(BF16)
    s_ref[...] = _dot(u8, wst_ref[...])
    a1, a2 = a1_ref[0], a2_ref[0]
    half = a1.shape[1] // 2

    def step(c, h):
        hp_ref[pl.ds(c, 1), :] = h
        swapped = jnp.concatenate([h[:, half:], h[:, :half]], axis=1)
        return a1 * h + a2 * swapped + s_ref[pl.ds(c, 1), :]

    hl_ref[0, 0] = lax.fori_loop(0, n_chunk, step, jnp.zeros((1, 2 * half), F32))
    y8 = _dot(u8, toep_ref[...]) + _dot(hp_ref[...].astype(BF16), wout_ref[...])
    for t in range(S5_CHUNK):
        y_ref[pl.ds(t, n_chunk, stride=S5_CHUNK), :] = y8[:, t * LANES:(t + 1) * LANES]


def s5_prompt_weights(lam_re, lam_im, log_dt, b_re, b_im, c_re, c_im):
    chunk = S5_CHUNK
    dt = jnp.exp(log_dt)[:, None]
    tau = jnp.arange(chunk + 1, dtype=F32)[:, None, None]
    mag = jnp.exp(lam_re * dt * tau)
    pw_r = mag * jnp.cos(lam_im * dt * tau)
    pw_i = mag * jnp.sin(lam_im * dt * tau)
    abar_r, abar_i = pw_r[1], pw_i[1]
    den = lam_re * lam_re + lam_im * lam_im
    fr = ((abar_r - 1.0) * lam_re + abar_i * lam_im) / den
    fi = (abar_i * lam_re - (abar_r - 1.0) * lam_im) / den
    bbr = fr[..., None] * b_re - fi[..., None] * b_im
    bbi = fr[..., None] * b_im + fi[..., None] * b_re
    ab_r = pw_r[..., None] * bbr - pw_i[..., None] * bbi
    ab_i = pw_r[..., None] * bbi + pw_i[..., None] * bbr
    kern = jnp.sum(c_re[None, :, :, :, None] * ab_r[:chunk, :, None] - c_im[None, :, :, :, None] * ab_i[:chunk, :, None],
                   axis=3)
    s_idx = jnp.arange(chunk)[:, None]
    t_idx = jnp.arange(chunk)[None, :]
    lag = t_idx - s_idx
    k_st = kern[jnp.maximum(lag, 0)]
    k_st = jnp.where((lag >= 0)[:, :, None, None, None], k_st, 0.0)
    split = lambda x, axis: x.reshape(x.shape[:axis] + (N_SUPER, S5_PACK) + x.shape[axis + 1:])
    width = chunk * LANES
    n_state = S5_PACK * S5_STATE
    m_toep = split(k_st, 2).transpose(2, 0, 3, 5, 1, 4).reshape(N_SUPER, width, LANES)
    rev = chunk - 1 - jnp.arange(chunk)
    ab_rev = jnp.stack([ab_r[rev], ab_i[rev]])
    m_st = split(ab_rev, 2).transpose(2, 1, 3, 5, 0, 4).reshape(N_SUPER, width, LANES)
    ar, ai = pw_r[1:, :, None, :], pw_i[1:, :, None, :]
    co_r = c_re[None] * ar - c_im[None] * ai
    co_i = -c_re[None] * ai - c_im[None] * ar
    co = jnp.stack([co_r, co_i])
    m_out = split(co, 2).transpose(2, 0, 3, 5, 1, 4).reshape(N_SUPER, 2 * n_state, LANES)
    src = jnp.arange(LANES)[:, None]
    col = jnp.arange(width)[None, :]
    e_to = ((src // S5_CH == col // LANES) & (src % S5_CH == col % S5_CH)).astype(BF16)
    e_st = ((src // S5_STATE == col // n_state) & (src % S5_STATE == col % S5_STATE)).astype(BF16)
    last_r = pw_r[chunk].reshape(N_SUPER, 1, n_state)
    last_i = pw_i[chunk].reshape(N_SUPER, 1, n_state)
    a1 = jnp.concatenate([last_r, last_r], axis=2)
    a2 = jnp.concatenate([-last_i, last_i], axis=2)
    return m_toep, m_st, m_out, e_to, e_st, a1, a2


def s5_prompt(mid, weights):
    n_chunk = SEQ // S5_CHUNK
    n_state2 = 2 * S5_PACK * S5_STATE
    width = S5_CHUNK * LANES
    assert width == n_state2
    seq_blk = pl.BlockSpec((SEQ, LANES), lambda g, b: (b, g))
    per_g = lambda shape: pl.BlockSpec((1,) + shape, lambda g, b: (g, 0, 0))
    spread = pl.BlockSpec((LANES, width), lambda g, b: (0, 0))
    y, h_last = _call(
        _s5_prompt_body, grid=(N_SUPER, BATCH),
        in_specs=[seq_blk, per_g((width, LANES)), per_g((width, LANES)), per_g((n_state2, LANES)),
                  spread, spread, per_g((1, n_state2)), per_g((1, n_state2))],
        out_specs=[seq_blk, pl.BlockSpec((1, 1, 1, n_state2), lambda g, b: (b, g, 0, 0))],
        out_shape=[jax.ShapeDtypeStruct((M_PROMPT, S5_WIDTH), F32),
                   jax.ShapeDtypeStruct((BATCH, N_SUPER, 1, n_state2), F32)],
        scratch_shapes=[pltpu.VMEM((n_chunk, n_state2), F32), pltpu.VMEM((n_chunk, n_state2), F32)]
        + [pltpu.VMEM((width, width), BF16)] * 3,
        name="s5_prompt")(mid, *weights)
    h = h_last.reshape(BATCH, N_SUPER, 2, S5_PACK, S5_STATE).transpose(0, 1, 3, 4, 2)
    return y, h.reshape(BATCH, S5_GROUPS, S5_STATE, 2)


def _s5_post_body(y_ref, u_ref, d_ref, w_ref, b_ref, o_ref):
    z = _gelu(y_ref[...] + d_ref[...] * u_ref[...])
    lin = _dot(z.astype(BF16), w_ref[...].astype(BF16))
    o_ref[...] = (z * _sigmoid(lin + b_ref[...])).astype(o_ref.dtype)


def s5_post(y, mid, d_skip, w_glu, b_glu, tm):
    m = y.shape[0]
    row = lambda i: (i, 0)
    fixed = lambda i: (0, 0)
    return _call(
        _s5_post_body, grid=(m // tm,),
        in_specs=[pl.BlockSpec((tm, S5_WIDTH), row), pl.BlockSpec((tm, S5_WIDTH), row),
                  pl.BlockSpec((1, S5_WIDTH), fixed), pl.BlockSpec((S5_WIDTH, S5_WIDTH), fixed),
                  pl.BlockSpec((1, S5_WIDTH), fixed)],
        out_specs=pl.BlockSpec((tm, S5_WIDTH), row),
        out_shape=jax.ShapeDtypeStruct((m, S5_WIDTH), BF16),
        name="s5_post")(y, mid, d_skip.reshape(1, -1), w_glu, b_glu.reshape(1, -1))


def _s5_decode_body(u_ref, hr_ref, hi_ref, ar_ref, ai_ref, bb_ref, cc_ref, y_ref, sr_ref, si_ref):
    n = hr_ref.shape[1]
    bu = _dot(u_ref[...].astype(BF16), bb_ref[...])
    ar, ai = ar_ref[...], ai_ref[...]
    hr, hi = hr_ref[...], hi_ref[...]
    sr = bu[:, :n] + (ar * hr - ai * hi)
    si = bu[:, n:] + (ar * hi + ai * hr)
    sr_ref[...] = sr
    si_ref[...] = si
    s_cat = jnp.concatenate([sr, si], axis=1).astype(BF16)
    y_ref[...] = lax.dot_general(s_cat, cc_ref[...], (((1,), (1,)), ((), ())), preferred_element_type=F32)


def s5_decode(mid, state, lam_re, lam_im, log_dt, b_re, b_im, c_re, c_im):
    dt = jnp.exp(log_dt)[:, None]
    mag = jnp.exp(lam_re * dt)
    abar_r, abar_i = mag * jnp.cos(lam_im * dt), mag * jnp.sin(lam_im * dt)
    den = lam_re * lam_re + lam_im * lam_im
    fr = ((abar_r - 1.0) * lam_re + abar_i * lam_im) / den
    fi = (abar_i * lam_re - (abar_r - 1.0) * lam_im) / den
    bbr = fr[..., None] * b_re - fi[..., None] * b_im
    bbi = fr[..., None] * b_im + fi[..., None] * b_re
    n_state = S5_GROUPS * S5_STATE
    same_group = (jnp.arange(S5_WIDTH)[:, None] // S5_CH) == (jnp.arange(n_state)[None, :] // S5_STATE)

    def block_diag(w):
        tiled = jnp.tile(w.reshape(S5_WIDTH, S5_STATE), (1, S5_GROUPS))
        return jnp.where(same_group, tiled, 0.0)

    in_r, in_i = block_diag(bbr.transpose(0, 2, 1)), block_diag(bbi.transpose(0, 2, 1))
    bb = jnp.concatenate([in_r, in_i], axis=1).astype(BF16)
    cc_t = jnp.concatenate([block_diag(c_re), -block_diag(c_im)], axis=1).astype(BF16)
    nb = DEC_BATCH
    fixed = lambda i: (0, 0)
    st = pl.BlockSpec((nb, n_state), fixed)
    vec = pl.BlockSpec((1, n_state), fixed)
    y, sr, si = _call(
        _s5_decode_body, grid=(1,),
        in_specs=[pl.BlockSpec((nb, S5_WIDTH), fixed), st, st, vec, vec,
                  pl.BlockSpec((S5_WIDTH, 2 * n_state), fixed), pl.BlockSpec((S5_WIDTH, 2 * n_state), fixed)],
        out_specs=[pl.BlockSpec((nb, S5_WIDTH), fixed), st, st],
        out_shape=[jax.ShapeDtypeStruct((nb, S5_WIDTH), F32), jax.ShapeDtypeStruct((nb, n_state), F32),
                   jax.ShapeDtypeStruct((nb, n_state), F32)],
        name="s5_decode")(mid, state[..., 0].reshape(nb, n_state), state[..., 1].reshape(nb, n_state),
                          abar_r.reshape(1, n_state), abar_i.reshape(1, n_state), bb, cc_t)
    new_state = jnp.stack([sr.reshape(nb, S5_GROUPS, S5_STATE), si.reshape(nb, S5_GROUPS, S5_STATE)], axis=-1)
    return y, new_state


def _lru_gates(xc, wa_ref, ba_ref, wi_ref, bi_ref, c8_ref):
    xb = xc.astype(BF16)
    r = _sigmoid(_dot(xb, wa_ref[...]) + ba_ref[...])
    ig = _sigmoid(_dot(xb, wi_ref[...]) + bi_ref[...])
    log_a = c8_ref[...] * r
    a = jnp.exp(log_a)
    b = jnp.sqrt(-jnp.tanh(log_a) * (a * a + 1.0)) * (ig * xc)
    return a, b


def _lru_pre_body(x_ref, p_ref, cw_ref, cb_ref, wa_ref, ba_ref, wi_ref, bi_ref, c8_ref,
                  a_ref, b_ref, ext_ref, *, tiles_per_seq):
    tm = x_ref.shape[0]
    first = pl.program_id(0) % tiles_per_seq == 0
    ext_ref[0:8, :] = jnp.where(first, 0.0, p_ref[...])
    ext_ref[8:, :] = x_ref[...]
    xc = cb_ref[...] + cw_ref[3:4, :] * x_ref[...]
    for k in (1, 2, 3):
        xc = xc + cw_ref[3 - k:4 - k, :] * ext_ref[8 - k:8 - k + tm, :]
    a, b = _lru_gates(xc, wa_ref, ba_ref, wi_ref, bi_ref, c8_ref)
    a_ref[...] = a
    b_ref[...] = b


def _lru_scan_body(a_ref, b_ref, g_ref, y_ref, hl_ref, h_ref, p_ref):
    t_len = a_ref.shape[0]
    seg_len = t_len // LRU_SEG
    n_q = LRU_SEG // 8

    def step(t, carry):
        out = []
        for qi in range(n_q):
            h, p = carry[2 * qi], carry[2 * qi + 1]
            rows = pl.ds(qi * 8 * seg_len + t, 8, stride=seg_len)
            at = a_ref[rows, :]
            h = at * h + b_ref[rows, :]
            p = at * p
            h_ref[rows, :] = h
            p_ref[rows, :] = p
            out += [h, p]
        return tuple(out)

    init = (jnp.zeros((8, LANES), F32), jnp.ones((8, LANES), F32)) * n_q
    lax.fori_loop(0, seg_len, step, init)
    carry = jnp.zeros((1, LANES), F32)
    for s in range(LRU_SEG):
        rows = slice(s * seg_len, (s + 1) * seg_len)
        h = h_ref[rows, :] + p_ref[rows, :] * carry
        y_ref[rows, :] = (h * _gelu(g_ref[rows, :])).astype(y_ref.dtype)
        carry = h[seg_len - 1:seg_len, :]
    hl_ref[0] = carry


def lru_weights(conv_w, conv_b, w_a, b_a, w_i, b_i, lam):
    eye = jnp.eye(LRU_HEADS, dtype=F32)
    bd = lambda w: jnp.einsum("hij,hk->hikj", w, eye).reshape(LRU_WIDTH, LRU_WIDTH).astype(BF16)
    c8 = (-LRU_C) * jax.nn.softplus(-lam)
    return (conv_w, conv_b.reshape(1, -1), bd(w_a), b_a.reshape(1, -1), bd(w_i), b_i.reshape(1, -1),
            c8.reshape(1, -1))


def lru_prompt(mid, lw, tm):
    conv_w, conv_b, wa, ba, wi, bi, c8 = lw
    fixed = lambda i: (0, 0)
    vec = pl.BlockSpec((1, LRU_WIDTH), fixed)
    mat = pl.BlockSpec((LRU_WIDTH, LRU_WIDTH), fixed)
    a, b = _call(
        functools.partial(_lru_pre_body, tiles_per_seq=SEQ // tm), grid=(M_PROMPT // tm,),
        in_specs=[pl.BlockSpec((tm, LRU_WIDTH), lambda i: (i, 1)),
                  pl.BlockSpec((8, LRU_WIDTH), lambda i: (jnp.maximum(i * (tm // 8) - 1, 0), 1)),
                  pl.BlockSpec((4, LRU_WIDTH), fixed), vec, mat, vec, mat, vec, vec],
        out_specs=[pl.BlockSpec((tm, LRU_WIDTH), lambda i: (i, 0))] * 2,
        out_shape=[jax.ShapeDtypeStruct((M_PROMPT, LRU_WIDTH), F32)] * 2,
        scratch_shapes=[pltpu.VMEM((tm + 8, LRU_WIDTH), F32)],
        name="lru_pre")(mid, mid, conv_w, conv_b, wa, ba, wi, bi, c8)
    n_lane = LRU_WIDTH // LANES
    g_col0 = (S5_WIDTH + LRU_WIDTH) // LANES
    blk = lambda off: pl.BlockSpec((SEQ, LANES), lambda bi_, c: (bi_, off + c))
    y, h_last = _call(
        _lru_scan_body, grid=(BATCH, n_lane),
        in_specs=[blk(0), blk(0), blk(g_col0)],
        out_specs=[blk(0), pl.BlockSpec((1, 1, LANES), lambda bi_, c: (bi_, 0, c))],
        out_shape=[jax.ShapeDtypeStruct((M_PROMPT, LRU_WIDTH), BF16),
                   jax.ShapeDtypeStruct((BATCH, 1, LRU_WIDTH), F32)],
        scratch_shapes=[pltpu.VMEM((SEQ, LANES), F32), pltpu.VMEM((SEQ, LANES), F32)],
        name="lru_scan")(a, b, mid)
    return y, h_last.reshape(BATCH, LRU_WIDTH)


def _lru_decode_body(x_ref, g_ref, c0_ref, c1_ref, c2_ref, h0_ref, cw_ref, cb_ref, wa_ref, ba_ref,
                     wi_ref, bi_ref, c8_ref, y_ref, h_ref):
    xc = (cb_ref[...] + cw_ref[0:1, :] * c0_ref[...] + cw_ref[1:2, :] * c1_ref[...]
          + cw_ref[2:3, :] * c2_ref[...] + cw_ref[3:4, :] * x_ref[...])
    a, b = _lru_gates(xc, wa_ref, ba_ref, wi_ref, bi_ref, c8_ref)
    h = a * h0_ref[...] + b
    h_ref[...] = h
    y_ref[...] = (h * _gelu(g_ref[...])).astype(y_ref.dtype)


def lru_decode(mid, conv_buf, h0, lw):
    conv_w, conv_b, wa, ba, wi, bi, c8 = lw
    nb = DEC_BATCH
    fixed = lambda i: (0, 0)
    vec = pl.BlockSpec((1, LRU_WIDTH), fixed)
    mat = pl.BlockSpec((LRU_WIDTH, LRU_WIDTH), fixed)
    st = pl.BlockSpec((nb, LRU_WIDTH), fixed)
    return _call(
        _lru_decode_body, grid=(1,),
        in_specs=[pl.BlockSpec((nb, LRU_WIDTH), lambda i: (0, 1)), pl.BlockSpec((nb, LRU_WIDTH), lambda i: (0, 2)),
                  st, st, st, st, pl.BlockSpec((4, LRU_WIDTH), fixed), vec, mat, vec, mat, vec, vec],
        out_specs=[st, st],
        out_shape=[jax.ShapeDtypeStruct((nb, LRU_WIDTH), BF16), jax.ShapeDtypeStruct((nb, LRU_WIDTH), F32)],
        name="lru_decode")(mid, mid, conv_buf[:, 0], conv_buf[:, 1], conv_buf[:, 2], h0,
                           conv_w, conv_b, wa, ba, wi, bi, c8)


TM = 1024
TM_SMALL = 512
TM_EXPERT = 512
M_ALL = M_PROMPT + DEC_BATCH
N_EXPERT_TILES = (2 * M_ALL) // TM_EXPERT + N_EXPERTS
TM_ROUTER = M_ALL // 16
DEC_ATT_BS = (16, 8, 2)


def _token_mixers(layer, xp, xs, p, caches_t, win_prev, state_s5, state_conv, state_lru):
    n_p = rmsnorm(xp, p["norm_mix"][layer], TM_SMALL, BF16)
    n_s = rmsnorm(xs, p["norm_mix"][layer], DEC_BATCH, BF16)
    w_in = p["w_in"]
    proj = functools.partial(matmul_ws, [n_p], [n_s], w_of_x=[0], tm=TM)
    qkv_p, qkv_s = proj([(w_in, (layer,))], n_cols=OFF_MID, tn=768, out_dtype=F32, epilogue=_ep_plain,
                        name="proj_qkv")
    mid_p, mid_s = proj([(w_in[layer, :, OFF_MID:OFF_GATE], ())], tn=640, out_dtype=F32, epilogue=_ep_plain,
                        name="proj_mid")
    gates_p, gates_s = proj([(w_in[layer, :, OFF_GATE:], ())], tn=1024, out_dtype=BF16, out_dtype_s=F32,
                            epilogue=_ep_sigmoid, name="proj_gates")

    qn_p, kn_p = qk_norm(qkv_p, p["q_gain"][layer], p["k_gain"][layer], TM_SMALL, F32)
    qn_s, kn_s = qk_norm(qkv_s, p["q_gain"][layer], p["k_gain"][layer], DEC_BATCH, F32)
    outs, lses, outs_s, lses_s = [], [], [], []
    new_p, win = [], []
    for g, (window, dil) in enumerate(ATT_GROUPS):
        cols = slice(GROUP_W * g, GROUP_W * (g + 1))
        v_cols = slice(2 * ATT_WIDTH + GROUP_W * g, 2 * ATT_WIDTH + GROUP_W * (g + 1))
        o, lse = prompt_attention(g, qn_p, kn_p, qkv_p)
        outs.append(o)
        lses.append(lse)
        keep = min(window, SEQ)
        tail = lambda x, cs: (x.reshape(BATCH, SEQ, x.shape[1])[:, SEQ - keep:, cs]
                              .reshape(BATCH, keep, HEADS, HEAD_DIM))
        new_p.append(tail(kn_p, cols))
        new_p.append(tail(qkv_p, v_cols))
        cols_s = jnp.concatenate([qn_s[:, cols], kn_s[:, cols], qkv_s[:, v_cols]], axis=1)
        prev_k, prev_v = (None, None) if win_prev is None else (win_prev[2 * g], win_prev[2 * g + 1])
        o_s, lse_s, wk, wv = decode_attention(layer, g, cols_s, caches_t[2 * g], caches_t[2 * g + 1],
                                              prev_k, prev_v, DEC_ATT_BS[g])
        outs_s.append(o_s)
        lses_s.append(lse_s)
        win += [wk, wv]
    att_p = mix_groups(outs, lses, TM_SMALL, BF16)
    att_s = mix_groups(outs_s, lses_s, DEC_BATCH, BF16)

    s5p = (p["s5_lam_re"][layer], p["s5_lam_im"][layer], p["s5_log_dt"][layer], p["s5_b_re"][layer],
           p["s5_b_im"][layer], p["s5_c_re"][layer], p["s5_c_im"][layer])
    y_p, h_p = s5_prompt(mid_p, s5_prompt_weights(*s5p))
    y_s, h_s = s5_decode(mid_s, state_s5[layer], *s5p)
    glu = (p["s5_d"][layer], p["s5_w_glu"][layer], p["s5_b_glu"][layer])
    s5_p = s5_post(y_p, mid_p, *glu, TM_SMALL)
    s5_s = s5_post(y_s, mid_s, *glu, DEC_BATCH)
    new_p.append(h_p)
    new_s = [h_s]

    lw = lru_weights(p["lru_conv_w"][layer], p["lru_conv_b"][layer], p["lru_w_a"][layer], p["lru_b_a"][layer],
                     p["lru_w_i"][layer], p["lru_b_i"][layer], p["lru_lam"][layer])
    lru_p, h_lp = lru_prompt(mid_p, lw, 512)
    lru_s, h_ls = lru_decode(mid_s, state_conv[layer], state_lru[layer], lw)
    x_lru = slice(S5_WIDTH, S5_WIDTH + LRU_WIDTH)
    new_p.append(mid_p.reshape(BATCH, SEQ, MID_WIDTH)[:, SEQ - 3:, x_lru])
    new_p.append(h_lp)
    new_s.append(jnp.concatenate([state_conv[layer, :, 1:], mid_s[:, None, x_lru]], axis=1))
    new_s.append(h_ls)

    tn = 1024
    gate_tiles = D_MODEL // tn
    merged_p, merged_s = matmul_ws(
        [att_p, s5_p, lru_p], [att_s, s5_s, lru_s],
        [(p["w_br_attn"], (layer,)), (p["w_br_s5"], (layer,)), (p["w_br_lru"], (layer,))],
        w_of_x=[0, 1, 2], tm=TM, tn=tn, out_dtype=BF16, epilogue=_ep_merge,
        tile_extras=[(gates_p, gates_s, 0), (gates_p, gates_s, gate_tiles), (gates_p, gates_s, 2 * gate_tiles)],
        name="merge")
    xp, xs = matmul_ws([merged_p], [merged_s], [(p["w_o"], (layer,))], w_of_x=[0], tm=TM, tn=tn,
                       out_dtype=F32, epilogue=_ep_residual, tile_extras=[(xp, xs, 0)], name="out_proj")
    return xp, xs, new_p, new_s, win


def _dense_ffn(xp, xs, p, layer, j):
    n_p = rmsnorm(xp, p["norm_ffn"][layer], TM_SMALL, BF16)
    n_s = rmsnorm(xs, p["norm_ffn"][layer], DEC_BATCH, BF16)
    h_p, h_s = matmul_ws([n_p], [n_s], [(p["ffn_w_gate"], (j,)), (p["ffn_w_up"], (j,))], w_of_x=[0, 0],
                         tm=TM, tn=512, out_dtype=BF16, epilogue=_ep_swiglu, name="ffn_up")
    return matmul_ws([h_p], [h_s], [(p["ffn_w_down"], (j,))], w_of_x=[0], tm=512, tn=512, out_dtype=F32,
                     epilogue=_ep_residual, tile_extras=[(xp, xs, 0)], name="ffn_down")


def _moe_ffn(xp, xs, p, layer, j):
    router = (p["norm_ffn"][layer], p["moe_router_w"][j], p["moe_router_b"][j])
    n, comb, idx = rmsnorm_router(jnp.concatenate([xp, xs], axis=0), *router, TM_ROUTER)
    idx = idx[:, :2]
    m = M_ALL
    top_e = idx.reshape(-1)
    top_w = jnp.take_along_axis(comb, idx, axis=1).reshape(-1)
    counts = jnp.sum(top_e[:, None] == jnp.arange(N_EXPERTS)[None, :], axis=0).astype(jnp.int32)
    padded = ((counts + TM_EXPERT - 1) // TM_EXPERT) * TM_EXPERT
    pad_end = jnp.cumsum(padded)
    pad_start = pad_end - padded
    start = jnp.cumsum(counts) - counts
    order = jnp.argsort(top_e, stable=True).astype(jnp.int32)
    rank = jnp.argsort(order).astype(jnp.int32)
    dest = pad_start[top_e] + rank - start[top_e]
    n_rows = N_EXPERT_TILES * TM_EXPERT
    tile_start = jnp.arange(N_EXPERT_TILES, dtype=jnp.int32) * TM_EXPERT
    tile_expert = jnp.minimum(jnp.sum(tile_start[:, None] >= pad_end[None, :], axis=1), N_EXPERTS - 1)
    tile_expert = tile_expert.astype(jnp.int32)
    n_used = (pad_end[-1] // TM_EXPERT).reshape(1).astype(jnp.int32)
    row_e = jnp.repeat(tile_expert, TM_EXPERT)
    within = jnp.arange(n_rows, dtype=jnp.int32) - pad_start[row_e]
    row_valid = within < counts[row_e]
    row_pair = order[jnp.clip(start[row_e] + within, 0, 2 * m - 1)]
    src_token = jnp.where(row_valid, row_pair // 2, 0)
    row_w = jnp.where(row_valid, top_w[row_pair], 0.0)
    x_sorted = jnp.take(n, src_token, axis=0, mode="clip")
    gmm = functools.partial(grouped_matmul_ws, tile_expert, n_used, lead=(j,), tm=TM_EXPERT)
    gate = gmm(x_sorted, p["moe_w_gate"], tn=1408, out_dtype=BF16, epilogue=_ep_plain, name="moe_gate")
    h = gmm(x_sorted, p["moe_w_up"], tn=1408, out_dtype=BF16, epilogue=_ep_silu_times, tile_extras=[gate],
            name="moe_up")
    ys = gmm(h, p["moe_w_down"], tn=1024, out_dtype=F32, epilogue=_ep_scale_rows,
             col_extras=[row_w.reshape(-1, 1)], name="moe_down")
    dest2 = dest.reshape(m, 2)
    y0 = jnp.take(ys, dest2[:, 0], axis=0, mode="clip")
    y1 = jnp.take(ys, dest2[:, 1], axis=0, mode="clip")
    return xp + (y0[:M_PROMPT] + y1[:M_PROMPT]), xs + (y0[M_PROMPT:] + y1[M_PROMPT:])


def kernel(x_prompt, x_sample, cache_k_w128, cache_v_w128, cache_k_w512, cache_v_w512, cache_k_w2048, cache_v_w2048, state_s5, state_conv, state_lru, norm_mix, w_in, q_gain, k_gain, s5_lam_re, s5_lam_im, s5_log_dt, s5_b_re, s5_b_im, s5_c_re, s5_c_im, s5_d, s5_w_glu, s5_b_glu, lru_conv_w, lru_conv_b, lru_w_a, lru_b_a, lru_w_i, lru_b_i, lru_lam, w_br_attn, w_br_s5, w_br_lru, w_o, norm_ffn, ffn_w_gate, ffn_w_up, ffn_w_down, moe_router_w, moe_router_b, moe_w_gate, moe_w_up, moe_w_down):
    p = dict(norm_mix=norm_mix, w_in=w_in, q_gain=q_gain, k_gain=k_gain, s5_lam_re=s5_lam_re,
             s5_lam_im=s5_lam_im, s5_log_dt=s5_log_dt, s5_b_re=s5_b_re, s5_b_im=s5_b_im, s5_c_re=s5_c_re,
             s5_c_im=s5_c_im, s5_d=s5_d, s5_w_glu=s5_w_glu, s5_b_glu=s5_b_glu, lru_conv_w=lru_conv_w,
             lru_conv_b=lru_conv_b, lru_w_a=lru_w_a, lru_b_a=lru_b_a, lru_w_i=lru_w_i, lru_b_i=lru_b_i,
             lru_lam=lru_lam, w_br_attn=w_br_attn, w_br_s5=w_br_s5, w_br_lru=w_br_lru, w_o=w_o,
             norm_ffn=norm_ffn, ffn_w_gate=ffn_w_gate, ffn_w_up=ffn_w_up, ffn_w_down=ffn_w_down,
             moe_router_w=moe_router_w, moe_router_b=moe_router_b, moe_w_gate=moe_w_gate,
             moe_w_up=moe_w_up, moe_w_down=moe_w_down)
    caches_t = [jnp.transpose(c, (0, 1, 3, 4, 2)) for c in
                (cache_k_w128, cache_v_w128, cache_k_w512, cache_v_w512, cache_k_w2048, cache_v_w2048)]
    xp = x_prompt.reshape(M_PROMPT, D_MODEL)
    xs = x_sample.reshape(DEC_BATCH, D_MODEL)
    states_p = [[] for _ in range(9)]
    states_s = [[] for _ in range(3)]
    win = None
    for layer in range(DEPTH):
        xp, xs, new_p, new_s, win = _token_mixers(layer, xp, xs, p, caches_t, win, state_s5, state_conv,
                                                  state_lru)
        if layer % 2 == 0:
            xp, xs = _dense_ffn(xp, xs, p, layer, layer // 2)
        else:
            xp, xs = _moe_ffn(xp, xs, p, layer, layer // 2)
        for i in range(9):
            states_p[i].append(new_p[i])
        for i in range(3):
            states_s[i].append(new_s[i])
    y_prompt = xp.reshape(BATCH, SEQ, D_MODEL)
    y_sample = xs.reshape(DEC_BATCH, 1, D_MODEL)
    windows = [jnp.transpose(w, (0, 1, 4, 2, 3)) for w in win]
    return (y_prompt, y_sample, *[jnp.stack(a) for a in states_p], *windows,
            *[jnp.stack(a) for a in states_s])
```

```python
import functools
import math

import jax
import jax.numpy as jnp
from jax import lax
from jax.experimental import pallas as pl
from jax.experimental.pallas import tpu as pltpu

F32 = jnp.float32
BF16 = jnp.bfloat16

D_MODEL = 2048
BATCH = 2
SEQ = 4096
DEPTH = 2
DEC_BATCH = 128
HEAD_DIM = 64
HEADS = 4
GROUP_W = HEADS * HEAD_DIM
ATT_GROUPS = ((128, 1), (512, 4), (2048, 16))
N_GROUPS = 3
ATT_WIDTH = N_GROUPS * GROUP_W
N_KEYS = 128
S5_WIDTH = 640
S5_GROUPS = 40
S5_CH = 16
S5_STATE = 64
LRU_WIDTH = 640
LRU_HEADS = 8
LRU_C = 8.0
OFF_MID = 3 * ATT_WIDTH
MID_WIDTH = S5_WIDTH + 2 * LRU_WIDTH
OFF_GATE = OFF_MID + MID_WIDTH
N_EXPERTS = 8
D_FF_EXPERT = 2816
EPS = 1e-6
LANES = 128

M_PROMPT = BATCH * SEQ

S5_CHUNK = 8
S5_PACK = LANES // S5_CH
N_SUPER = S5_GROUPS // S5_PACK
LRU_SEG = 32
NEG_BIG = -1e30

VMEM_LIMIT_BYTES = 56 * 1024 * 1024


def _call(body, *, grid, in_specs, out_specs, out_shape, name, scratch_shapes=(), num_scalar_prefetch=0,
          input_output_aliases=None):
    params = pltpu.CompilerParams(dimension_semantics=("arbitrary",) * len(grid),
                                  vmem_limit_bytes=VMEM_LIMIT_BYTES)
    kwargs = {}
    if input_output_aliases:
        kwargs["input_output_aliases"] = input_output_aliases
    if num_scalar_prefetch:
        grid_spec = pltpu.PrefetchScalarGridSpec(
            num_scalar_prefetch=num_scalar_prefetch, grid=grid, in_specs=in_specs,
            out_specs=out_specs, scratch_shapes=scratch_shapes)
        return pl.pallas_call(body, grid_spec=grid_spec, out_shape=out_shape,
                              compiler_params=params, name=name, **kwargs)
    return pl.pallas_call(body, grid=grid, in_specs=in_specs, out_specs=out_specs,
                          out_shape=out_shape, scratch_shapes=scratch_shapes,
                          compiler_params=params, name=name, **kwargs)


def _dot(a, b):
    return jnp.dot(a, b, preferred_element_type=F32)


def _dot_sel(x, sel, parts=2):
    acc = None
    r = x
    for _ in range(parts):
        piece = r.astype(BF16)
        r = r - piece.astype(F32)
        d = _dot(piece, sel)
        acc = d if acc is None else acc + d
    return acc


def _sigmoid(x):
    return 1.0 / (1.0 + jnp.exp(-x))


def _gelu(x):
    c = math.sqrt(2.0 / math.pi)
    return 0.5 * x * (1.0 + jnp.tanh(c * (x + 0.044715 * (x * x * x))))


def _rmsnorm_body(x_ref, g_ref, o_ref):
    x = x_ref[...]
    y = x * lax.rsqrt(jnp.mean(x * x, axis=-1, keepdims=True) + EPS)
    o_ref[...] = (y * g_ref[...]).astype(o_ref.dtype)


def _rmsnorm_router_body(x_ref, g_ref, rw_ref, rb_ref, o_ref, comb_ref, idx_ref):
    x = x_ref[...]
    y = x * lax.rsqrt(jnp.mean(x * x, axis=-1, keepdims=True) + EPS)
    xn = y * g_ref[...]
    o_ref[...] = xn.astype(o_ref.dtype)
    logits = _dot(xn.astype(BF16), rw_ref[...].astype(BF16)) + rb_ref[...]
    lane = lax.broadcasted_iota(jnp.int32, logits.shape, 1).astype(F32)
    m1 = jnp.max(logits, axis=1, keepdims=True)
    i1 = jnp.min(jnp.where(logits == m1, lane, float(LANES)), axis=1, keepdims=True)
    rest = jnp.where(lane == i1, -jnp.inf, logits)
    m2 = jnp.max(rest, axis=1, keepdims=True)
    i2 = jnp.min(jnp.where(rest == m2, lane, float(LANES)), axis=1, keepdims=True)
    e = jnp.exp(m2 - m1)
    w1 = 1.0 / (1.0 + e)
    w2 = e / (1.0 + e)
    comb_ref[...] = jnp.where(lane == i1, w1, 0.0) + jnp.where(lane == i2, w2, 0.0)
    idx_ref[...] = jnp.where(lane == 0.0, i1, jnp.where(lane == 1.0, i2, 0.0)).astype(jnp.int32)


def rmsnorm(x, gain, tm, out_dtype):
    m, d = x.shape
    return _call(
        _rmsnorm_body, grid=(m // tm,),
        in_specs=[pl.BlockSpec((tm, d), lambda i: (i, 0)), pl.BlockSpec((1, d), lambda i: (0, 0))],
        out_specs=pl.BlockSpec((tm, d), lambda i: (i, 0)),
        out_shape=jax.ShapeDtypeStruct((m, d), out_dtype), name="rmsnorm")(x, gain.reshape(1, d))


def rmsnorm_router(x, gain, router_w, router_b, tm):
    m, d = x.shape
    ne = router_w.shape[1]
    rw = jnp.zeros((d, LANES), F32).at[:, :ne].set(router_w)
    rb = jnp.full((1, LANES), NEG_BIG, F32).at[0, :ne].set(router_b)
    row = lambda i: (i, 0)
    fixed = lambda i: (0, 0)
    return _call(
        _rmsnorm_router_body, grid=(m // tm,),
        in_specs=[pl.BlockSpec((tm, d), row), pl.BlockSpec((1, d), fixed),
                  pl.BlockSpec((d, LANES), fixed), pl.BlockSpec((1, LANES), fixed)],
        out_specs=[pl.BlockSpec((tm, d), row), pl.BlockSpec((tm, LANES), row), pl.BlockSpec((tm, LANES), row)],
        out_shape=[jax.ShapeDtypeStruct((m, d), F32), jax.ShapeDtypeStruct((m, LANES), F32),
                   jax.ShapeDtypeStruct((m, LANES), jnp.int32)],
        name="rmsnorm_router")(x, gain.reshape(1, d), rw, rb)


def _mm_body(*refs, n_x, w_of_x, n_te, n_re, epilogue, n_ptiles):
    n_w = len(w_of_x)
    pos = 0
    xp_refs = refs[pos:pos + n_x]; pos += n_x
    xs_refs = refs[pos:pos + n_x]; pos += n_x
    w_refs = refs[pos:pos + n_w]; pos += n_w
    tep_refs = refs[pos:pos + n_te]; pos += n_te
    tes_refs = refs[pos:pos + n_te]; pos += n_te
    re_refs = refs[pos:pos + n_re]; pos += n_re
    op_ref, os_ref = refs[pos], refs[pos + 1]; pos += 2
    wb_refs = refs[pos:pos + n_w]
    i = pl.program_id(1)

    @pl.when(i == 0)
    def _():
        for w_ref, wb_ref in zip(w_refs, wb_refs):
            wb_ref[...] = w_ref[...].astype(BF16)

    def rows(x_refs, te_refs, o_ref):
        accs = [_dot(x_refs[xi][...].astype(BF16), wb_ref[...]) for xi, wb_ref in zip(w_of_x, wb_refs)]
        extras = [e[...] for e in te_refs] + [e[...] for e in re_refs]
        o_ref[...] = epilogue(accs, extras).astype(o_ref.dtype)

    @pl.when(i < n_ptiles)
    def _():
        rows(xp_refs, tep_refs, op_ref)

    @pl.when(i == n_ptiles)
    def _():
        rows(xs_refs, tes_refs, os_ref)


def matmul_ws(xps, xss, ws, *, w_of_x, tm, tn, out_dtype, epilogue, tile_extras=(), row_extras=(), name,
              out_dtype_s=None, n_cols=None):
    mp = xps[0].shape[0]
    ms = xss[0].shape[0]
    n = n_cols or ws[0][0].shape[-1]
    n_ptiles = mp // tm
    last = n_ptiles - 1
    grid = (n // tn, n_ptiles + 1)
    in_specs = [pl.BlockSpec((tm, x.shape[1]), lambda j, i: (jnp.minimum(i, last), 0)) for x in xps]
    in_specs += [pl.BlockSpec((ms, x.shape[1]), lambda j, i: (0, 0)) for x in xss]
    scratch = []
    for w, lead in ws:
        k = w.shape[-2]
        block = (None,) * len(lead) + (k, tn)
        in_specs.append(pl.BlockSpec(block, lambda j, i, lead=lead: lead + (0, j)))
        scratch.append(pltpu.VMEM((k, tn), BF16))
    for _, _, off in tile_extras:
        in_specs.append(pl.BlockSpec((tm, tn), lambda j, i, off=off: (jnp.minimum(i, last), off + j)))
    for _, _, off in tile_extras:
        in_specs.append(pl.BlockSpec((ms, tn), lambda j, i, off=off: (0, off + j)))
    for _ in row_extras:
        in_specs.append(pl.BlockSpec((1, tn), lambda j, i: (0, j)))
    body = functools.partial(_mm_body, n_x=len(xps), w_of_x=tuple(w_of_x), n_te=len(tile_extras),
                             n_re=len(row_extras), epilogue=epilogue, n_ptiles=n_ptiles)
    return _call(
        body, grid=grid, in_specs=in_specs,
        out_specs=[pl.BlockSpec((tm, tn), lambda j, i: (jnp.minimum(i, last), j)),
                   pl.BlockSpec((ms, tn), lambda j, i: (0, j))],
        out_shape=[jax.ShapeDtypeStruct((mp, n), out_dtype),
                   jax.ShapeDtypeStruct((ms, n), out_dtype_s or out_dtype)],
        scratch_shapes=scratch, name=name,
    )(*xps, *xss, *[w for w, _ in ws], *[a for a, _, _ in tile_extras], *[a for _, a, _ in tile_extras],
      *row_extras)


def _ep_plain(accs, extras):
    return accs[0]


def _ep_sigmoid(accs, extras):
    return _sigmoid(accs[0])


def _ep_residual(accs, extras):
    return extras[0] + accs[0]


def _ep_swiglu(accs, extras):
    g, u = accs
    return (g * _sigmoid(g)) * u


def _ep_merge(accs, extras):
    return (extras[0].astype(F32) * accs[0] + extras[1].astype(F32) * accs[1]
            + extras[2].astype(F32) * accs[2])


def _gmm_body(te_ref, nu_ref, *refs, n_te, n_ce, epilogue):
    x_ref, w_ref = refs[0], refs[1]
    extra_refs = refs[2:2 + n_te + n_ce]
    o_ref = refs[2 + n_te + n_ce]
    wb_ref = refs[3 + n_te + n_ce]
    i = pl.program_id(1)
    new_expert = jnp.logical_or(i == 0, te_ref[i] != te_ref[jnp.maximum(i - 1, 0)])

    @pl.when(new_expert)
    def _():
        wb_ref[...] = w_ref[...].astype(BF16)

    @pl.when(i < nu_ref[0])
    def _():
        acc = _dot(x_ref[...].astype(BF16), wb_ref[...])
        o_ref[...] = epilogue([acc], [e[...] for e in extra_refs]).astype(o_ref.dtype)

    @pl.when(i >= nu_ref[0])
    def _():
        o_ref[...] = jnp.zeros(o_ref.shape, o_ref.dtype)


def grouped_matmul_ws(tile_expert, n_used, x, w, lead, *, tm, tn, out_dtype, epilogue, tile_extras=(),
                      col_extras=(), name):
    m = x.shape[0]
    k, n = w.shape[-2:]
    grid = (n // tn, m // tm)
    block = (None,) * (len(lead) + 1) + (k, tn)
    in_specs = [pl.BlockSpec((tm, x.shape[1]), lambda j, i, te, nu: (i, 0)),
                pl.BlockSpec(block, lambda j, i, te, nu: lead + (te[i], 0, j))]
    in_specs += [pl.BlockSpec((tm, tn), lambda j, i, te, nu: (i, j)) for _ in tile_extras]
    in_specs += [pl.BlockSpec((tm, 1), lambda j, i, te, nu: (i, 0)) for _ in col_extras]
    body = functools.partial(_gmm_body, n_te=len(tile_extras), n_ce=len(col_extras), epilogue=epilogue)
    return _call(
        body, grid=grid, in_specs=in_specs,
        out_specs=pl.BlockSpec((tm, tn), lambda j, i, te, nu: (i, j)),
        out_shape=jax.ShapeDtypeStruct((m, n), out_dtype), scratch_shapes=[pltpu.VMEM((k, tn), BF16)],
        num_scalar_prefetch=2, name=name,
    )(tile_expert, n_used, x, w, *tile_extras, *col_extras)


def _ep_silu_times(accs, extras):
    g = extras[0].astype(F32)
    return (g * _sigmoid(g)) * accs[0]


def _ep_scale_rows(accs, extras):
    return extras[0] * accs[0]


def _qknorm_body(q_ref, k_ref, qg_ref, kg_ref, seg_ref, qo_ref, ko_ref):
    seg = seg_ref[...]
    for src, gain, dst, scale in ((q_ref, qg_ref, qo_ref, HEAD_DIM ** -0.5), (k_ref, kg_ref, ko_ref, 1.0)):
        for g in range(N_GROUPS):
            cols = slice(GROUP_W * g, GROUP_W * (g + 1))
            x = src[:, cols]
            ms = _dot_sel(x * x, seg, 3)
            y = x * lax.rsqrt(ms + EPS) * gain[:, cols]
            dst[:, cols] = (y * scale).astype(dst.dtype)


def qk_norm(qkv, q_gain, k_gain, tm, q_dtype):
    m = qkv.shape[0]
    head = jnp.arange(GROUP_W) // HEAD_DIM
    seg = jnp.where(head[:, None] == head[None, :], 1.0 / HEAD_DIM, 0.0).astype(BF16)
    qg = jnp.broadcast_to(q_gain[:, None, :], (N_GROUPS, HEADS, HEAD_DIM)).reshape(1, ATT_WIDTH)
    kg = jnp.broadcast_to(k_gain[:, None, :], (N_GROUPS, HEADS, HEAD_DIM)).reshape(1, ATT_WIDTH)
    fixed = lambda i: (0, 0)
    return _call(
        _qknorm_body, grid=(m // tm,),
        in_specs=[pl.BlockSpec((tm, ATT_WIDTH), lambda i: (i, 0)), pl.BlockSpec((tm, ATT_WIDTH), lambda i: (i, 1)),
                  pl.BlockSpec((1, ATT_WIDTH), fixed), pl.BlockSpec((1, ATT_WIDTH), fixed),
                  pl.BlockSpec((GROUP_W, GROUP_W), fixed)],
        out_specs=[pl.BlockSpec((tm, ATT_WIDTH), lambda i: (i, 0)), pl.BlockSpec((tm, ATT_WIDTH), lambda i: (i, 0))],
        out_shape=[jax.ShapeDtypeStruct((m, ATT_WIDTH), q_dtype), jax.ShapeDtypeStruct((m, ATT_WIDTH), F32)],
        name="qk_norm")(qkv, qkv, qg, kg, seg)


def _attn_body(*refs, dil):
    q_refs, kp_refs, kc_refs, vp_refs, vc_refs = (refs[2 * n:2 * n + 2] for n in range(5))
    o_ref, lse_ref = refs[10], refs[11]
    o_stage, lse_stage = refs[12:14], refs[14:16]
    a = pl.program_id(1)
    tq = N_KEYS
    r = lax.broadcasted_iota(jnp.int32, (HEADS * tq, 2 * tq), 0) % tq
    c = lax.broadcasted_iota(jnp.int32, (HEADS * tq, 2 * tq), 1)
    valid = (c >= r) & (c <= r + N_KEYS) & ((a > 0) | (c >= tq))
    head = lax.broadcasted_iota(jnp.int32, (1, GROUP_W), 1) // HEAD_DIM

    def residue(res, carry):
        rows = pl.ds(res, tq, stride=dil) if dil > 1 else slice(None)
        load = lambda halves: jnp.concatenate([halves[0][rows, :], halves[1][rows, :]], axis=1)
        q = load(q_refs)
        k2 = jnp.concatenate([load(kp_refs), load(kc_refs)], axis=0).astype(BF16)
        v2 = jnp.concatenate([load(vp_refs), load(vc_refs)], axis=0).astype(BF16)
        q4 = jnp.concatenate([jnp.where(head == h, q, 0.0) for h in range(HEADS)], axis=0).astype(BF16)
        s = lax.dot_general(q4, k2, (((1,), (1,)), ((), ())), preferred_element_type=F32)
        s = jnp.where(valid, s, NEG_BIG)
        m = jnp.max(s, axis=1, keepdims=True)
        p = jnp.exp(s - m)
        l = jnp.sum(p, axis=1, keepdims=True)
        o4 = _dot(p.astype(BF16), v2) / l
        lse4 = m + jnp.log(l)
        o = jnp.zeros((tq, GROUP_W), F32)
        lse = jnp.zeros((tq, GROUP_W), F32)
        for h in range(HEADS):
            blk = slice(h * tq, (h + 1) * tq)
            o = jnp.where(head == h, o4[blk, :], o)
            lse = jnp.where(head == h, lse4[blk, :], lse)
        for half in range(2):
            lanes = slice(LANES * half, LANES * (half + 1))
            o_stage[half][rows, :] = o[:, lanes]
            lse_stage[half][rows, :] = lse[:, lanes]
        return carry

    if dil == 1:
        residue(0, 0)
    else:
        lax.fori_loop(0, dil, residue, 0, unroll=2)
    for half in range(2):
        lanes = slice(LANES * half, LANES * (half + 1))
        o_ref[:, lanes] = o_stage[half][...]
        lse_ref[:, lanes] = lse_stage[half][...]


def prompt_attention(g, qn, kn, qkv):
    _, dil = ATT_GROUPS[g]
    span = N_KEYS * dil
    n_span = SEQ // span
    halves_per_group = GROUP_W // LANES
    cur = lambda col: (lambda b, a: (b * n_span + a, col))
    prev = lambda col: (lambda b, a: (b * n_span + jnp.maximum(a - 1, 0), col))
    blk = (span, LANES)
    qk_cols = [halves_per_group * g + half for half in range(halves_per_group)]
    v_cols = [halves_per_group * (2 * N_GROUPS + g) + half for half in range(halves_per_group)]
    in_specs = ([pl.BlockSpec(blk, cur(c)) for c in qk_cols] + [pl.BlockSpec(blk, prev(c)) for c in qk_cols]
                + [pl.BlockSpec(blk, cur(c)) for c in qk_cols] + [pl.BlockSpec(blk, prev(c)) for c in v_cols]
                + [pl.BlockSpec(blk, cur(c)) for c in v_cols])
    out = pl.BlockSpec((span, GROUP_W), lambda b, a: (b * n_span + a, 0))
    return _call(
        functools.partial(_attn_body, dil=dil), grid=(BATCH, n_span),
        in_specs=in_specs, out_specs=[out, out],
        out_shape=[jax.ShapeDtypeStruct((M_PROMPT, GROUP_W), F32)] * 2,
        scratch_shapes=[pltpu.VMEM((span, LANES), F32)] * 4,
        name="prompt_attention")(qn, qn, kn, kn, kn, kn, qkv, qkv, qkv, qkv)


def _dec_attn_body(*refs, dil, aliased):
    n_in = 6 if aliased else 4
    qkv_ref, row_ref, ck_ref, cv_ref = refs[:4]
    o_ref, lse_ref, nk_ref, nv_ref, cols_ref, acc_ref, own_ref, den_ref = refs[n_in:]
    bs = ck_ref.shape[1]
    ones = jnp.ones((8, LANES), BF16)
    n_chunk = ck_ref.shape[4] // LANES
    lane = lax.broadcasted_iota(jnp.int32, (1, LANES), 1)
    key_lane = (lane % dil) == 0
    not_last = lax.broadcasted_iota(jnp.int32, (HEAD_DIM, LANES), 1) < LANES - 1
    sample_row = lax.broadcasted_iota(jnp.int32, (qkv_ref.shape[1], LANES), 0)
    qkv_t = qkv_ref[...]
    first = pl.program_id(0) * bs
    for s in range(bs):
        cols_ref[s] = _dot_sel(qkv_t, (sample_row == first + s).astype(BF16), 3)
    for s in range(bs):
        sample = first + s
        sample_rows = row_ref[pl.ds(sample, 1), :]
        accs, own_heads, den_heads, lse_heads = [], [], [], []
        for h in range(HEADS):
            rows = slice(HEAD_DIM * h, HEAD_DIM * (h + 1))
            qb = cols_ref[s, rows, :]
            kn = cols_ref[s, GROUP_W + HEAD_DIM * h:GROUP_W + HEAD_DIM * (h + 1), :]
            vn = cols_ref[s, 2 * GROUP_W + HEAD_DIM * h:2 * GROUP_W + HEAD_DIM * (h + 1), :]
            chunk = lambda ref, c: ref[0, s, h, :, c * LANES:(c + 1) * LANES]
            scores = [jnp.where(key_lane, jnp.sum(chunk(ck_ref, c) * qb, axis=0, keepdims=True), NEG_BIG)
                      for c in range(n_chunk)]
            s_self = jnp.sum(kn * qb, axis=0, keepdims=True)
            m = s_self
            for sc in scores:
                m = jnp.maximum(m, jnp.max(sc, axis=1, keepdims=True))
            p_self = jnp.exp(s_self - m)
            l = p_self
            acc = None
            for c in range(n_chunk):
                p = jnp.exp(scores[c] - m)
                l = l + jnp.sum(p, axis=1, keepdims=True)
                pv = p * chunk(cv_ref, c)
                acc = pv if acc is None else acc + pv
            accs.append(acc)
            own_heads.append(p_self[:, :HEAD_DIM])
            den_heads.append(l[:, :HEAD_DIM])
            lse_heads.append((m + jnp.log(l))[:, :HEAD_DIM])
            for src, dst, new in ((ck_ref, nk_ref, kn), (cv_ref, nv_ref, vn)):
                rolled = [pltpu.roll(chunk(src, c), LANES - 1, axis=1) for c in range(n_chunk)] + [new]
                for c in range(n_chunk):
                    dst[0, s, h, :, c * LANES:(c + 1) * LANES] = jnp.where(not_last, rolled[c], rolled[c + 1])
        acc_ref[s] = jnp.concatenate(accs, axis=0)
        own_ref[s:s + 1, :] = jnp.concatenate(own_heads, axis=1) * sample_rows[:, 2 * GROUP_W:]
        den_ref[s:s + 1, :] = jnp.concatenate(den_heads, axis=1)
        lse_ref[pl.ds(sample, 1), :] = jnp.concatenate(lse_heads, axis=1)
    for s in range(bs):
        acc_t, rest = None, acc_ref[s]
        for _ in range(3):
            piece = rest.astype(BF16)
            rest = rest - piece.astype(F32)
            part = lax.dot_general(ones, piece, (((1,), (1,)), ((), ())), preferred_element_type=F32)
            acc_t = part if acc_t is None else acc_t + part
        o_ref[pl.ds(first + s, 1), :] = (acc_t[0:1, :] + own_ref[s:s + 1, :]) / den_ref[s:s + 1, :]


def decode_attention(layer, g, qkv_rows, cache_k, cache_v, prev_k, prev_v, bs):
    window, dil = ATT_GROUPS[g]
    cblk = (1, bs, HEADS, HEAD_DIM, window)
    cmap = lambda i: (layer, i, 0, 0, 0)
    whole = lambda shape: pl.BlockSpec(shape, lambda i: (0, 0))
    in_specs = [whole((3 * GROUP_W, DEC_BATCH)), whole((DEC_BATCH, 3 * GROUP_W)),
                pl.BlockSpec(cblk, cmap), pl.BlockSpec(cblk, cmap)]
    args = [qkv_rows.T, qkv_rows, cache_k, cache_v]
    aliases = None
    if prev_k is not None:
        in_specs += [pl.BlockSpec(memory_space=pl.ANY), pl.BlockSpec(memory_space=pl.ANY)]
        args += [prev_k, prev_v]
        aliases = {4: 2, 5: 3}
    out = whole((DEC_BATCH, GROUP_W))
    return _call(
        functools.partial(_dec_attn_body, dil=dil, aliased=prev_k is not None), grid=(DEC_BATCH // bs,),
        in_specs=in_specs,
        out_specs=[out, out, pl.BlockSpec(cblk, cmap), pl.BlockSpec(cblk, cmap)],
        out_shape=[jax.ShapeDtypeStruct((DEC_BATCH, GROUP_W), F32)] * 2
        + [jax.ShapeDtypeStruct(cache_k.shape, F32)] * 2,
        scratch_shapes=[pltpu.VMEM((bs, 3 * GROUP_W, LANES), F32), pltpu.VMEM((bs, GROUP_W, LANES), F32),
                        pltpu.VMEM((max(bs, 8), GROUP_W), F32), pltpu.VMEM((max(bs, 8), GROUP_W), F32)],
        input_output_aliases=aliases, name="decode_attention")(*args)


def _mix_body(o0_ref, o1_ref, o2_ref, l0_ref, l1_ref, l2_ref, att_ref):
    l0, l1, l2 = l0_ref[...], l1_ref[...], l2_ref[...]
    m = jnp.maximum(jnp.maximum(l0, l1), l2)
    e0, e1, e2 = jnp.exp(l0 - m), jnp.exp(l1 - m), jnp.exp(l2 - m)
    num = e0 * o0_ref[...] + e1 * o1_ref[...] + e2 * o2_ref[...]
    att_ref[...] = (num / (e0 + e1 + e2)).astype(att_ref.dtype)


def mix_groups(outs, lses, tm, out_dtype):
    m = outs[0].shape[0]
    spec = pl.BlockSpec((tm, GROUP_W), lambda i: (i, 0))
    return _call(_mix_body, grid=(m // tm,), in_specs=[spec] * 6, out_specs=spec,
                 out_shape=jax.ShapeDtypeStruct((m, GROUP_W), out_dtype), name="mix_groups")(*outs, *lses)


def _s5_prompt_body(u_ref, mt_ref, ms_ref, mo_ref, et_ref, es_ref, a1_ref, a2_ref, y_ref, hl_ref,
                    s_ref, hp_ref, toep_ref, wst_ref, wout_ref):
    @pl.when(pl.program_id(1) == 0)
    def _():
        def expand(m_ref, e_ref, row_div, col_div, dst_ref):
            full = _dot(m_ref[0].astype(BF16), e_ref[...])
            row_g = (lax.broadcasted_iota(jnp.int32, full.shape, 0) // row_div) % S5_PACK
            col_g = (lax.broadcasted_iota(jnp.int32, full.shape, 1) // col_div) % S5_PACK
            dst_ref[...] = jnp.where(row_g == col_g, full, 0.0).astype(BF16)

        expand(mt_ref, et_ref, S5_CH, S5_CH, toep_ref)
        expand(ms_ref, es_ref, S5_CH, S5_STATE, wst_ref)
        expand(mo_ref, et_ref, S5_STATE, S5_CH, wout_ref)

    n_chunk = s_ref.shape[0]
    u8 = jnp.concatenate([u_ref[pl.ds(s, n_chunk, stride=S5_CHUNK), :] for s in range(S5_CHUNK)],
                         axis=1).astype(BF16)
    s_ref[...] = _dot(u8, wst_ref[...])
    a1, a2 = a1_ref[0], a2_ref[0]
    half = a1.shape[1] // 2

    def step(c, h):
        hp_ref[pl.ds(c, 1), :] = h
        swapped = jnp.concatenate([h[:, half:], h[:, :half]], axis=1)
        return a1 * h + a2 * swapped + s_ref[pl.ds(c, 1), :]

    hl_ref[0, 0] = lax.fori_loop(0, n_chunk, step, jnp.zeros((1, 2 * half), F32))
    y8 = _dot(u8, toep_ref[...]) + _dot(hp_ref[...].astype(BF16), wout_ref[...])
    for t in range(S5_CHUNK):
        y_ref[pl.ds(t, n_chunk, stride=S5_CHUNK), :] = y8[:, t * LANES:(t + 1) * LANES]


def s5_prompt_weights(lam_re, lam_im, log_dt, b_re, b_im, c_re, c_im):
    chunk = S5_CHUNK
    dt = jnp.exp(log_dt)[:, None]
    tau = jnp.arange(chunk + 1, dtype=F32)[:, None, None]
    mag = jnp.exp(lam_re * dt * tau)
    pw_r = mag * jnp.cos(lam_im * dt * tau)
    pw_i = mag * jnp.sin(lam_im * dt * tau)
    abar_r, abar_i = pw_r[1], pw_i[1]
    den = lam_re * lam_re + lam_im * lam_im
    fr = ((abar_r - 1.0) * lam_re + abar_i * lam_im) / den
    fi = (abar_i * lam_re - (abar_r - 1.0) * lam_im) / den
    bbr = fr[..., None] * b_re - fi[..., None] * b_im
    bbi = fr[..., None] * b_im + fi[..., None] * b_re
    ab_r = pw_r[..., None] * bbr - pw_i[..., None] * bbi
    ab_i = pw_r[..., None] * bbi + pw_i[..., None] * bbr
    kern = jnp.sum(c_re[None, :, :, :, None] * ab_r[:chunk, :, None] - c_im[None, :, :, :, None] * ab_i[:chunk, :, None],
                   axis=3)
    s_idx = jnp.arange(chunk)[:, None]
    t_idx = jnp.arange(chunk)[None, :]
    lag = t_idx - s_idx
    k_st = kern[jnp.maximum(lag, 0)]
    k_st = jnp.where((lag >= 0)[:, :, None, None, None], k_st, 0.0)
    split = lambda x, axis: x.reshape(x.shape[:axis] + (N_SUPER, S5_PACK) + x.shape[axis + 1:])
    width = chunk * LANES
    n_state = S5_PACK * S5_STATE
    m_toep = split(k_st, 2).transpose(2, 0, 3, 5, 1, 4).reshape(N_SUPER, width, LANES)
    rev = chunk - 1 - jnp.arange(chunk)
    ab_rev = jnp.stack([ab_r[rev], ab_i[rev]])
    m_st = split(ab_rev, 2).transpose(2, 1, 3, 5, 0, 4).reshape(N_SUPER, width, LANES)
    ar, ai = pw_r[1:, :, None, :], pw_i[1:, :, None, :]
    co_r = c_re[None] * ar - c_im[None] * ai
    co_i = -c_re[None] * ai - c_im[None] * ar
    co = jnp.stack([co_r, co_i])
    m_out = split(co, 2).transpose(2, 0, 3, 5, 1, 4).reshape(N_SUPER, 2 * n_state, LANES)
    src = jnp.arange(LANES)[:, None]
    col = jnp.arange(width)[None, :]
    e_to = ((src // S5_CH == col // LANES) & (src % S5_CH == col % S5_CH)).astype(BF16)
    e_st = ((src // S5_STATE == col // n_state) & (src % S5_STATE == col % S5_STATE)).astype(BF16)
    last_r = pw_r[chunk].reshape(N_SUPER, 1, n_state)
    last_i = pw_i[chunk].reshape(N_SUPER, 1, n_state)
    a1 = jnp.concatenate([last_r, last_r], axis=2)
    a2 = jnp.concatenate([-last_i, last_i], axis=2)
    return m_toep, m_st, m_out, e_to, e_st, a1, a2


def s5_prompt(mid, weights):
    n_chunk = SEQ // S5_CHUNK
    n_state2 = 2 * S5_PACK * S5_STATE
    width = S5_CHUNK * LANES
    assert width == n_state2
    seq_blk = pl.BlockSpec((SEQ, LANES), lambda g, b: (b, g))
    per_g = lambda shape: pl.BlockSpec((1,) + shape, lambda g, b: (g, 0, 0))
    spread = pl.BlockSpec((LANES, width), lambda g, b: (0, 0))
    y, h_last = _call(
        _s5_prompt_body, grid=(N_SUPER, BATCH),
        in_specs=[seq_blk, per_g((width, LANES)), per_g((width, LANES)), per_g((n_state2, LANES)),
                  spread, spread, per_g((1, n_state2)), per_g((1, n_state2))],
        out_specs=[seq_blk, pl.BlockSpec((1, 1, 1, n_state2), lambda g, b: (b, g, 0, 0))],
        out_shape=[jax.ShapeDtypeStruct((M_PROMPT, S5_WIDTH), F32),
                   jax.ShapeDtypeStruct((BATCH, N_SUPER, 1, n_state2), F32)],
        scratch_shapes=[pltpu.VMEM((n_chunk, n_state2), F32), pltpu.VMEM((n_chunk, n_state2), F32)]
        + [pltpu.VMEM((width, width), BF16)] * 3,
        name="s5_prompt")(mid, *weights)
    h = h_last.reshape(BATCH, N_SUPER, 2, S5_PACK, S5_STATE).transpose(0, 1, 3, 4, 2)
    return y, h.reshape(BATCH, S5_GROUPS, S5_STATE, 2)


def _s5_post_body(y_ref, u_ref, d_ref, w_ref, b_ref, o_ref):
    z = _gelu(y_ref[...] + d_ref[...] * u_ref[...])
    lin = _dot(z.astype(BF16), w_ref[...].astype(BF16))
    o_ref[...] = (z * _sigmoid(lin + b_ref[...])).astype(o_ref.dtype)


def s5_post(y, mid, d_skip, w_glu, b_glu, tm):
    m = y.shape[0]
    row = lambda i: (i, 0)
    fixed = lambda i: (0, 0)
    return _call(
        _s5_post_body, grid=(m // tm,),
        in_specs=[pl.BlockSpec((tm, S5_WIDTH), row), pl.BlockSpec((tm, S5_WIDTH), row),
                  pl.BlockSpec((1, S5_WIDTH), fixed), pl.BlockSpec((S5_WIDTH, S5_WIDTH), fixed),
                  pl.BlockSpec((1, S5_WIDTH), fixed)],
        out_specs=pl.BlockSpec((tm, S5_WIDTH), row),
        out_shape=jax.ShapeDtypeStruct((m, S5_WIDTH), BF16),
        name="s5_post")(y, mid, d_skip.reshape(1, -1), w_glu, b_glu.reshape(1, -1))


def _s5_decode_body(u_ref, hr_ref, hi_ref, ar_ref, ai_ref, bb_ref, cc_ref, y_ref, sr_ref, si_ref):
    n = hr_ref.shape[1]
    bu = _dot(u_ref[...].astype(BF16), bb_ref[...])
    ar, ai = ar_ref[...], ai_ref[...]
    hr, hi = hr_ref[...], hi_ref[...]
    sr = bu[:, :n] + (ar * hr - ai * hi)
    si = bu[:, n:] + (ar * hi + ai * hr)
    sr_ref[...] = sr
    si_ref[...] = si
    s_cat = jnp.concatenate([sr, si], axis=1).astype(BF16)
    y_ref[...] = lax.dot_general(s_cat, cc_ref[...], (((1,), (1,)), ((), ())), preferred_element_type=F32)


def s5_decode(mid, state, lam_re, lam_im, log_dt, b_re, b_im, c_re, c_im):
    dt = jnp.exp(log_dt)[:, None]
    mag = jnp.exp(lam_re * dt)
    abar_r, abar_i = mag * jnp.cos(lam_im * dt), mag * jnp.sin(lam_im * dt)
    den = lam_re * lam_re + lam_im * lam_im
    fr = ((abar_r - 1.0) * lam_re + abar_i * lam_im) / den
    fi = (abar_i * lam_re - (abar_r - 1.0) * lam_im) / den
    bbr = fr[..., None] * b_re - fi[..., None] * b_im
    bbi = fr[..., None] * b_im + fi[..., None] * b_re
    n_state = S5_GROUPS * S5_STATE
    same_group = (jnp.arange(S5_WIDTH)[:, None] // S5_CH) == (jnp.arange(n_state)[None, :] // S5_STATE)

    def block_diag(w):
        tiled = jnp.tile(w.reshape(S5_WIDTH, S5_STATE), (1, S5_GROUPS))
        return jnp.where(same_group, tiled, 0.0)

    in_r, in_i = block_diag(bbr.transpose(0, 2, 1)), block_diag(bbi.transpose(0, 2, 1))
    bb = jnp.concatenate([in_r, in_i], axis=1).astype(BF16)
    cc_t = jnp.concatenate([block_diag(c_re), -block_diag(c_im)], axis=1).astype(BF16)
    nb = DEC_BATCH
    fixed = lambda i: (0, 0)
    st = pl.BlockSpec((nb, n_state), fixed)
    vec = pl.BlockSpec((1, n_state), fixed)
    y, sr, si = _call(
        _s5_decode_body, grid=(1,),
        in_specs=[pl.BlockSpec((nb, S5_WIDTH), fixed), st, st, vec, vec,
                  pl.BlockSpec((S5_WIDTH, 2 * n_state), fixed), pl.BlockSpec((S5_WIDTH, 2 * n_state), fixed)],
        out_specs=[pl.BlockSpec((nb, S5_WIDTH), fixed), st, st],
        out_shape=[jax.ShapeDtypeStruct((nb, S5_WIDTH), F32), jax.ShapeDtypeStruct((nb, n_state), F32),
                   jax.ShapeDtypeStruct((nb, n_state), F32)],
        name="s5_decode")(mid, state[..., 0].reshape(nb, n_state), state[..., 1].reshape(nb, n_state),
                          abar_r.reshape(1, n_state), abar_i.reshape(1, n_state), bb, cc_t)
    new_state = jnp.stack([sr.reshape(nb, S5_GROUPS, S5_STATE), si.reshape(nb, S5_GROUPS, S5_STATE)], axis=-1)
    return y, new_state


def _lru_gates(xc, wa_ref, ba_ref, wi_ref, bi_ref, c8_ref):
    xb = xc.astype(BF16)
    r = _sigmoid(_dot(xb, wa_ref[...]) + ba_ref[...])
    ig = _sigmoid(_dot(xb, wi_ref[...]) + bi_ref[...])
    log_a = c8_ref[...] * r
    a = jnp.exp(log_a)
    b = jnp.sqrt(-jnp.tanh(log_a) * (a * a + 1.0)) * (ig * xc)
    return a, b


def _lru_pre_body(x_ref, p_ref, cw_ref, cb_ref, wa_ref, ba_ref, wi_ref, bi_ref, c8_ref,
                  a_ref, b_ref, ext_ref, *, tiles_per_seq):
    tm = x_ref.shape[0]
    first = pl.program_id(0) % tiles_per_seq == 0
    ext_ref[0:8, :] = jnp.where(first, 0.0, p_ref[...])
    ext_ref[8:, :] = x_ref[...]
    xc = cb_ref[...] + cw_ref[3:4, :] * x_ref[...]
    for k in (1, 2, 3):
        xc = xc + cw_ref[3 - k:4 - k, :] * ext_ref[8 - k:8 - k + tm, :]
    a, b = _lru_gates(xc, wa_ref, ba_ref, wi_ref, bi_ref, c8_ref)
    a_ref[...] = a
    b_ref[...] = b


def _lru_scan_body(a_ref, b_ref, g_ref, y_ref, hl_ref, h_ref, p_ref):
    t_len = a_ref.shape[0]
    seg_len = t_len // LRU_SEG
    n_q = LRU_SEG // 8

    def step(t, carry):
        out = []
        for qi in range(n_q):
            h, p = carry[2 * qi], carry[2 * qi + 1]
            rows = pl.ds(qi * 8 * seg_len + t, 8, stride=seg_len)
            at = a_ref[rows, :]
            h = at * h + b_ref[rows, :]
            p = at * p
            h_ref[rows, :] = h
            p_ref[rows, :] = p
            out += [h, p]
        return tuple(out)

    init = (jnp.zeros((8, LANES), F32), jnp.ones((8, LANES), F32)) * n_q
    lax.fori_loop(0, seg_len, step, init)
    carry = jnp.zeros((1, LANES), F32)
    for s in range(LRU_SEG):
        rows = slice(s * seg_len, (s + 1) * seg_len)
        h = h_ref[rows, :] + p_ref[rows, :] * carry
        y_ref[rows, :] = (h * _gelu(g_ref[rows, :])).astype(y_ref.dtype)
        carry = h[seg_len - 1:seg_len, :]
    hl_ref[0] = carry


def lru_weights(conv_w, conv_b, w_a, b_a, w_i, b_i, lam):
    eye = jnp.eye(LRU_HEADS, dtype=F32)
    bd = lambda w: jnp.einsum("hij,hk->hikj", w, eye).reshape(LRU_WIDTH, LRU_WIDTH).astype(BF16)
    c8 = (-LRU_C) * jax.nn.softplus(-lam)
    return (conv_w, conv_b.reshape(1, -1), bd(w_a), b_a.reshape(1, -1), bd(w_i), b_i.reshape(1, -1),
            c8.reshape(1, -1))


def lru_prompt(mid, lw, tm):
    conv_w, conv_b, wa, ba, wi, bi, c8 = lw
    fixed = lambda i: (0, 0)
    vec = pl.BlockSpec((1, LRU_WIDTH), fixed)
    mat = pl.BlockSpec((LRU_WIDTH, LRU_WIDTH), fixed)
    a, b = _call(
        functools.partial(_lru_pre_body, tiles_per_seq=SEQ // tm), grid=(M_PROMPT // tm,),
        in_specs=[pl.BlockSpec((tm, LRU_WIDTH), lambda i: (i, 1)),
                  pl.BlockSpec((8, LRU_WIDTH), lambda i: (jnp.maximum(i * (tm // 8) - 1, 0), 1)),
                  pl.BlockSpec((4, LRU_WIDTH), fixed), vec, mat, vec, mat, vec, vec],
        out_specs=[pl.BlockSpec((tm, LRU_WIDTH), lambda i: (i, 0))] * 2,
        out_shape=[jax.ShapeDtypeStruct((M_PROMPT, LRU_WIDTH), F32)] * 2,
        scratch_shapes=[pltpu.VMEM((tm + 8, LRU_WIDTH), F32)],
        name="lru_pre")(mid, mid, conv_w, conv_b, wa, ba, wi, bi, c8)
    n_lane = LRU_WIDTH // LANES
    g_col0 = (S5_WIDTH + LRU_WIDTH) // LANES
    blk = lambda off: pl.BlockSpec((SEQ, LANES), lambda bi_, c: (bi_, off + c))
    y, h_last = _call(
        _lru_scan_body, grid=(BATCH, n_lane),
        in_specs=[blk(0), blk(0), blk(g_col0)],
        out_specs=[blk(0), pl.BlockSpec((1, 1, LANES), lambda bi_, c: (bi_, 0, c))],
        out_shape=[jax.ShapeDtypeStruct((M_PROMPT, LRU_WIDTH), BF16),
                   jax.ShapeDtypeStruct((BATCH, 1, LRU_WIDTH), F32)],
        scratch_shapes=[pltpu.VMEM((SEQ, LANES), F32), pltpu.VMEM((SEQ, LANES), F32)],
        name="lru_scan")(a, b, mid)
    return y, h_last.reshape(BATCH, LRU_WIDTH)


def _lru_decode_body(x_ref, g_ref, c0_ref, c1_ref, c2_ref, h0_ref, cw_ref, cb_ref, wa_ref, ba_ref,
                     wi_ref, bi_ref, c8_ref, y_ref, h_ref):
    xc = (cb_ref[...] + cw_ref[0:1, :] * c0_ref[...] + cw_ref[1:2, :] * c1_ref[...]
          + cw_ref[2:3, :] * c2_ref[...] + cw_ref[3:4, :] * x_ref[...])
    a, b = _lru_gates(xc, wa_ref, ba_ref, wi_ref, bi_ref, c8_ref)
    h = a * h0_ref[...] + b
    h_ref[...] = h
    y_ref[...] = (h * _gelu(g_ref[...])).astype(y_ref.dtype)


def lru_decode(mid, conv_buf, h0, lw):
    conv_w, conv_b, wa, ba, wi, bi, c8 = lw
    nb = DEC_BATCH
    fixed = lambda i: (0, 0)
    vec = pl.BlockSpec((1, LRU_WIDTH), fixed)
    mat = pl.BlockSpec((LRU_WIDTH, LRU_WIDTH), fixed)
    st = pl.BlockSpec((nb, LRU_WIDTH), fixed)
    return _call(
        _lru_decode_body, grid=(1,),
        in_specs=[pl.BlockSpec((nb, LRU_WIDTH), lambda i: (0, 1)), pl.BlockSpec((nb, LRU_WIDTH), lambda i: (0, 2)),
                  st, st, st, st, pl.BlockSpec((4, LRU_WIDTH), fixed), vec, mat, vec, mat, vec, vec],
        out_specs=[st, st],
        out_shape=[jax.ShapeDtypeStruct((nb, LRU_WIDTH), BF16), jax.ShapeDtypeStruct((nb, LRU_WIDTH), F32)],
        name="lru_decode")(mid, mid, conv_buf[:, 0], conv_buf[:, 1], conv_buf[:, 2], h0,
                           conv_w, conv_b, wa, ba, wi, bi, c8)


TM = 1024
TM_SMALL = 512
TM_EXPERT = 512
M_ALL = M_PROMPT + DEC_BATCH
N_EXPERT_TILES = (2 * M_ALL) // TM_EXPERT + N_EXPERTS
TM_ROUTER = M_ALL // 16
DEC_ATT_BS = (16, 8, 2)


def _token_mixers(layer, xp, xs, p, caches_t, win_prev, state_s5, state_conv, state_lru):
    n_p = rmsnorm(xp, p["norm_mix"][layer], TM_SMALL, BF16)
    n_s = rmsnorm(xs, p["norm_mix"][layer], DEC_BATCH, BF16)
    w_in = p["w_in"]
    proj = functools.partial(matmul_ws, [n_p], [n_s], w_of_x=[0], tm=TM)
    qkv_p, qkv_s = proj([(w_in, (layer,))], n_cols=OFF_MID, tn=768, out_dtype=F32, epilogue=_ep_plain,
                        name="proj_qkv")
    mid_p, mid_s = proj([(w_in[layer, :, OFF_MID:OFF_GATE], ())], tn=640, out_dtype=F32, epilogue=_ep_plain,
                        name="proj_mid")
    gates_p, gates_s = proj([(w_in[layer, :, OFF_GATE:], ())], tn=1024, out_dtype=BF16, out_dtype_s=F32,
                            epilogue=_ep_sigmoid, name="proj_gates")

    qn_p, kn_p = qk_norm(qkv_p, p["q_gain"][layer], p["k_gain"][layer], TM_SMALL, F32)
    qn_s, kn_s = qk_norm(qkv_s, p["q_gain"][layer], p["k_gain"][layer], DEC_BATCH, F32)
    outs, lses, outs_s, lses_s = [], [], [], []
    new_p, win = [], []
    for g, (window, dil) in enumerate(ATT_GROUPS):
        cols = slice(GROUP_W * g, GROUP_W * (g + 1))
        v_cols = slice(2 * ATT_WIDTH + GROUP_W * g, 2 * ATT_WIDTH + GROUP_W * (g + 1))
        o, lse = prompt_attention(g, qn_p, kn_p, qkv_p)
        outs.append(o)
        lses.append(lse)
        keep = min(window, SEQ)
        tail = lambda x, cs: (x.reshape(BATCH, SEQ, x.shape[1])[:, SEQ - keep:, cs]
                              .reshape(BATCH, keep, HEADS, HEAD_DIM))
        new_p.append(tail(kn_p, cols))
        new_p.append(tail(qkv_p, v_cols))
        cols_s = jnp.concatenate([qn_s[:, cols], kn_s[:, cols], qkv_s[:, v_cols]], axis=1)
        prev_k, prev_v = (None, None) if win_prev is None else (win_prev[2 * g], win_prev[2 * g + 1])
        o_s, lse_s, wk, wv = decode_attention(layer, g, cols_s, caches_t[2 * g], caches_t[2 * g + 1],
                                              prev_k, prev_v, DEC_ATT_BS[g])
        outs_s.append(o_s)
        lses_s.append(lse_s)
        win += [wk, wv]
    att_p = mix_groups(outs, lses, TM_SMALL, BF16)
    att_s = mix_groups(outs_s, lses_s, DEC_BATCH, BF16)

    s5p = (p["s5_lam_re"][layer], p["s5_lam_im"][layer], p["s5_log_dt"][layer], p["s5_b_re"][layer],
           p["s5_b_im"][layer], p["s5_c_re"][layer], p["s5_c_im"][layer])
    y_p, h_p = s5_prompt(mid_p, s5_prompt_weights(*s5p))
    y_s, h_s = s5_decode(mid_s, state_s5[layer], *s5p)
    glu = (p["s5_d"][layer], p["s5_w_glu"][layer], p["s5_b_glu"][layer])
    s5_p = s5_post(y_p, mid_p, *glu, TM_SMALL)
    s5_s = s5_post(y_s, mid_s, *glu, DEC_BATCH)
    new_p.append(h_p)
    new_s = [h_s]

    lw = lru_weights(p["lru_conv_w"][layer], p["lru_conv_b"][layer], p["lru_w_a"][layer], p["lru_b_a"][layer],
                     p["lru_w_i"][layer], p["lru_b_i"][layer], p["lru_lam"][layer])
    lru_p, h_lp = lru_prompt(mid_p, lw, 512)
    lru_s, h_ls = lru_decode(mid_s, state_conv[layer], state_lru[layer], lw)
    x_lru = slice(S5_WIDTH, S5_WIDTH + LRU_WIDTH)
    new_p.append(mid_p.reshape(BATCH, SEQ, MID_WIDTH)[:, SEQ - 3:, x_lru])
    new_p.append(h_lp)
    new_s.append(jnp.concatenate([state_conv[layer, :, 1:], mid_s[:, None, x_lru]], axis=1))
    new_s.append(h_ls)

    tn = 1024
    gate_tiles = D_MODEL // tn
    merged_p, merged_s = matmul_ws(
        [att_p, s5_p, lru_p], [att_s, s5_s, lru_s],
        [(p["w_br_attn"], (layer,)), (p["w_br_s5"], (layer,)), (p["w_br_lru"], (layer,))],
        w_of_x=[0, 1, 2], tm=TM, tn=tn, out_dtype=BF16, epilogue=_ep_merge,
        tile_extras=[(gates_p, gates_s, 0), (gates_p, gates_s, gate_tiles), (gates_p, gates_s, 2 * gate_tiles)],
        name="merge")
    xp, xs = matmul_ws([merged_p], [merged_s], [(p["w_o"], (layer,))], w_of_x=[0], tm=TM, tn=tn,
                       out_dtype=F32, epilogue=_ep_residual, tile_extras=[(xp, xs, 0)], name="out_proj")
    return xp, xs, new_p, new_s, win


def _dense_ffn(xp, xs, p, layer, j):
    n_p = rmsnorm(xp, p["norm_ffn"][layer], TM_SMALL, BF16)
    n_s = rmsnorm(xs, p["norm_ffn"][layer], DEC_BATCH, BF16)
    h_p, h_s = matmul_ws([n_p], [n_s], [(p["ffn_w_gate"], (j,)), (p["ffn_w_up"], (j,))], w_of_x=[0, 0],
                         tm=TM, tn=512, out_dtype=BF16, epilogue=_ep_swiglu, name="ffn_up")
    return matmul_ws([h_p], [h_s], [(p["ffn_w_down"], (j,))], w_of_x=[0], tm=512, tn=512, out_dtype=F32,
                     epilogue=_ep_residual, tile_extras=[(xp, xs, 0)], name="ffn_down")


def _moe_ffn(xp, xs, p, layer, j):
    router = (p["norm_ffn"][layer], p["moe_router_w"][j], p["moe_router_b"][j])
    n, comb, idx = rmsnorm_router(jnp.concatenate([xp, xs], axis=0), *router, TM_ROUTER)
    idx = idx[:, :2]
    m = M_ALL
    top_e = idx.reshape(-1)
    top_w = jnp.take_along_axis(comb, idx, axis=1).reshape(-1)
    counts = jnp.sum(top_e[:, None] == jnp.arange(N_EXPERTS)[None, :], axis=0).astype(jnp.int32)
    padded = ((counts + TM_EXPERT - 1) // TM_EXPERT) * TM_EXPERT
    pad_end = jnp.cumsum(padded)
    pad_start = pad_end - padded
    start = jnp.cumsum(counts) - counts
    order = jnp.argsort(top_e, stable=True).astype(jnp.int32)
    rank = jnp.argsort(order).astype(jnp.int32)
    dest = pad_start[top_e] + rank - start[top_e]
    n_rows = N_EXPERT_TILES * TM_EXPERT
    tile_start = jnp.arange(N_EXPERT_TILES, dtype=jnp.int32) * TM_EXPERT
    tile_expert = jnp.minimum(jnp.sum(tile_start[:, None] >= pad_end[None, :], axis=1), N_EXPERTS - 1)
    tile_expert = tile_expert.astype(jnp.int32)
    n_used = (pad_end[-1] // TM_EXPERT).reshape(1).astype(jnp.int32)
    row_e = jnp.repeat(tile_expert, TM_EXPERT)
    within = jnp.arange(n_rows, dtype=jnp.int32) - pad_start[row_e]
    row_valid = within < counts[row_e]
    row_pair = order[jnp.clip(start[row_e] + within, 0, 2 * m - 1)]
    src_token = jnp.where(row_valid, row_pair // 2, 0)
    row_w = jnp.where(row_valid, top_w[row_pair], 0.0)
    x_sorted = jnp.take(n, src_token, axis=0, mode="clip")
    gmm = functools.partial(grouped_matmul_ws, tile_expert, n_used, lead=(j,), tm=TM_EXPERT)
    gate = gmm(x_sorted, p["moe_w_gate"], tn=1408, out_dtype=BF16, epilogue=_ep_plain, name="moe_gate")
    h = gmm(x_sorted, p["moe_w_up"], tn=1408, out_dtype=BF16, epilogue=_ep_silu_times, tile_extras=[gate],
            name="moe_up")
    ys = gmm(h, p["moe_w_down"], tn=1024, out_dtype=F32, epilogue=_ep_scale_rows,
             col_extras=[row_w.reshape(-1, 1)], name="moe_down")
    dest2 = dest.reshape(m, 2)
    y0 = jnp.take(ys, dest2[:, 0], axis=0, mode="clip")
    y1 = jnp.take(ys, dest2[:, 1], axis=0, mode="clip")
    return xp + (y0[:M_PROMPT] + y1[:M_PROMPT]), xs + (y0[M_PROMPT:] + y1[M_PROMPT:])


def kernel(x_prompt, x_sample, cache_k_w128, cache_v_w128, cache_k_w512, cache_v_w512, cache_k_w2048, cache_v_w2048, state_s5, state_conv, state_lru, norm_mix, w_in, q_gain, k_gain, s5_lam_re, s5_lam_im, s5_log_dt, s5_b_re, s5_b_im, s5_c_re, s5_c_im, s5_d, s5_w_glu, s5_b_glu, lru_conv_w, lru_conv_b, lru_w_a, lru_b_a, lru_w_i, lru_b_i, lru_lam, w_br_attn, w_br_s5, w_br_lru, w_o, norm_ffn, ffn_w_gate, ffn_w_up, ffn_w_down, moe_router_w, moe_router_b, moe_w_gate, moe_w_up, moe_w_down):
    p = dict(norm_mix=norm_mix, w_in=w_in, q_gain=q_gain, k_gain=k_gain, s5_lam_re=s5_lam_re,
             s5_lam_im=s5_lam_im, s5_log_dt=s5_log_dt, s5_b_re=s5_b_re, s5_b_im=s5_b_im, s5_c_re=s5_c_re,
             s5_c_im=s5_c_im, s5_d=s5_d, s5_w_glu=s5_w_glu, s5_b_glu=s5_b_glu, lru_conv_w=lru_conv_w,
             lru_conv_b=lru_conv_b, lru_w_a=lru_w_a, lru_b_a=lru_b_a, lru_w_i=lru_w_i, lru_b_i=lru_b_i,
             lru_lam=lru_lam, w_br_attn=w_br_attn, w_br_s5=w_br_s5, w_br_lru=w_br_lru, w_o=w_o,
             norm_ffn=norm_ffn, ffn_w_gate=ffn_w_gate, ffn_w_up=ffn_w_up, ffn_w_down=ffn_w_down,
             moe_router_w=moe_router_w, moe_router_b=moe_router_b, moe_w_gate=moe_w_gate,
             moe_w_up=moe_w_up, moe_w_down=moe_w_down)
    caches_t = [jnp.transpose(c, (0, 1, 3, 4, 2)) for c in
                (cache_k_w128, cache_v_w128, cache_k_w512, cache_v_w512, cache_k_w2048, cache_v_w2048)]
    xp = x_prompt.reshape(M_PROMPT, D_MODEL)
    xs = x_sample.reshape(DEC_BATCH, D_MODEL)
    states_p = [[] for _ in range(9)]
    states_s = [[] for _ in range(3)]
    win = None
    for layer in range(DEPTH):
        xp, xs, new_p, new_s, win = _token_mixers(layer, xp, xs, p, caches_t, win, state_s5, state_conv,
                                                  state_lru)
        if layer % 2 == 0:
            xp, xs = _dense_ffn(xp, xs, p, layer, layer // 2)
        else:
            xp, xs = _moe_ffn(xp, xs, p, layer, layer // 2)
        for i in range(9):
            states_p[i].append(new_p[i])
        for i in range(3):
            states_s[i].append(new_s[i])
    y_prompt = xp.reshape(BATCH, SEQ, D_MODEL)
    y_sample = xs.reshape(DEC_BATCH, 1, D_MODEL)
    windows = [jnp.transpose(w, (0, 1, 4, 2, 3)) for w in win]
    return (y_prompt, y_sample, *[jnp.stack(a) for a in states_p], *windows,
            *[jnp.stack(a) for a in states_s])
```

```python
import functools
import math

import jax
import jax.numpy as jnp
from jax import lax
from jax.experimental import pallas as pl
from jax.experimental.pallas import tpu as pltpu

F32 = jnp.float32
BF16 = jnp.bfloat16

D_MODEL = 2048
BATCH = 2
SEQ = 4096
DEPTH = 2
DEC_BATCH = 128
HEAD_DIM = 64
HEADS = 4
GROUP_W = HEADS * HEAD_DIM
ATT_GROUPS = ((128, 1), (512, 4), (2048, 16))
N_GROUPS = 3
ATT_WIDTH = N_GROUPS * GROUP_W
N_KEYS = 128
S5_WIDTH = 640
S5_GROUPS = 40
S5_CH = 16
S5_STATE = 64
LRU_WIDTH = 640
LRU_HEADS = 8
LRU_C = 8.0
OFF_MID = 3 * ATT_WIDTH
MID_WIDTH = S5_WIDTH + 2 * LRU_WIDTH
OFF_GATE = OFF_MID + MID_WIDTH
N_EXPERTS = 8
D_FF_EXPERT = 2816
EPS = 1e-6
LANES = 128

M_PROMPT = BATCH * SEQ

S5_CHUNK = 8
S5_PACK = LANES // S5_CH
N_SUPER = S5_GROUPS // S5_PACK
LRU_SEG = 32
NEG_BIG = -1e30

VMEM_LIMIT_BYTES = 56 * 1024 * 1024


def _call(body, *, grid, in_specs, out_specs, out_shape, name, scratch_shapes=(), num_scalar_prefetch=0,
          input_output_aliases=None):
    params = pltpu.CompilerParams(dimension_semantics=("arbitrary",) * len(grid),
                                  vmem_limit_bytes=VMEM_LIMIT_BYTES)
    kwargs = {}
    if input_output_aliases:
        kwargs["input_output_aliases"] = input_output_aliases
    if num_scalar_prefetch:
        grid_spec = pltpu.PrefetchScalarGridSpec(
            num_scalar_prefetch=num_scalar_prefetch, grid=grid, in_specs=in_specs,
            out_specs=out_specs, scratch_shapes=scratch_shapes)
        return pl.pallas_call(body, grid_spec=grid_spec, out_shape=out_shape,
                              compiler_params=params, name=name, **kwargs)
    return pl.pallas_call(body, grid=grid, in_specs=in_specs, out_specs=out_specs,
                          out_shape=out_shape, scratch_shapes=scratch_shapes,
                          compiler_params=params, name=name, **kwargs)


def _dot(a, b):
    return jnp.dot(a, b, preferred_element_type=F32)


def _dot_sel(x, sel, parts=2):
    acc = None
    r = x
    for _ in range(parts):
        piece = r.astype(BF16)
        r = r - piece.astype(F32)
        d = _dot(piece, sel)
        acc = d if acc is None else acc + d
    return acc


def _sigmoid(x):
    return 1.0 / (1.0 + jnp.exp(-x))


def _gelu(x):
    c = math.sqrt(2.0 / math.pi)
    return 0.5 * x * (1.0 + jnp.tanh(c * (x + 0.044715 * (x * x * x))))


def _rmsnorm_body(x_ref, g_ref, o_ref):
    x = x_ref[...]
    y = x * lax.rsqrt(jnp.mean(x * x, axis=-1, keepdims=True) + EPS)
    o_ref[...] = (y * g_ref[...]).astype(o_ref.dtype)


def _rmsnorm_router_body(x_ref, g_ref, rw_ref, rb_ref, o_ref, comb_ref, idx_ref):
    x = x_ref[...]
    y = x * lax.rsqrt(jnp.mean(x * x, axis=-1, keepdims=True) + EPS)
    xn = y * g_ref[...]
    o_ref[...] = xn.astype(o_ref.dtype)
    logits = _dot(xn.astype(BF16), rw_ref[...].astype(BF16)) + rb_ref[...]
    lane = lax.broadcasted_iota(jnp.int32, logits.shape, 1).astype(F32)
    m1 = jnp.max(logits, axis=1, keepdims=True)
    i1 = jnp.min(jnp.where(logits == m1, lane, float(LANES)), axis=1, keepdims=True)
    rest = jnp.where(lane == i1, -jnp.inf, logits)
    m2 = jnp.max(rest, axis=1, keepdims=True)
    i2 = jnp.min(jnp.where(rest == m2, lane, float(LANES)), axis=1, keepdims=True)
    e = jnp.exp(m2 - m1)
    w1 = 1.0 / (1.0 + e)
    w2 = e / (1.0 + e)
    comb_ref[...] = jnp.where(lane == i1, w1, 0.0) + jnp.where(lane == i2, w2, 0.0)
    idx_ref[...] = jnp.where(lane == 0.0, i1, jnp.where(lane == 1.0, i2, 0.0)).astype(jnp.int32)


def rmsnorm(x, gain, tm, out_dtype):
    m, d = x.shape
    return _call(
        _rmsnorm_body, grid=(m // tm,),
        in_specs=[pl.BlockSpec((tm, d), lambda i: (i, 0)), pl.BlockSpec((1, d), lambda i: (0, 0))],
        out_specs=pl.BlockSpec((tm, d), lambda i: (i, 0)),
        out_shape=jax.ShapeDtypeStruct((m, d), out_dtype), name="rmsnorm")(x, gain.reshape(1, d))


def rmsnorm_router(x, gain, router_w, router_b, tm):
    m, d = x.shape
    ne = router_w.shape[1]
    rw = jnp.zeros((d, LANES), F32).at[:, :ne].set(router_w)
    rb = jnp.full((1, LANES), NEG_BIG, F32).at[0, :ne].set(router_b)
    row = lambda i: (i, 0)
    fixed = lambda i: (0, 0)
    return _call(
        _rmsnorm_router_body, grid=(m // tm,),
        in_specs=[pl.BlockSpec((tm, d), row), pl.BlockSpec((1, d), fixed),
                  pl.BlockSpec((d, LANES), fixed), pl.BlockSpec((1, LANES), fixed)],
        out_specs=[pl.BlockSpec((tm, d), row), pl.BlockSpec((tm, LANES), row), pl.BlockSpec((tm, LANES), row)],
        out_shape=[jax.ShapeDtypeStruct((m, d), F32), jax.ShapeDtypeStruct((m, LANES), F32),
                   jax.ShapeDtypeStruct((m, LANES), jnp.int32)],
        name="rmsnorm_router")(x, gain.reshape(1, d), rw, rb)


def _mm_body(*refs, n_x, w_of_x, n_te, n_re, epilogue, n_ptiles):
    n_w = len(w_of_x)
    pos = 0
    xp_refs = refs[pos:pos + n_x]; pos += n_x
    xs_refs = refs[pos:pos + n_x]; pos += n_x
    w_refs = refs[pos:pos + n_w]; pos += n_w
    tep_refs = refs[pos:pos + n_te]; pos += n_te
    tes_refs = refs[pos:pos + n_te]; pos += n_te
    re_refs = refs[pos:pos + n_re]; pos += n_re
    op_ref, os_ref = refs[pos], refs[pos + 1]; pos += 2
    wb_refs = refs[pos:pos + n_w]
    i = pl.program_id(1)

    @pl.when(i == 0)
    def _():
        for w_ref, wb_ref in zip(w_refs, wb_refs):
            wb_ref[...] = w_ref[...].astype(BF16)

    def rows(x_refs, te_refs, o_ref):
        accs = [_dot(x_refs[xi][...].astype(BF16), wb_ref[...]) for xi, wb_ref in zip(w_of_x, wb_refs)]
        extras = [e[...] for e in te_refs] + [e[...] for e in re_refs]
        o_ref[...] = epilogue(accs, extras).astype(o_ref.dtype)

    @pl.when(i < n_ptiles)
    def _():
        rows(xp_refs, tep_refs, op_ref)

    @pl.when(i == n_ptiles)
    def _():
        rows(xs_refs, tes_refs, os_ref)


def matmul_ws(xps, xss, ws, *, w_of_x, tm, tn, out_dtype, epilogue, tile_extras=(), row_extras=(), name,
              out_dtype_s=None, n_cols=None):
    mp = xps[0].shape[0]
    ms = xss[0].shape[0]
    n = n_cols or ws[0][0].shape[-1]
    n_ptiles = mp // tm
    last = n_ptiles - 1
    grid = (n // tn, n_ptiles + 1)
    in_specs = [pl.BlockSpec((tm, x.shape[1]), lambda j, i: (jnp.minimum(i, last), 0)) for x in xps]
    in_specs += [pl.BlockSpec((ms, x.shape[1]), lambda j, i: (0, 0)) for x in xss]
    scratch = []
    for w, lead in ws:
        k = w.shape[-2]
        block = (None,) * len(lead) + (k, tn)
        in_specs.append(pl.BlockSpec(block, lambda j, i, lead=lead: lead + (0, j)))
        scratch.append(pltpu.VMEM((k, tn), BF16))
    for _, _, off in tile_extras:
        in_specs.append(pl.BlockSpec((tm, tn), lambda j, i, off=off: (jnp.minimum(i, last), off + j)))
    for _, _, off in tile_extras:
        in_specs.append(pl.BlockSpec((ms, tn), lambda j, i, off=off: (0, off + j)))
    for _ in row_extras:
        in_specs.append(pl.BlockSpec((1, tn), lambda j, i: (0, j)))
    body = functools.partial(_mm_body, n_x=len(xps), w_of_x=tuple(w_of_x), n_te=len(tile_extras),
                             n_re=len(row_extras), epilogue=epilogue, n_ptiles=n_ptiles)
    return _call(
        body, grid=grid, in_specs=in_specs,
        out_specs=[pl.BlockSpec((tm, tn), lambda j, i: (jnp.minimum(i, last), j)),
                   pl.BlockSpec((ms, tn), lambda j, i: (0, j))],
        out_shape=[jax.ShapeDtypeStruct((mp, n), out_dtype),
                   jax.ShapeDtypeStruct((ms, n), out_dtype_s or out_dtype)],
        scratch_shapes=scratch, name=name,
    )(*xps, *xss, *[w for w, _ in ws], *[a for a, _, _ in tile_extras], *[a for _, a, _ in tile_extras],
      *row_extras)


def _ep_plain(accs, extras):
    return accs[0]


def _ep_sigmoid(accs, extras):
    return _sigmoid(accs[0])


def _ep_residual(accs, extras):
    return extras[0] + accs[0]


def _ep_swiglu(accs, extras):
    g, u = accs
    return (g * _sigmoid(g)) * u


def _ep_merge(accs, extras):
    return (extras[0].astype(F32) * accs[0] + extras[1].astype(F32) * accs[1]
            + extras[2].astype(F32) * accs[2])


def _gmm_body(te_ref, nu_ref, *refs, n_te, n_ce, epilogue):
    x_ref, w_ref = refs[0], refs[1]
    extra_refs = refs[2:2 + n_te + n_ce]
    o_ref = refs[2 + n_te + n_ce]
    wb_ref = refs[3 + n_te + n_ce]
    i = pl.program_id(1)
    new_expert = jnp.logical_or(i == 0, te_ref[i] != te_ref[jnp.maximum(i - 1, 0)])

    @pl.when(new_expert)
    def _():
        wb_ref[...] = w_ref[...].astype(BF16)

    @pl.when(i < nu_ref[0])
    def _():
        acc = _dot(x_ref[...].astype(BF16), wb_ref[...])
        o_ref[...] = epilogue([acc], [e[...] for e in extra_refs]).astype(o_ref.dtype)

    @pl.when(i >= nu_ref[0])
    def _():
        o_ref[...] = jnp.zeros(o_ref.shape, o_ref.dtype)


def grouped_matmul_ws(tile_expert, n_used, x, w, lead, *, tm, tn, out_dtype, epilogue, tile_extras=(),
                      col_extras=(), name):
    m = x.shape[0]
    k, n = w.shape[-2:]
    grid = (n // tn, m // tm)
    block = (None,) * (len(lead) + 1) + (k, tn)
    in_specs = [pl.BlockSpec((tm, x.shape[1]), lambda j, i, te, nu: (i, 0)),
                pl.BlockSpec(block, lambda j, i, te, nu: lead + (te[i], 0, j))]
    in_specs += [pl.BlockSpec((tm, tn), lambda j, i, te, nu: (i, j)) for _ in tile_extras]
    in_specs += [pl.BlockSpec((tm, 1), lambda j, i, te, nu: (i, 0)) for _ in col_extras]
    body = functools.partial(_gmm_body, n_te=len(tile_extras), n_ce=len(col_extras), epilogue=epilogue)
    return _call(
        body, grid=grid, in_specs=in_specs,
        out_specs=pl.BlockSpec((tm, tn), lambda j, i, te, nu: (i, j)),
        out_shape=jax.ShapeDtypeStruct((m, n), out_dtype), scratch_shapes=[pltpu.VMEM((k, tn), BF16)],
        num_scalar_prefetch=2, name=name,
    )(tile_expert, n_used, x, w, *tile_extras, *col_extras)


def _ep_silu_times(accs, extras):
    g = extras[0].astype(F32)
    return (g * _sigmoid(g)) * accs[0]


def _ep_scale_rows(accs, extras):
    return extras[0] * accs[0]


def _qknorm_body(q_ref, k_ref, qg_ref, kg_ref, seg_ref, qo_ref, ko_ref):
    seg = seg_ref[...]
    for src, gain, dst, scale in ((q_ref, qg_ref, qo_ref, HEAD_DIM ** -0.5), (k_ref, kg_ref, ko_ref, 1.0)):
        for g in range(N_GROUPS):
            cols = slice(GROUP_W * g, GROUP_W * (g + 1))
            x = src[:, cols]
            ms = _dot_sel(x * x, seg, 3)
            y = x * lax.rsqrt(ms + EPS) * gain[:, cols]
            dst[:, cols] = (y * scale).astype(dst.dtype)


def qk_norm(qkv, q_gain, k_gain, tm, q_dtype):
    m = qkv.shape[0]
    head = jnp.arange(GROUP_W) // HEAD_DIM
    seg = jnp.where(head[:, None] == head[None, :], 1.0 / HEAD_DIM, 0.0).astype(BF16)
    qg = jnp.broadcast_to(q_gain[:, None, :], (N_GROUPS, HEADS, HEAD_DIM)).reshape(1, ATT_WIDTH)
    kg = jnp.broadcast_to(k_gain[:, None, :], (N_GROUPS, HEADS, HEAD_DIM)).reshape(1, ATT_WIDTH)
    fixed = lambda i: (0, 0)
    return _call(
        _qknorm_body, grid=(m // tm,),
        in_specs=[pl.BlockSpec((tm, ATT_WIDTH), lambda i: (i, 0)), pl.BlockSpec((tm, ATT_WIDTH), lambda i: (i, 1)),
                  pl.BlockSpec((1, ATT_WIDTH), fixed), pl.BlockSpec((1, ATT_WIDTH), fixed),
                  pl.BlockSpec((GROUP_W, GROUP_W), fixed)],
        out_specs=[pl.BlockSpec((tm, ATT_WIDTH), lambda i: (i, 0)), pl.BlockSpec((tm, ATT_WIDTH), lambda i: (i, 0))],
        out_shape=[jax.ShapeDtypeStruct((m, ATT_WIDTH), q_dtype), jax.ShapeDtypeStruct((m, ATT_WIDTH), F32)],
        name="qk_norm")(qkv, qkv, qg, kg, seg)


def _attn_body(*refs, dil):
    q_refs, kp_refs, kc_refs, vp_refs, vc_refs = (refs[2 * n:2 * n + 2] for n in range(5))
    o_ref, lse_ref = refs[10], refs[11]
    o_stage, lse_stage = refs[12:14], refs[14:16]
    a = pl.program_id(1)
    tq = N_KEYS
    r = lax.broadcasted_iota(jnp.int32, (HEADS * tq, 2 * tq), 0) % tq
    c = lax.broadcasted_iota(jnp.int32, (HEADS * tq, 2 * tq), 1)
    valid = (c >= r) & (c <= r + N_KEYS) & ((a > 0) | (c >= tq))
    head = lax.broadcasted_iota(jnp.int32, (1, GROUP_W), 1) // HEAD_DIM

    def residue(res, carry):
        rows = pl.ds(res, tq, stride=dil) if dil > 1 else slice(None)
        load = lambda halves: jnp.concatenate([halves[0][rows, :], halves[1][rows, :]], axis=1)
        q = load(q_refs)
        k2 = jnp.concatenate([load(kp_refs), load(kc_refs)], axis=0).astype(BF16)
        v2 = jnp.concatenate([load(vp_refs), load(vc_refs)], axis=0).astype(BF16)
        q4 = jnp.concatenate([jnp.where(head == h, q, 0.0) for h in range(HEADS)], axis=0).astype(BF16)
        s = lax.dot_general(q4, k2, (((1,), (1,)), ((), ())), preferred_element_type=F32)
        s = jnp.where(valid, s, NEG_BIG)
        m = jnp.max(s, axis=1, keepdims=True)
        p = jnp.exp(s - m)
        l = jnp.sum(p, axis=1, keepdims=True)
        o4 = _dot(p.astype(BF16), v2) / l
        lse4 = m + jnp.log(l)
        o = jnp.zeros((tq, GROUP_W), F32)
        lse = jnp.zeros((tq, GROUP_W), F32)
        for h in range(HEADS):
            blk = slice(h * tq, (h + 1) * tq)
            o = jnp.where(head == h, o4[blk, :], o)
            lse = jnp.where(head == h, lse4[blk, :], lse)
        for half in range(2):
            lanes = slice(LANES * half, LANES * (half + 1))
            o_stage[half][rows, :] = o[:, lanes]
            lse_stage[half][rows, :] = lse[:, lanes]
        return carry

    if dil == 1:
        residue(0, 0)
    else:
        lax.fori_loop(0, dil, residue, 0, unroll=2)
    for half in range(2):
        lanes = slice(LANES * half, LANES * (half + 1))
        o_ref[:, lanes] = o_stage[half][...]
        lse_ref[:, lanes] = lse_stage[half][...]


def prompt_attention(g, qn, kn, qkv):
    _, dil = ATT_GROUPS[g]
    span = N_KEYS * dil
    n_span = SEQ // span
    halves_per_group = GROUP_W // LANES
    cur = lambda col: (lambda b, a: (b * n_span + a, col))
    prev = lambda col: (lambda b, a: (b * n_span + jnp.maximum(a - 1, 0), col))
    blk = (span, LANES)
    qk_cols = [halves_per_group * g + half for half in range(halves_per_group)]
    v_cols = [halves_per_group * (2 * N_GROUPS + g) + half for half in range(halves_per_group)]
    in_specs = ([pl.BlockSpec(blk, cur(c)) for c in qk_cols] + [pl.BlockSpec(blk, prev(c)) for c in qk_cols]
                + [pl.BlockSpec(blk, cur(c)) for c in qk_cols] + [pl.BlockSpec(blk, prev(c)) for c in v_cols]
                + [pl.BlockSpec(blk, cur(c)) for c in v_cols])
    out = pl.BlockSpec((span, GROUP_W), lambda b, a: (b * n_span + a, 0))
    return _call(
        functools.partial(_attn_body, dil=dil), grid=(BATCH, n_span),
        in_specs=in_specs, out_specs=[out, out],
        out_shape=[jax.ShapeDtypeStruct((M_PROMPT, GROUP_W), F32)] * 2,
        scratch_shapes=[pltpu.VMEM((span, LANES), F32)] * 4,
        name="prompt_attention")(qn, qn, kn, kn, kn, kn, qkv, qkv, qkv, qkv)


def _dec_attn_body(*refs, dil, aliased):
    n_in = 6 if aliased else 4
    qkv_ref, row_ref, ck_ref, cv_ref = refs[:4]
    o_ref, lse_ref, nk_ref, nv_ref, cols_ref, acc_ref, own_ref, den_ref = refs[n_in:]
    bs = ck_ref.shape[1]
    ones = jnp.ones((8, LANES), BF16)
    n_chunk = ck_ref.shape[4] // LANES
    lane = lax.broadcasted_iota(jnp.int32, (1, LANES), 1)
    key_lane = (lane % dil) == 0
    not_last = lax.broadcasted_iota(jnp.int32, (HEAD_DIM, LANES), 1) < LANES - 1
    sample_row = lax.broadcasted_iota(jnp.int32, (qkv_ref.shape[1], LANES), 0)
    qkv_t = qkv_ref[...]
    first = pl.program_id(0) * bs
    def spread(s):
        cols_ref[s] = _dot_sel(qkv_t, (sample_row == first + s).astype(BF16), 3)

    def finish(s):
        acc_t, rest = None, acc_ref[s]
        for _ in range(3):
            piece = rest.astype(BF16)
            rest = rest - piece.astype(F32)
            part = lax.dot_general(ones, piece, (((1,), (1,)), ((), ())), preferred_element_type=F32)
            acc_t = part if acc_t is None else acc_t + part
        o_ref[pl.ds(first + s, 1), :] = (acc_t[0:1, :] + own_ref[s:s + 1, :]) / den_ref[s:s + 1, :]

    bulk = n_chunk == 1
    if bulk:
        for s in range(bs):
            spread(s)
    for s in range(bs):
        if not bulk:
            spread(s)
        sample = first + s
        sample_rows = row_ref[pl.ds(sample, 1), :]
        accs, own_heads, den_heads, lse_heads = [], [], [], []
        for h in range(HEADS):
            rows = slice(HEAD_DIM * h, HEAD_DIM * (h + 1))
            qb = cols_ref[s, rows, :]
            kn = cols_ref[s, GROUP_W + HEAD_DIM * h:GROUP_W + HEAD_DIM * (h + 1), :]
            vn = cols_ref[s, 2 * GROUP_W + HEAD_DIM * h:2 * GROUP_W + HEAD_DIM * (h + 1), :]
            chunk = lambda ref, c: ref[0, s, h, :, c * LANES:(c + 1) * LANES]
            scores = [jnp.where(key_lane, jnp.sum(chunk(ck_ref, c) * qb, axis=0, keepdims=True), NEG_BIG)
                      for c in range(n_chunk)]
            s_self = jnp.sum(kn * qb, axis=0, keepdims=True)
            m = s_self
            for sc in scores:
                m = jnp.maximum(m, jnp.max(sc, axis=1, keepdims=True))
            p_self = jnp.exp(s_self - m)
            l = p_self
            acc = None
            for c in range(n_chunk):
                p = jnp.exp(scores[c] - m)
                l = l + jnp.sum(p, axis=1, keepdims=True)
                pv = p * chunk(cv_ref, c)
                acc = pv if acc is None else acc + pv
            accs.append(acc)
            own_heads.append(p_self[:, :HEAD_DIM])
            den_heads.append(l[:, :HEAD_DIM])
            lse_heads.append((m + jnp.log(l))[:, :HEAD_DIM])
            for src, dst, new in ((ck_ref, nk_ref, kn), (cv_ref, nv_ref, vn)):
                rolled = [pltpu.roll(chunk(src, c), LANES - 1, axis=1) for c in range(n_chunk)] + [new]
                for c in range(n_chunk):
                    dst[0, s, h, :, c * LANES:(c + 1) * LANES] = jnp.where(not_last, rolled[c], rolled[c + 1])
        acc_ref[s] = jnp.concatenate(accs, axis=0)
        own_ref[s:s + 1, :] = jnp.concatenate(own_heads, axis=1) * sample_rows[:, 2 * GROUP_W:]
        den_ref[s:s + 1, :] = jnp.concatenate(den_heads, axis=1)
        lse_ref[pl.ds(sample, 1), :] = jnp.concatenate(lse_heads, axis=1)
        if not bulk:
            finish(s)
    if bulk:
        for s in range(bs):
            finish(s)


def decode_attention(layer, g, qkv_rows, cache_k, cache_v, prev_k, prev_v, bs):
    window, dil = ATT_GROUPS[g]
    cblk = (1, bs, HEADS, HEAD_DIM, window)
    cmap = lambda i: (layer, i, 0, 0, 0)
    whole = lambda shape: pl.BlockSpec(shape, lambda i: (0, 0))
    in_specs = [whole((3 * GROUP_W, DEC_BATCH)), whole((DEC_BATCH, 3 * GROUP_W)),
                pl.BlockSpec(cblk, cmap), pl.BlockSpec(cblk, cmap)]
    args = [qkv_rows.T, qkv_rows, cache_k, cache_v]
    aliases = None
    if prev_k is not None:
        in_specs += [pl.BlockSpec(memory_space=pl.ANY), pl.BlockSpec(memory_space=pl.ANY)]
        args += [prev_k, prev_v]
        aliases = {4: 2, 5: 3}
    out = whole((DEC_BATCH, GROUP_W))
    return _call(
        functools.partial(_dec_attn_body, dil=dil, aliased=prev_k is not None), grid=(DEC_BATCH // bs,),
        in_specs=in_specs,
        out_specs=[out, out, pl.BlockSpec(cblk, cmap), pl.BlockSpec(cblk, cmap)],
        out_shape=[jax.ShapeDtypeStruct((DEC_BATCH, GROUP_W), F32)] * 2
        + [jax.ShapeDtypeStruct(cache_k.shape, F32)] * 2,
        scratch_shapes=[pltpu.VMEM((bs, 3 * GROUP_W, LANES), F32), pltpu.VMEM((bs, GROUP_W, LANES), F32),
                        pltpu.VMEM((max(bs, 8), GROUP_W), F32), pltpu.VMEM((max(bs, 8), GROUP_W), F32)],
        input_output_aliases=aliases, name="decode_attention")(*args)


def _mix_body(o0_ref, o1_ref, o2_ref, l0_ref, l1_ref, l2_ref, att_ref):
    l0, l1, l2 = l0_ref[...], l1_ref[...], l2_ref[...]
    m = jnp.maximum(jnp.maximum(l0, l1), l2)
    e0, e1, e2 = jnp.exp(l0 - m), jnp.exp(l1 - m), jnp.exp(l2 - m)
    num = e0 * o0_ref[...] + e1 * o1_ref[...] + e2 * o2_ref[...]
    att_ref[...] = (num / (e0 + e1 + e2)).astype(att_ref.dtype)


def mix_groups(outs, lses, tm, out_dtype):
    m = outs[0].shape[0]
    spec = pl.BlockSpec((tm, GROUP_W), lambda i: (i, 0))
    return _call(_mix_body, grid=(m // tm,), in_specs=[spec] * 6, out_specs=spec,
                 out_shape=jax.ShapeDtypeStruct((m, GROUP_W), out_dtype), name="mix_groups")(*outs, *lses)


def _s5_prompt_body(u_ref, mt_ref, ms_ref, mo_ref, et_ref, es_ref, a1_ref, a2_ref, y_ref, hl_ref,
                    s_ref, hp_ref, toep_ref, wst_ref, wout_ref):
    @pl.when(pl.program_id(1) == 0)
    def _():
        def expand(m_ref, e_ref, row_div, col_div, dst_ref):
            full = _dot(m_ref[0].astype(BF16), e_ref[...])
            row_g = (lax.broadcasted_iota(jnp.int32, full.shape, 0) // row_div) % S5_PACK
            col_g = (lax.broadcasted_iota(jnp.int32, full.shape, 1) // col_div) % S5_PACK
            dst_ref[...] = jnp.where(row_g == col_g, full, 0.0).astype(BF16)

        expand(mt_ref, et_ref, S5_CH, S5_CH, toep_ref)
        expand(ms_ref, es_ref, S5_CH, S5_STATE, wst_ref)
        expand(mo_ref, et_ref, S5_STATE, S5_CH, wout_ref)

    n_chunk = s_ref.shape[0]
    u8 = jnp.concatenate([u_ref[pl.ds(s, n_chunk, stride=S5_CHUNK), :] for s in range(S5_CHUNK)],
                         axis=1).astype(BF16)
    s_ref[...] = _dot(u8, wst_ref[...])
    a1, a2 = a1_ref[0], a2_ref[0]
    half = a1.shape[1] // 2

    def step(c, h):
        hp_ref[pl.ds(c, 1), :] = h
        swapped = jnp.concatenate([h[:, half:], h[:, :half]], axis=1)
        return a1 * h + a2 * swapped + s_ref[pl.ds(c, 1), :]

    hl_ref[0, 0] = lax.fori_loop(0, n_chunk, step, jnp.zeros((1, 2 * half), F32))
    y8 = _dot(u8, toep_ref[...]) + _dot(hp_ref[...].astype(BF16), wout_ref[...])
    for t in range(S5_CHUNK):
        y_ref[pl.ds(t, n_chunk, stride=S5_CHUNK), :] = y8[:, t * LANES:(t + 1) * LANES]


def s5_prompt_weights(lam_re, lam_im, log_dt, b_re, b_im, c_re, c_im):
    chunk = S5_CHUNK
    dt = jnp.exp(log_dt)[:, None]
    tau = jnp.arange(chunk + 1, dtype=F32)[:, None, None]
    mag = jnp.exp(lam_re * dt * tau)
    pw_r = mag * jnp.cos(lam_im * dt * tau)
    pw_i = mag * jnp.sin(lam_im * dt * tau)
    abar_r, abar_i = pw_r[1], pw_i[1]
    den = lam_re * lam_re + lam_im * lam_im
    fr = ((abar_r - 1.0) * lam_re + abar_i * lam_im) / den
    fi = (abar_i * lam_re - (abar_r - 1.0) * lam_im) / den
    bbr = fr[..., None] * b_re - fi[..., None] * b_im
    bbi = fr[..., None] * b_im + fi[..., None] * b_re
    ab_r = pw_r[..., None] * bbr - pw_i[..., None] * bbi
    ab_i = pw_r[..., None] * bbi + pw_i[..., None] * bbr
    kern = jnp.sum(c_re[None, :, :, :, None] * ab_r[:chunk, :, None] - c_im[None, :, :, :, None] * ab_i[:chunk, :, None],
                   axis=3)
    s_idx = jnp.arange(chunk)[:, None]
    t_idx = jnp.arange(chunk)[None, :]
    lag = t_idx - s_idx
    k_st = kern[jnp.maximum(lag, 0)]
    k_st = jnp.where((lag >= 0)[:, :, None, None, None], k_st, 0.0)
    split = lambda x, axis: x.reshape(x.shape[:axis] + (N_SUPER, S5_PACK) + x.shape[axis + 1:])
    width = chunk * LANES
    n_state = S5_PACK * S5_STATE
    m_toep = split(k_st, 2).transpose(2, 0, 3, 5, 1, 4).reshape(N_SUPER, width, LANES)
    rev = chunk - 1 - jnp.arange(chunk)
    ab_rev = jnp.stack([ab_r[rev], ab_i[rev]])
    m_st = split(ab_rev, 2).transpose(2, 1, 3, 5, 0, 4).reshape(N_SUPER, width, LANES)
    ar, ai = pw_r[1:, :, None, :], pw_i[1:, :, None, :]
    co_r = c_re[None] * ar - c_im[None] * ai
    co_i = -c_re[None] * ai - c_im[None] * ar
    co = jnp.stack([co_r, co_i])
    m_out = split(co, 2).transpose(2, 0, 3, 5, 1, 4).reshape(N_SUPER, 2 * n_state, LANES)
    src = jnp.arange(LANES)[:, None]
    col = jnp.arange(width)[None, :]
    e_to = ((src // S5_CH == col // LANES) & (src % S5_CH == col % S5_CH)).astype(BF16)
    e_st = ((src // S5_STATE == col // n_state) & (src % S5_STATE == col % S5_STATE)).astype(BF16)
    last_r = pw_r[chunk].reshape(N_SUPER, 1, n_state)
    last_i = pw_i[chunk].reshape(N_SUPER, 1, n_state)
    a1 = jnp.concatenate([last_r, last_r], axis=2)
    a2 = jnp.concatenate([-last_i, last_i], axis=2)
    return m_toep, m_st, m_out, e_to, e_st, a1, a2


def s5_prompt(mid, weights):
    n_chunk = SEQ // S5_CHUNK
    n_state2 = 2 * S5_PACK * S5_STATE
    width = S5_CHUNK * LANES
    assert width == n_state2
    seq_blk = pl.BlockSpec((SEQ, LANES), lambda g, b: (b, g))
    per_g = lambda shape: pl.BlockSpec((1,) + shape, lambda g, b: (g, 0, 0))
    spread = pl.BlockSpec((LANES, width), lambda g, b: (0, 0))
    y, h_last = _call(
        _s5_prompt_body, grid=(N_SUPER, BATCH),
        in_specs=[seq_blk, per_g((width, LANES)), per_g((width, LANES)), per_g((n_state2, LANES)),
                  spread, spread, per_g((1, n_state2)), per_g((1, n_state2))],
        out_specs=[seq_blk, pl.BlockSpec((1, 1, 1, n_state2), lambda g, b: (b, g, 0, 0))],
        out_shape=[jax.ShapeDtypeStruct((M_PROMPT, S5_WIDTH), F32),
                   jax.ShapeDtypeStruct((BATCH, N_SUPER, 1, n_state2), F32)],
        scratch_shapes=[pltpu.VMEM((n_chunk, n_state2), F32), pltpu.VMEM((n_chunk, n_state2), F32)]
        + [pltpu.VMEM((width, width), BF16)] * 3,
        name="s5_prompt")(mid, *weights)
    h = h_last.reshape(BATCH, N_SUPER, 2, S5_PACK, S5_STATE).transpose(0, 1, 3, 4, 2)
    return y, h.reshape(BATCH, S5_GROUPS, S5_STATE, 2)


def _s5_post_body(y_ref, u_ref, d_ref, w_ref, b_ref, o_ref):
    z = _gelu(y_ref[...] + d_ref[...] * u_ref[...])
    lin = _dot(z.astype(BF16), w_ref[...].astype(BF16))
    o_ref[...] = (z * _sigmoid(lin + b_ref[...])).astype(o_ref.dtype)


def s5_post(y, mid, d_skip, w_glu, b_glu, tm):
    m = y.shape[0]
    row = lambda i: (i, 0)
    fixed = lambda i: (0, 0)
    return _call(
        _s5_post_body, grid=(m // tm,),
        in_specs=[pl.BlockSpec((tm, S5_WIDTH), row), pl.BlockSpec((tm, S5_WIDTH), row),
                  pl.BlockSpec((1, S5_WIDTH), fixed), pl.BlockSpec((S5_WIDTH, S5_WIDTH), fixed),
                  pl.BlockSpec((1, S5_WIDTH), fixed)],
        out_specs=pl.BlockSpec((tm, S5_WIDTH), row),
        out_shape=jax.ShapeDtypeStruct((m, S5_WIDTH), BF16),
        name="s5_post")(y, mid, d_skip.reshape(1, -1), w_glu, b_glu.reshape(1, -1))


def _s5_decode_body(u_ref, hr_ref, hi_ref, ar_ref, ai_ref, bb_ref, cc_ref, y_ref, sr_ref, si_ref):
    n = hr_ref.shape[1]
    bu = _dot(u_ref[...].astype(BF16), bb_ref[...])
    ar, ai = ar_ref[...], ai_ref[...]
    hr, hi = hr_ref[...], hi_ref[...]
    sr = bu[:, :n] + (ar * hr - ai * hi)
    si = bu[:, n:] + (ar * hi + ai * hr)
    sr_ref[...] = sr
    si_ref[...] = si
    s_cat = jnp.concatenate([sr, si], axis=1).astype(BF16)
    y_ref[...] = lax.dot_general(s_cat, cc_ref[...], (((1,), (1,)), ((), ())), preferred_element_type=F32)


def s5_decode(mid, state, lam_re, lam_im, log_dt, b_re, b_im, c_re, c_im):
    dt = jnp.exp(log_dt)[:, None]
    mag = jnp.exp(lam_re * dt)
    abar_r, abar_i = mag * jnp.cos(lam_im * dt), mag * jnp.sin(lam_im * dt)
    den = lam_re * lam_re + lam_im * lam_im
    fr = ((abar_r - 1.0) * lam_re + abar_i * lam_im) / den
    fi = (abar_i * lam_re - (abar_r - 1.0) * lam_im) / den
    bbr = fr[..., None] * b_re - fi[..., None] * b_im
    bbi = fr[..., None] * b_im + fi[..., None] * b_re
    n_state = S5_GROUPS * S5_STATE
    same_group = (jnp.arange(S5_WIDTH)[:, None] // S5_CH) == (jnp.arange(n_state)[None, :] // S5_STATE)

    def block_diag(w):
        tiled = jnp.tile(w.reshape(S5_WIDTH, S5_STATE), (1, S5_GROUPS))
        return jnp.where(same_group, tiled, 0.0)

    in_r, in_i = block_diag(bbr.transpose(0, 2, 1)), block_diag(bbi.transpose(0, 2, 1))
    bb = jnp.concatenate([in_r, in_i], axis=1).astype(BF16)
    cc_t = jnp.concatenate([block_diag(c_re), -block_diag(c_im)], axis=1).astype(BF16)
    nb = DEC_BATCH
    fixed = lambda i: (0, 0)
    st = pl.BlockSpec((nb, n_state), fixed)
    vec = pl.BlockSpec((1, n_state), fixed)
    y, sr, si = _call(
        _s5_decode_body, grid=(1,),
        in_specs=[pl.BlockSpec((nb, S5_WIDTH), fixed), st, st, vec, vec,
                  pl.BlockSpec((S5_WIDTH, 2 * n_state), fixed), pl.BlockSpec((S5_WIDTH, 2 * n_state), fixed)],
        out_specs=[pl.BlockSpec((nb, S5_WIDTH), fixed), st, st],
        out_shape=[jax.ShapeDtypeStruct((nb, S5_WIDTH), F32), jax.ShapeDtypeStruct((nb, n_state), F32),
                   jax.ShapeDtypeStruct((nb, n_state), F32)],
        name="s5_decode")(mid, state[..., 0].reshape(nb, n_state), state[..., 1].reshape(nb, n_state),
                          abar_r.reshape(1, n_state), abar_i.reshape(1, n_state), bb, cc_t)
    new_state = jnp.stack([sr.reshape(nb, S5_GROUPS, S5_STATE), si.reshape(nb, S5_GROUPS, S5_STATE)], axis=-1)
    return y, new_state


def _lru_gates(xc, wa_ref, ba_ref, wi_ref, bi_ref, c8_ref):
    xb = xc.astype(BF16)
    r = _sigmoid(_dot(xb, wa_ref[...]) + ba_ref[...])
    ig = _sigmoid(_dot(xb, wi_ref[...]) + bi_ref[...])
    log_a = c8_ref[...] * r
    a = jnp.exp(log_a)
    b = jnp.sqrt(-jnp.tanh(log_a) * (a * a + 1.0)) * (ig * xc)
    return a, b


def _lru_pre_body(x_ref, p_ref, cw_ref, cb_ref, wa_ref, ba_ref, wi_ref, bi_ref, c8_ref,
                  a_ref, b_ref, ext_ref, *, tiles_per_seq):
    tm = x_ref.shape[0]
    first = pl.program_id(0) % tiles_per_seq == 0
    ext_ref[0:8, :] = jnp.where(first, 0.0, p_ref[...])
    ext_ref[8:, :] = x_ref[...]
    xc = cb_ref[...] + cw_ref[3:4, :] * x_ref[...]
    for k in (1, 2, 3):
        xc = xc + cw_ref[3 - k:4 - k, :] * ext_ref[8 - k:8 - k + tm, :]
    a, b = _lru_gates(xc, wa_ref, ba_ref, wi_ref, bi_ref, c8_ref)
    a_ref[...] = a
    b_ref[...] = b


def _lru_scan_body(a_ref, b_ref, g_ref, y_ref, hl_ref, h_ref, p_ref):
    t_len = a_ref.shape[0]
    seg_len = t_len // LRU_SEG
    n_q = LRU_SEG // 8

    def step(t, carry):
        out = []
        for qi in range(n_q):
            h, p = carry[2 * qi], carry[2 * qi + 1]
            rows = pl.ds(qi * 8 * seg_len + t, 8, stride=seg_len)
            at = a_ref[rows, :]
            h = at * h + b_ref[rows, :]
            p = at * p
            h_ref[rows, :] = h
            p_ref[rows, :] = p
            out += [h, p]
        return tuple(out)

    init = (jnp.zeros((8, LANES), F32), jnp.ones((8, LANES), F32)) * n_q
    lax.fori_loop(0, seg_len, step, init)
    carry = jnp.zeros((1, LANES), F32)
    for s in range(LRU_SEG):
        rows = slice(s * seg_len, (s + 1) * seg_len)
        h = h_ref[rows, :] + p_ref[rows, :] * carry
        y_ref[rows, :] = (h * _gelu(g_ref[rows, :])).astype(y_ref.dtype)
        carry = h[seg_len - 1:seg_len, :]
    hl_ref[0] = carry


def lru_weights(conv_w, conv_b, w_a, b_a, w_i, b_i, lam):
    eye = jnp.eye(LRU_HEADS, dtype=F32)
    bd = lambda w: jnp.einsum("hij,hk->hikj", w, eye).reshape(LRU_WIDTH, LRU_WIDTH).astype(BF16)
    c8 = (-LRU_C) * jax.nn.softplus(-lam)
    return (conv_w, conv_b.reshape(1, -1), bd(w_a), b_a.reshape(1, -1), bd(w_i), b_i.reshape(1, -1),
            c8.reshape(1, -1))


def lru_prompt(mid, lw, tm):
    conv_w, conv_b, wa, ba, wi, bi, c8 = lw
    fixed = lambda i: (0, 0)
    vec = pl.BlockSpec((1, LRU_WIDTH), fixed)
    mat = pl.BlockSpec((LRU_WIDTH, LRU_WIDTH), fixed)
    a, b = _call(
        functools.partial(_lru_pre_body, tiles_per_seq=SEQ // tm), grid=(M_PROMPT // tm,),
        in_specs=[pl.BlockSpec((tm, LRU_WIDTH), lambda i: (i, 1)),
                  pl.BlockSpec((8, LRU_WIDTH), lambda i: (jnp.maximum(i * (tm // 8) - 1, 0), 1)),
                  pl.BlockSpec((4, LRU_WIDTH), fixed), vec, mat, vec, mat, vec, vec],
        out_specs=[pl.BlockSpec((tm, LRU_WIDTH), lambda i: (i, 0))] * 2,
        out_shape=[jax.ShapeDtypeStruct((M_PROMPT, LRU_WIDTH), F32)] * 2,
        scratch_shapes=[pltpu.VMEM((tm + 8, LRU_WIDTH), F32)],
        name="lru_pre")(mid, mid, conv_w, conv_b, wa, ba, wi, bi, c8)
    n_lane = LRU_WIDTH // LANES
    g_col0 = (S5_WIDTH + LRU_WIDTH) // LANES
    blk = lambda off: pl.BlockSpec((SEQ, LANES), lambda bi_, c: (bi_, off + c))
    y, h_last = _call(
        _lru_scan_body, grid=(BATCH, n_lane),
        in_specs=[blk(0), blk(0), blk(g_col0)],
        out_specs=[blk(0), pl.BlockSpec((1, 1, LANES), lambda bi_, c: (bi_, 0, c))],
        out_shape=[jax.ShapeDtypeStruct((M_PROMPT, LRU_WIDTH), BF16),
                   jax.ShapeDtypeStruct((BATCH, 1, LRU_WIDTH), F32)],
        scratch_shapes=[pltpu.VMEM((SEQ, LANES), F32), pltpu.VMEM((SEQ, LANES), F32)],
        name="lru_scan")(a, b, mid)
    return y, h_last.reshape(BATCH, LRU_WIDTH)


def _lru_decode_body(x_ref, g_ref, c0_ref, c1_ref, c2_ref, h0_ref, cw_ref, cb_ref, wa_ref, ba_ref,
                     wi_ref, bi_ref, c8_ref, y_ref, h_ref):
    xc = (cb_ref[...] + cw_ref[0:1, :] * c0_ref[...] + cw_ref[1:2, :] * c1_ref[...]
          + cw_ref[2:3, :] * c2_ref[...] + cw_ref[3:4, :] * x_ref[...])
    a, b = _lru_gates(xc, wa_ref, ba_ref, wi_ref, bi_ref, c8_ref)
    h = a * h0_ref[...] + b
    h_ref[...] = h
    y_ref[...] = (h * _gelu(g_ref[...])).astype(y_ref.dtype)


def lru_decode(mid, conv_buf, h0, lw):
    conv_w, conv_b, wa, ba, wi, bi, c8 = lw
    nb = DEC_BATCH
    fixed = lambda i: (0, 0)
    vec = pl.BlockSpec((1, LRU_WIDTH), fixed)
    mat = pl.BlockSpec((LRU_WIDTH, LRU_WIDTH), fixed)
    st = pl.BlockSpec((nb, LRU_WIDTH), fixed)
    return _call(
        _lru_decode_body, grid=(1,),
        in_specs=[pl.BlockSpec((nb, LRU_WIDTH), lambda i: (0, 1)), pl.BlockSpec((nb, LRU_WIDTH), lambda i: (0, 2)),
                  st, st, st, st, pl.BlockSpec((4, LRU_WIDTH), fixed), vec, mat, vec, mat, vec, vec],
        out_specs=[st, st],
        out_shape=[jax.ShapeDtypeStruct((nb, LRU_WIDTH), BF16), jax.ShapeDtypeStruct((nb, LRU_WIDTH), F32)],
        name="lru_decode")(mid, mid, conv_buf[:, 0], conv_buf[:, 1], conv_buf[:, 2], h0,
                           conv_w, conv_b, wa, ba, wi, bi, c8)


TM = 1024
TM_SMALL = 512
TM_EXPERT = 512
M_ALL = M_PROMPT + DEC_BATCH
N_EXPERT_TILES = (2 * M_ALL) // TM_EXPERT + N_EXPERTS
TM_ROUTER = M_ALL // 16
DEC_ATT_BS = (16, 8, 2)


def _token_mixers(layer, xp, xs, p, caches_t, win_prev, state_s5, state_conv, state_lru):
    n_p = rmsnorm(xp, p["norm_mix"][layer], TM_SMALL, BF16)
    n_s = rmsnorm(xs, p["norm_mix"][layer], DEC_BATCH, BF16)
    w_in = p["w_in"]
    proj = functools.partial(matmul_ws, [n_p], [n_s], w_of_x=[0], tm=TM)
    qkv_p, qkv_s = proj([(w_in, (layer,))], n_cols=OFF_MID, tn=768, out_dtype=F32, epilogue=_ep_plain,
                        name="proj_qkv")
    mid_p, mid_s = proj([(w_in[layer, :, OFF_MID:OFF_GATE], ())], tn=640, out_dtype=F32, epilogue=_ep_plain,
                        name="proj_mid")
    gates_p, gates_s = proj([(w_in[layer, :, OFF_GATE:], ())], tn=1024, out_dtype=BF16, out_dtype_s=F32,
                            epilogue=_ep_sigmoid, name="proj_gates")

    qn_p, kn_p = qk_norm(qkv_p, p["q_gain"][layer], p["k_gain"][layer], TM_SMALL, F32)
    qn_s, kn_s = qk_norm(qkv_s, p["q_gain"][layer], p["k_gain"][layer], DEC_BATCH, F32)
    outs, lses, outs_s, lses_s = [], [], [], []
    new_p, win = [], []
    for g, (window, dil) in enumerate(ATT_GROUPS):
        cols = slice(GROUP_W * g, GROUP_W * (g + 1))
        v_cols = slice(2 * ATT_WIDTH + GROUP_W * g, 2 * ATT_WIDTH + GROUP_W * (g + 1))
        o, lse = prompt_attention(g, qn_p, kn_p, qkv_p)
        outs.append(o)
        lses.append(lse)
        keep = min(window, SEQ)
        tail = lambda x, cs: (x.reshape(BATCH, SEQ, x.shape[1])[:, SEQ - keep:, cs]
                              .reshape(BATCH, keep, HEADS, HEAD_DIM))
        new_p.append(tail(kn_p, cols))
        new_p.append(tail(qkv_p, v_cols))
        cols_s = jnp.concatenate([qn_s[:, cols], kn_s[:, cols], qkv_s[:, v_cols]], axis=1)
        prev_k, prev_v = (None, None) if win_prev is None else (win_prev[2 * g], win_prev[2 * g + 1])
        o_s, lse_s, wk, wv = decode_attention(layer, g, cols_s, caches_t[2 * g], caches_t[2 * g + 1],
                                              prev_k, prev_v, DEC_ATT_BS[g])
        outs_s.append(o_s)
        lses_s.append(lse_s)
        win += [wk, wv]
    att_p = mix_groups(outs, lses, TM_SMALL, BF16)
    att_s = mix_groups(outs_s, lses_s, DEC_BATCH, BF16)

    s5p = (p["s5_lam_re"][layer], p["s5_lam_im"][layer], p["s5_log_dt"][layer], p["s5_b_re"][layer],
           p["s5_b_im"][layer], p["s5_c_re"][layer], p["s5_c_im"][layer])
    y_p, h_p = s5_prompt(mid_p, s5_prompt_weights(*s5p))
    y_s, h_s = s5_decode(mid_s, state_s5[layer], *s5p)
    glu = (p["s5_d"][layer], p["s5_w_glu"][layer], p["s5_b_glu"][layer])
    s5_p = s5_post(y_p, mid_p, *glu, TM_SMALL)
    s5_s = s5_post(y_s, mid_s, *glu, DEC_BATCH)
    new_p.append(h_p)
    new_s = [h_s]

    lw = lru_weights(p["lru_conv_w"][layer], p["lru_conv_b"][layer], p["lru_w_a"][layer], p["lru_b_a"][layer],
                     p["lru_w_i"][layer], p["lru_b_i"][layer], p["lru_lam"][layer])
    lru_p, h_lp = lru_prompt(mid_p, lw, 512)
    lru_s, h_ls = lru_decode(mid_s, state_conv[layer], state_lru[layer], lw)
    x_lru = slice(S5_WIDTH, S5_WIDTH + LRU_WIDTH)
    new_p.append(mid_p.reshape(BATCH, SEQ, MID_WIDTH)[:, SEQ - 3:, x_lru])
    new_p.append(h_lp)
    new_s.append(jnp.concatenate([state_conv[layer, :, 1:], mid_s[:, None, x_lru]], axis=1))
    new_s.append(h_ls)

    tn = 1024
    gate_tiles = D_MODEL // tn
    merged_p, merged_s = matmul_ws(
        [att_p, s5_p, lru_p], [att_s, s5_s, lru_s],
        [(p["w_br_attn"], (layer,)), (p["w_br_s5"], (layer,)), (p["w_br_lru"], (layer,))],
        w_of_x=[0, 1, 2], tm=TM, tn=tn, out_dtype=BF16, epilogue=_ep_merge,
        tile_extras=[(gates_p, gates_s, 0), (gates_p, gates_s, gate_tiles), (gates_p, gates_s, 2 * gate_tiles)],
        name="merge")
    xp, xs = matmul_ws([merged_p], [merged_s], [(p["w_o"], (layer,))], w_of_x=[0], tm=TM, tn=tn,
                       out_dtype=F32, epilogue=_ep_residual, tile_extras=[(xp, xs, 0)], name="out_proj")
    return xp, xs, new_p, new_s, win


def _dense_ffn(xp, xs, p, layer, j):
    n_p = rmsnorm(xp, p["norm_ffn"][layer], TM_SMALL, BF16)
    n_s = rmsnorm(xs, p["norm_ffn"][layer], DEC_BATCH, BF16)
    h_p, h_s = matmul_ws([n_p], [n_s], [(p["ffn_w_gate"], (j,)), (p["ffn_w_up"], (j,))], w_of_x=[0, 0],
                         tm=TM, tn=512, out_dtype=BF16, epilogue=_ep_swiglu, name="ffn_up")
    return matmul_ws([h_p], [h_s], [(p["ffn_w_down"], (j,))], w_of_x=[0], tm=512, tn=512, out_dtype=F32,
                     epilogue=_ep_residual, tile_extras=[(xp, xs, 0)], name="ffn_down")


def _moe_ffn(xp, xs, p, layer, j):
    router = (p["norm_ffn"][layer], p["moe_router_w"][j], p["moe_router_b"][j])
    n, comb, idx = rmsnorm_router(jnp.concatenate([xp, xs], axis=0), *router, TM_ROUTER)
    idx = idx[:, :2]
    m = M_ALL
    top_e = idx.reshape(-1)
    top_w = jnp.take_along_axis(comb, idx, axis=1).reshape(-1)
    counts = jnp.sum(top_e[:, None] == jnp.arange(N_EXPERTS)[None, :], axis=0).astype(jnp.int32)
    padded = ((counts + TM_EXPERT - 1) // TM_EXPERT) * TM_EXPERT
    pad_end = jnp.cumsum(padded)
    pad_start = pad_end - padded
    start = jnp.cumsum(counts) - counts
    order = jnp.argsort(top_e, stable=True).astype(jnp.int32)
    rank = jnp.argsort(order).astype(jnp.int32)
    dest = pad_start[top_e] + rank - start[top_e]
    n_rows = N_EXPERT_TILES * TM_EXPERT
    tile_start = jnp.arange(N_EXPERT_TILES, dtype=jnp.int32) * TM_EXPERT
    tile_expert = jnp.minimum(jnp.sum(tile_start[:, None] >= pad_end[None, :], axis=1), N_EXPERTS - 1)
    tile_expert = tile_expert.astype(jnp.int32)
    n_used = (pad_end[-1] // TM_EXPERT).reshape(1).astype(jnp.int32)
    row_e = jnp.repeat(tile_expert, TM_EXPERT)
    within = jnp.arange(n_rows, dtype=jnp.int32) - pad_start[row_e]
    row_valid = within < counts[row_e]
    row_pair = order[jnp.clip(start[row_e] + within, 0, 2 * m - 1)]
    src_token = jnp.where(row_valid, row_pair // 2, 0)
    row_w = jnp.where(row_valid, top_w[row_pair], 0.0)
    x_sorted = jnp.take(n, src_token, axis=0, mode="clip")
    gmm = functools.partial(grouped_matmul_ws, tile_expert, n_used, lead=(j,), tm=TM_EXPERT)
    gate = gmm(x_sorted, p["moe_w_gate"], tn=1408, out_dtype=BF16, epilogue=_ep_plain, name="moe_gate")
    h = gmm(x_sorted, p["moe_w_up"], tn=1408, out_dtype=BF16, epilogue=_ep_silu_times, tile_extras=[gate],
            name="moe_up")
    ys = gmm(h, p["moe_w_down"], tn=1024, out_dtype=F32, epilogue=_ep_scale_rows,
             col_extras=[row_w.reshape(-1, 1)], name="moe_down")
    dest2 = dest.reshape(m, 2)
    y0 = jnp.take(ys, dest2[:, 0], axis=0, mode="clip")
    y1 = jnp.take(ys, dest2[:, 1], axis=0, mode="clip")
    return xp + (y0[:M_PROMPT] + y1[:M_PROMPT]), xs + (y0[M_PROMPT:] + y1[M_PROMPT:])


def kernel(x_prompt, x_sample, cache_k_w128, cache_v_w128, cache_k_w512, cache_v_w512, cache_k_w2048, cache_v_w2048, state_s5, state_conv, state_lru, norm_mix, w_in, q_gain, k_gain, s5_lam_re, s5_lam_im, s5_log_dt, s5_b_re, s5_b_im, s5_c_re, s5_c_im, s5_d, s5_w_glu, s5_b_glu, lru_conv_w, lru_conv_b, lru_w_a, lru_b_a, lru_w_i, lru_b_i, lru_lam, w_br_attn, w_br_s5, w_br_lru, w_o, norm_ffn, ffn_w_gate, ffn_w_up, ffn_w_down, moe_router_w, moe_router_b, moe_w_gate, moe_w_up, moe_w_down):
    p = dict(norm_mix=norm_mix, w_in=w_in, q_gain=q_gain, k_gain=k_gain, s5_lam_re=s5_lam_re,
             s5_lam_im=s5_lam_im, s5_log_dt=s5_log_dt, s5_b_re=s5_b_re, s5_b_im=s5_b_im, s5_c_re=s5_c_re,
             s5_c_im=s5_c_im, s5_d=s5_d, s5_w_glu=s5_w_glu, s5_b_glu=s5_b_glu, lru_conv_w=lru_conv_w,
             lru_conv_b=lru_conv_b, lru_w_a=lru_w_a, lru_b_a=lru_b_a, lru_w_i=lru_w_i, lru_b_i=lru_b_i,
             lru_lam=lru_lam, w_br_attn=w_br_attn, w_br_s5=w_br_s5, w_br_lru=w_br_lru, w_o=w_o,
             norm_ffn=norm_ffn, ffn_w_gate=ffn_w_gate, ffn_w_up=ffn_w_up, ffn_w_down=ffn_w_down,
             moe_router_w=moe_router_w, moe_router_b=moe_router_b, moe_w_gate=moe_w_gate,
             moe_w_up=moe_w_up, moe_w_down=moe_w_down)
    caches_t = [jnp.transpose(c, (0, 1, 3, 4, 2)) for c in
                (cache_k_w128, cache_v_w128, cache_k_w512, cache_v_w512, cache_k_w2048, cache_v_w2048)]
    xp = x_prompt.reshape(M_PROMPT, D_MODEL)
    xs = x_sample.reshape(DEC_BATCH, D_MODEL)
    states_p = [[] for _ in range(9)]
    states_s = [[] for _ in range(3)]
    win = None
    for layer in range(DEPTH):
        xp, xs, new_p, new_s, win = _token_mixers(layer, xp, xs, p, caches_t, win, state_s5, state_conv,
                                                  state_lru)
        if layer % 2 == 0:
            xp, xs = _dense_ffn(xp, xs, p, layer, layer // 2)
        else:
            xp, xs = _moe_ffn(xp, xs, p, layer, layer // 2)
        for i in range(9):
            states_p[i].append(new_p[i])
        for i in range(3):
            states_s[i].append(new_s[i])
    y_prompt = xp.reshape(BATCH, SEQ, D_MODEL)
    y_sample = xs.reshape(DEC_BATCH, 1, D_MODEL)
    windows = [jnp.transpose(w, (0, 1, 4, 2, 3)) for w in win]
    return (y_prompt, y_sample, *[jnp.stack(a) for a in states_p], *windows,
            *[jnp.stack(a) for a in states_s])
```
